```python
import math
import jax, jax.numpy as jnp
from jax import lax
import numpy as np

D_MODEL = 1024
BATCH = 1
SEQ = 16384
DEPTH = 1
DEC_BATCH = 16
DEC_SEQ = 4096
PAST_LEN = 128

D_F = D_MODEL
N_FGROUPS = 4
D_INNER = 2 * D_MODEL
HEADDIM = 64
N_HEADS = D_INNER // HEADDIM
N_GROUPS = 8
D_STATE = 128
D_CONV = 5
CONV_PAD = D_CONV // 2
CHUNK = 128
CONV_DIM = D_INNER + 2 * N_GROUPS * D_STATE
N_EXPERTS = 32
TOP_K = 4
D_FF = D_MODEL
SWIGLU_ALPHA = 1.702
SWIGLU_LIMIT = 7.0
EPS = 1e-5
PROJ_DIM = D_F + D_INNER + CONV_DIM + 2 * N_HEADS + 2 * D_MODEL
SPLIT_IDX = (D_F, D_F + D_INNER, D_F + D_INNER + CONV_DIM, D_F + D_INNER + CONV_DIM + 2 * N_HEADS)

kernel_name = "hybrid_fnet_ssd_moe_encoder"


def rmsnorm(x, w):
    xf = x.astype(jnp.float32)
    y = xf * lax.rsqrt(jnp.mean(xf * xf, axis=-1, keepdims=True) + EPS)
    return (y * w.astype(jnp.float32)).astype(x.dtype)


def dwconv_centred(u, w, b):
    out = lax.conv_general_dilated(
        u, w[:, None, :].astype(u.dtype), window_strides=(1,), padding=[(CONV_PAD, CONV_PAD)],
        dimension_numbers=('NWC', 'WIO', 'NWC'), feature_group_count=u.shape[-1])
    return out + b.astype(u.dtype)


def ssd_chunked(xh, dt, A, Bm, Cm):
    bsz, L, _, _ = xh.shape
    nc = L // CHUNK
    R = N_HEADS // N_GROUPS
    f32 = jnp.float32
    X = (xh.astype(f32) * dt[..., None]).reshape(bsz, nc, CHUNK, N_GROUPS, R, HEADDIM)
    Ad = jnp.moveaxis((dt * A).reshape(bsz, nc, CHUNK, N_GROUPS, R), 2, -1)
    Acs = jnp.cumsum(Ad, axis=-1)
    Bc = Bm.astype(f32).reshape(bsz, nc, CHUNK, N_GROUPS, D_STATE)
    Cc = Cm.astype(f32).reshape(bsz, nc, CHUNK, N_GROUPS, D_STATE)
    seg = Acs[..., :, None] - Acs[..., None, :]
    lower = jnp.tril(jnp.ones((CHUNK, CHUNK), dtype=bool))
    Lmat = jnp.exp(jnp.where(lower, seg, -jnp.inf))
    CB = jnp.einsum('bclgn,bcsgn->bcgls', Cc, Bc)
    y_diag = jnp.einsum('bcgrls,bcsgrp->bclgrp', CB[:, :, :, None] * Lmat, X)
    decay_states = jnp.exp(Acs[..., -1:] - Acs)
    states = jnp.einsum('bclgn,bcgrl,bclgrp->bcgrpn', Bc, decay_states, X)
    chunk_decay = jnp.exp(Acs[..., -1])

    def step(prev, inp):
        st, dec = inp
        return prev * dec[..., None, None] + st, prev

    init = jnp.zeros((bsz, N_GROUPS, R, HEADDIM, D_STATE), f32)
    _, prev_states = lax.scan(step, init, (jnp.moveaxis(states, 1, 0), jnp.moveaxis(chunk_decay, 1, 0)))
    prev_states = jnp.moveaxis(prev_states, 0, 1)
    y_off = jnp.einsum('bclgn,bcgrpn,bcgrl->bclgrp', Cc, prev_states, jnp.exp(Acs))
    return (y_diag + y_off).reshape(bsz, L, N_HEADS, HEADDIM)


def moe(h, w_router, b_router, w_gate_up, b_gate_up, w_down, b_down):
    shp = h.shape
    t = h.reshape(-1, D_MODEL)
    logits = (t @ w_router).astype(jnp.float32) + b_router.astype(jnp.float32)
    vals, idx = lax.top_k(logits, TOP_K)
    probs = jax.nn.softmax(vals, axis=-1)
    gates = jnp.sum(jax.nn.one_hot(idx, N_EXPERTS, dtype=jnp.float32) * probs[..., None], axis=1)

    def expert(acc, inp):
        w1, b1, w2, b2, g = inp
        hu = t @ w1 + b1
        glu, lin = jnp.split(hu, 2, axis=-1)
        glu = jnp.minimum(glu, SWIGLU_LIMIT)
        lin = jnp.clip(lin, -SWIGLU_LIMIT, SWIGLU_LIMIT)
        act = glu * jax.nn.sigmoid(SWIGLU_ALPHA * glu) * (lin + 1)
        out = act @ w2 + b2
        return acc + g[:, None].astype(out.dtype) * out, None

    acc, _ = lax.scan(expert, jnp.zeros_like(t), (w_gate_up, b_gate_up, w_down, b_down, gates.T))
    return acc.reshape(shp)


def _layer(x, norm_mix, w_in, conv_w, conv_b, dt_bias_fwd, dt_bias_bwd, a_log_fwd, a_log_bwd,
           d_skip, ssm_norm, w_fourier, w_ssm_out, w_out, norm_ffn, w_router, b_router,
           w_gate_up, b_gate_up, w_down, b_down):
    f32 = jnp.float32
    bsz, L, _ = x.shape
    h = rmsnorm(x, norm_mix)
    proj = jnp.einsum('bld,de->ble', h, w_in)
    f_in, z, xbc, dt_raw, gate_logits = jnp.split(proj, SPLIT_IDX, axis=-1)
    fg = f_in.astype(f32).reshape(bsz, L, N_FGROUPS, D_F // N_FGROUPS)
    f_mix = jnp.fft.fft2(fg, axes=(1, 3), norm='ortho').real.reshape(bsz, L, D_F).astype(x.dtype)
    u_f = jnp.einsum('ble,ed->bld', f_mix, w_fourier)
    xbc = jax.nn.silu(dwconv_centred(xbc, conv_w, conv_b))
    xs, Bm, Cm = jnp.split(xbc, (D_INNER, D_INNER + N_GROUPS * D_STATE), axis=-1)
    xs = xs.reshape(bsz, L, N_HEADS, HEADDIM)
    Bm = Bm.reshape(bsz, L, N_GROUPS, D_STATE)
    Cm = Cm.reshape(bsz, L, N_GROUPS, D_STATE)
    dt_f, dt_b = jnp.split(dt_raw.astype(f32), 2, axis=-1)
    dt_f = jax.nn.softplus(dt_f + dt_bias_fwd.astype(f32))
    dt_b = jax.nn.softplus(dt_b + dt_bias_bwd.astype(f32))
    y_fwd = ssd_chunked(xs, dt_f, -jnp.exp(a_log_fwd.astype(f32)), Bm, Cm)
    rev = lambda a: jnp.flip(a, axis=1)
    y_bwd = rev(ssd_chunked(rev(xs), rev(dt_b), -jnp.exp(a_log_bwd.astype(f32)), rev(Bm), rev(Cm)))
    y = y_fwd + y_bwd + d_skip.astype(f32)[:, None] * xs.astype(f32)
    y = y.reshape(bsz, L, D_INNER) * jax.nn.silu(z.astype(f32))
    y = rmsnorm(y, ssm_norm).astype(x.dtype)
    u_s = jnp.einsum('ble,ed->bld', y, w_ssm_out)
    g = jax.nn.sigmoid(gate_logits.astype(f32)).reshape(bsz, L, 2, D_MODEL)
    merged = (g[..., 0, :] * u_f.astype(f32) + g[..., 1, :] * u_s.astype(f32)).astype(x.dtype)
    x = x + jnp.einsum('bld,de->ble', merged, w_out)
    x = x + moe(rmsnorm(x, norm_ffn), w_router, b_router, w_gate_up, b_gate_up, w_down, b_down)
    return x


def setup_inputs(seed: int = 0) -> dict:
    key = jax.random.key(seed)
    ks = jax.random.split(key, 24)
    nrm = lambda k, shape, scale: jax.random.normal(k, shape, jnp.float32) * scale
    gain = lambda k, shape: 1.0 + 0.02 * jax.random.normal(k, shape, jnp.float32)

    def dt_bias(k):
        u = jax.random.uniform(k, (DEPTH, N_HEADS), jnp.float32)
        dt = jnp.exp(u * (math.log(0.1) - math.log(0.001)) + math.log(0.001))
        return dt + jnp.log(-jnp.expm1(-dt))

    def a_log(k):
        return jnp.log(jax.random.uniform(k, (DEPTH, N_HEADS), jnp.float32, 1.0, 16.0))

    return {
        'x_prompt': nrm(ks[0], (BATCH, SEQ, D_MODEL), 1.0),
        'x_sample': nrm(ks[1], (DEC_BATCH, DEC_SEQ, D_MODEL), 1.0),
        'norm_mix': gain(ks[2], (DEPTH, D_MODEL)),
        'w_in': nrm(ks[3], (DEPTH, D_MODEL, PROJ_DIM), D_MODEL ** -0.5),
        'conv_w': nrm(ks[4], (DEPTH, D_CONV, CONV_DIM), D_CONV ** -0.5),
        'conv_b': nrm(ks[5], (DEPTH, CONV_DIM), 0.02),
        'dt_bias_fwd': dt_bias(ks[6]),
        'dt_bias_bwd': dt_bias(ks[7]),
        'a_log_fwd': a_log(ks[8]),
        'a_log_bwd': a_log(ks[9]),
        'd_skip': gain(ks[10], (DEPTH, N_HEADS)),
        'ssm_norm': gain(ks[11], (DEPTH, D_INNER)),
        'w_fourier': nrm(ks[12], (DEPTH, D_F, D_MODEL), D_F ** -0.5),
        'w_ssm_out': nrm(ks[13], (DEPTH, D_INNER, D_MODEL), D_INNER ** -0.5),
        'w_out': nrm(ks[14], (DEPTH, D_MODEL, D_MODEL), D_MODEL ** -0.5),
        'norm_ffn': gain(ks[15], (DEPTH, D_MODEL)),
        'w_router': nrm(ks[16], (DEPTH, D_MODEL, N_EXPERTS), D_MODEL ** -0.5),
        'b_router': nrm(ks[17], (DEPTH, N_EXPERTS), 0.01),
        'w_gate_up': nrm(ks[18], (DEPTH, N_EXPERTS, D_MODEL, 2 * D_FF), D_MODEL ** -0.5),
        'b_gate_up': nrm(ks[19], (DEPTH, N_EXPERTS, 2 * D_FF), 0.02),
        'w_down': nrm(ks[20], (DEPTH, N_EXPERTS, D_FF, D_MODEL), D_FF ** -0.5),
        'b_down': nrm(ks[21], (DEPTH, N_EXPERTS, D_MODEL), 0.02),
        'norm_final': gain(ks[22], (D_MODEL,)),
    }


def reference(x_prompt, x_sample, norm_mix, w_in, conv_w, conv_b, dt_bias_fwd, dt_bias_bwd,
              a_log_fwd, a_log_bwd, d_skip, ssm_norm, w_fourier, w_ssm_out, w_out, norm_ffn,
              w_router, b_router, w_gate_up, b_gate_up, w_down, b_down, norm_final):
    yp = x_prompt
    ys = x_sample
    for layer in range(DEPTH):
        lp = (norm_mix[layer], w_in[layer], conv_w[layer], conv_b[layer], dt_bias_fwd[layer],
              dt_bias_bwd[layer], a_log_fwd[layer], a_log_bwd[layer], d_skip[layer], ssm_norm[layer],
              w_fourier[layer], w_ssm_out[layer], w_out[layer], norm_ffn[layer], w_router[layer],
              b_router[layer], w_gate_up[layer], b_gate_up[layer], w_down[layer], b_down[layer])
        yp = _layer(yp, *lp)
        ys = _layer(ys, *lp)
    y_prompt = rmsnorm(yp, norm_final)
    y_sample = rmsnorm(ys, norm_final)
    return (y_prompt, y_sample)
```

```python
import functools
import math

import jax
import jax.numpy as jnp
import numpy as np
from jax import lax
from jax.experimental import pallas as pl
from jax.experimental.pallas import tpu as pltpu

F32 = jnp.float32
BF16 = jnp.bfloat16

D_MODEL = 1024
D_F = 1024
FGROUP = 256
D_INNER = 2048
HEADDIM = 64
N_HEADS = 32
N_GROUPS = 8
HEADS_PER_GROUP = N_HEADS // N_GROUPS
D_STATE = 128
D_CONV = 5
CHUNK = 128
CONV_DIM = D_INNER + 2 * N_GROUPS * D_STATE
N_EXPERTS = 32
TOP_K = 4
D_FF = 1024
SWIGLU_ALPHA = 1.702
SWIGLU_LIMIT = 7.0
EPS = 1e-5
NEG_BIG = -1e30

LANES = 128
HALO = 16
VMEM_LIMIT = 56 * 1024 * 1024

COL_Z = 0
COL_XBC = D_INNER
COL_GATE = COL_XBC + CONV_DIM
COL_F = COL_GATE + 2 * D_MODEL
PROJ_MAIN = COL_F + D_F


def _cparams(sem):
    return pltpu.CompilerParams(dimension_semantics=sem, vmem_limit_bytes=VMEM_LIMIT)


def _inproj_kernel(x_ref, nw_ref, w_ref, wdtf_ref, wdtb_ref, proj_ref, dtf_ref, dtb_ref, hn_ref):
    @pl.when(pl.program_id(1) == 0)
    def _():
        x = x_ref[...]
        ms = jnp.mean(x * x, axis=-1, keepdims=True)
        hn = (x * lax.rsqrt(ms + EPS) * nw_ref[...]).astype(BF16)
        hn_ref[...] = hn
        dtf_ref[...] = jnp.dot(hn, wdtf_ref[...], preferred_element_type=F32)
        dtb_ref[...] = jnp.dot(hn, wdtb_ref[...], preferred_element_type=F32)

    proj_ref[...] = jnp.dot(hn_ref[...], w_ref[...], preferred_element_type=F32).astype(BF16)


def _inproj(x, norm_w, w_main, w_dtf, w_dtb, tm=1024, tn=1024):
    t = x.shape[0]
    tm = min(tm, t)
    grid = (t // tm, PROJ_MAIN // tn)
    return pl.pallas_call(
        _inproj_kernel,
        grid=grid,
        in_specs=[
            pl.BlockSpec((tm, D_MODEL), lambda i, j: (i, 0)),
            pl.BlockSpec((1, D_MODEL), lambda i, j: (0, 0)),
            pl.BlockSpec((D_MODEL, tn), lambda i, j: (0, j)),
            pl.BlockSpec((D_MODEL, LANES), lambda i, j: (0, 0)),
            pl.BlockSpec((D_MODEL, LANES), lambda i, j: (0, 0)),
        ],
        out_specs=[
            pl.BlockSpec((tm, tn), lambda i, j: (i, j)),
            pl.BlockSpec((tm, LANES), lambda i, j: (i, 0)),
            pl.BlockSpec((tm, LANES), lambda i, j: (i, 0)),
        ],
        out_shape=[
            jax.ShapeDtypeStruct((t, PROJ_MAIN), BF16),
            jax.ShapeDtypeStruct((t, LANES), F32),
            jax.ShapeDtypeStruct((t, LANES), F32),
        ],
        scratch_shapes=[pltpu.VMEM((tm, D_MODEL), BF16)],
        compiler_params=_cparams(("parallel", "arbitrary")),
        name="inproj",
    )(x, norm_w, w_main, w_dtf, w_dtb)


def _dft_factors(seq):
    n1 = 1 << ((int(math.log2(seq)) + 1) // 2)
    return n1, seq // n1


def _dft_tables(seq):
    n1, n2 = _dft_factors(seq)
    two_pi = 2.0 * math.pi
    c = jnp.arange(FGROUP, dtype=jnp.int32)
    ph = ((c[:, None] * c[None, :]) % FGROUP).astype(F32) * (two_pi / FGROUP)
    cs_chan = jnp.concatenate([jnp.cos(ph), -jnp.sin(ph)], axis=1).astype(BF16)
    k1 = jnp.arange(n1, dtype=jnp.int32)
    nn = (n2 * jnp.arange(n1, dtype=jnp.int32)[None, :] + jnp.arange(n2, dtype=jnp.int32)[:, None])
    al = ((k1[None, :, None] * nn[:, None, :]) % seq).astype(F32) * (two_pi / seq)
    ca, sa = jnp.cos(al), jnp.sin(al)
    g1 = jnp.concatenate([jnp.concatenate([ca, sa], axis=2),
                          jnp.concatenate([-sa, ca], axis=2)], axis=1).astype(BF16)
    k2 = jnp.arange(n2, dtype=jnp.int32)
    be = ((k2[:, None] * k2[None, :]) % n2).astype(F32) * (two_pi / n2)
    g2 = jnp.concatenate([jnp.cos(be), jnp.sin(be)], axis=1).astype(BF16)
    return cs_chan, g1, g2


def _dft1_kernel(x_ref, cs_ref, g_ref, o_ref, *, tn2, n1):
    for j in range(tn2):
        x = x_ref[0, j]
        parts = []
        for q in range(D_F // FGROUP):
            uv = jnp.dot(x[:, q * FGROUP:(q + 1) * FGROUP], cs_ref[...],
                         preferred_element_type=F32).astype(BF16)
            parts.append(jnp.concatenate([uv[:, :FGROUP], uv[:, FGROUP:]], axis=0))
        z = jnp.concatenate(parts, axis=1)
        o_ref[0, j] = jnp.dot(g_ref[j], z, preferred_element_type=F32).astype(BF16)


def _dft2_kernel(a_ref, g_ref, o_ref, *, tk1, scale):
    for j in range(tk1):
        o_ref[0, j] = (jnp.dot(g_ref[...], a_ref[0, j], preferred_element_type=F32) * scale).astype(BF16)


def _fourier_mix(f_in, tables):
    bsz, seq, _ = f_in.shape
    n1, n2 = _dft_factors(seq)
    cs_chan, g1, g2 = tables
    xt = f_in.reshape(bsz, n1, n2, D_F).transpose(0, 2, 1, 3)
    tn2 = 4
    stage1 = pl.pallas_call(
        functools.partial(_dft1_kernel, tn2=tn2, n1=n1),
        grid=(bsz, n2 // tn2),
        in_specs=[
            pl.BlockSpec((1, tn2, n1, D_F), lambda b, i: (b, i, 0, 0)),
            pl.BlockSpec((FGROUP, 2 * FGROUP), lambda b, i: (0, 0)),
            pl.BlockSpec((tn2, 2 * n1, 2 * n1), lambda b, i: (i, 0, 0)),
        ],
        out_specs=pl.BlockSpec((1, tn2, 2 * n1, D_F), lambda b, i: (b, i, 0, 0)),
        out_shape=jax.ShapeDtypeStruct((bsz, n2, 2 * n1, D_F), BF16),
        compiler_params=_cparams(("parallel", "parallel")),
        name="dft_stage1",
    )(xt, cs_chan, g1)
    a2 = (stage1.reshape(bsz, n2, 2, n1, D_F).transpose(0, 3, 2, 1, 4)
          .reshape(bsz, n1, 2 * n2, D_F))
    tk1 = 4
    scale = 1.0 / math.sqrt(seq * FGROUP)
    stage2 = pl.pallas_call(
        functools.partial(_dft2_kernel, tk1=tk1, scale=scale),
        grid=(bsz, n1 // tk1),
        in_specs=[
            pl.BlockSpec((1, tk1, 2 * n2, D_F), lambda b, i: (b, i, 0, 0)),
            pl.BlockSpec((n2, 2 * n2), lambda b, i: (0, 0)),
        ],
        out_specs=pl.BlockSpec((1, tk1, n2, D_F), lambda b, i: (b, i, 0, 0)),
        out_shape=jax.ShapeDtypeStruct((bsz, n1, n2, D_F), BF16),
        compiler_params=_cparams(("parallel", "parallel")),
        name="dft_stage2",
    )(a2, g2)
    return stage2.transpose(0, 2, 1, 3).reshape(bsz, seq, D_F)


def _conv_kernel(prev_ref, main_ref, next_ref, w_ref, b_ref, o_ref, buf_ref, *, tl, tiles_per_seq):
    i = pl.program_id(0) % tiles_per_seq
    prev = prev_ref[...].astype(F32)
    nxt = next_ref[...].astype(F32)
    buf_ref[0:HALO, :] = jnp.where(i == 0, 0.0, prev)
    buf_ref[HALO:HALO + tl, :] = main_ref[...].astype(F32)
    buf_ref[HALO + tl:, :] = jnp.where(i == tiles_per_seq - 1, 0.0, nxt)
    pad = D_CONV // 2
    acc = jnp.zeros(o_ref.shape, F32) + b_ref[...]
    for k in range(D_CONV):
        acc = acc + w_ref[k:k + 1, :] * buf_ref[HALO + k - pad:HALO + k - pad + tl, :]
    o_ref[...] = (acc * jax.nn.sigmoid(acc)).astype(BF16)


def _conv_silu(proj, conv_w, conv_b, seq, tl=512, tc=1024):
    t = proj.shape[0]
    tl = min(tl, seq)
    tiles_per_seq = seq // tl
    cb0 = COL_XBC // tc
    hb = tl // HALO
    last_hb = t // HALO - 1
    return pl.pallas_call(
        functools.partial(_conv_kernel, tl=tl, tiles_per_seq=tiles_per_seq),
        grid=(t // tl, CONV_DIM // tc),
        in_specs=[
            pl.BlockSpec((HALO, tc), lambda i, c: (jnp.maximum(i * hb - 1, 0), cb0 + c)),
            pl.BlockSpec((tl, tc), lambda i, c: (i, cb0 + c)),
            pl.BlockSpec((HALO, tc), lambda i, c: (jnp.minimum((i + 1) * hb, last_hb), cb0 + c)),
            pl.BlockSpec((D_CONV, tc), lambda i, c: (0, c)),
            pl.BlockSpec((1, tc), lambda i, c: (0, c)),
        ],
        out_specs=pl.BlockSpec((tl, tc), lambda i, c: (i, c)),
        out_shape=jax.ShapeDtypeStruct((t, CONV_DIM), BF16),
        scratch_shapes=[pltpu.VMEM((tl + 2 * HALO, tc), F32)],
        compiler_params=_cparams(("parallel", "parallel")),
        name="conv_silu",
    )(proj, proj, proj, conv_w, conv_b)


def _ssd_kernel(*refs, reverse, add_skip):
    if add_skip:
        (xs_ref, b_ref, c_ref, dt_ref, dtb_ref, alog_ref, exp_ref, yprev_ref, dskip_ref,
         o_ref, state_ref) = refs
    else:
        (xs_ref, b_ref, c_ref, dt_ref, dtb_ref, alog_ref, exp_ref, o_ref, state_ref) = refs

    @pl.when(pl.program_id(1) == 0)
    def _():
        state_ref[...] = jnp.zeros(state_ref.shape, F32)

    row = lax.broadcasted_iota(jnp.int32, (CHUNK, CHUNK), 0)
    col = lax.broadcasted_iota(jnp.int32, (CHUNK, CHUNK), 1)
    keep = (row <= col) if reverse else (row >= col)
    tri = keep.astype(F32)

    dt = jax.nn.softplus(dt_ref[...] + dtb_ref[...])
    a_neg = -jnp.exp(alog_ref[...])
    a = dt * a_neg
    cs = jnp.dot(tri, a, preferred_element_type=F32, precision=lax.Precision.HIGHEST)
    total = cs[0:1, :] if reverse else cs[CHUNK - 1:CHUNK, :]
    cs_t = cs.T
    dt_t = dt.T
    w_state = dt * jnp.exp(total - cs)
    e_cs = jnp.exp(cs)
    expand = exp_ref[...]
    w_e = jnp.dot(w_state.astype(BF16), expand, preferred_element_type=F32)
    ecs_e = jnp.dot(e_cs.astype(BF16), expand, preferred_element_type=F32)
    dec_e = jnp.dot(jnp.broadcast_to(jnp.exp(total), (8, LANES)), expand.astype(F32),
                    preferred_element_type=F32, precision=lax.Precision.HIGHEST)[0:1, :]

    xs = xs_ref[...]
    xd = (xs.astype(F32) * w_e).astype(BF16)
    lane = lax.broadcasted_iota(jnp.int32, (CHUNK, LANES), 1)
    lo = lane < HEADDIM
    gw = HEADS_PER_GROUP * HEADDIM
    for g in range(N_GROUPS):
        bg = b_ref[:, g * D_STATE:(g + 1) * D_STATE]
        cg = c_ref[:, g * D_STATE:(g + 1) * D_STATE]
        cb = lax.dot_general(cg, bg, (((1,), (1,)), ((), ())), preferred_element_type=F32)
        st_new = lax.dot_general(bg, xd[:, g * gw:(g + 1) * gw], (((0,), (0,)), ((), ())),
                                 preferred_element_type=F32)
        prev = state_ref[g]
        y_off = jnp.dot(cg, prev.astype(BF16), preferred_element_type=F32) * ecs_e[:, g * gw:(g + 1) * gw]
        state_ref[g] = prev * dec_e[:, g * gw:(g + 1) * gw] + st_new
        for q in range(HEADS_PER_GROUP // 2):
            ms = []
            for hh in range(2):
                h = g * HEADS_PER_GROUP + 2 * q + hh
                seg = cs[:, h:h + 1] - cs_t[h:h + 1, :]
                m = jnp.exp(jnp.where(keep, seg, NEG_BIG)) * cb * dt_t[h:h + 1, :]
                ms.append(m.astype(BF16))
            lhs = jnp.concatenate(ms, axis=1)
            c0 = g * gw + 2 * q * HEADDIM
            xp = xs[:, c0:c0 + LANES]
            zero = jnp.zeros_like(xp)
            rhs = jnp.concatenate([jnp.where(lo, xp, zero), jnp.where(lo, zero, xp)], axis=0)
            y = jnp.dot(lhs, rhs, preferred_element_type=F32) + y_off[:, 2 * q * HEADDIM:2 * q * HEADDIM + LANES]
            if add_skip:
                y = y + yprev_ref[:, c0:c0 + LANES].astype(F32) + dskip_ref[:, c0:c0 + LANES] * xp.astype(F32)
            o_ref[:, c0:c0 + LANES] = y.astype(BF16)


def _ssd_pass(xbc, dt_raw, dt_bias, a_log, expand, seq, reverse, y_prev=None, d_skip=None):
    t = xbc.shape[0]
    nc = seq // CHUNK
    bsz = t // seq
    add_skip = y_prev is not None

    def rblk(b, c):
        return b * nc + (nc - 1 - c if reverse else c)

    in_specs = [
        pl.BlockSpec((CHUNK, D_INNER), lambda b, c: (rblk(b, c), 0)),
        pl.BlockSpec((CHUNK, N_GROUPS * D_STATE), lambda b, c: (rblk(b, c), 2)),
        pl.BlockSpec((CHUNK, N_GROUPS * D_STATE), lambda b, c: (rblk(b, c), 3)),
        pl.BlockSpec((CHUNK, LANES), lambda b, c: (rblk(b, c), 0)),
        pl.BlockSpec((1, LANES), lambda b, c: (0, 0)),
        pl.BlockSpec((1, LANES), lambda b, c: (0, 0)),
        pl.BlockSpec((LANES, D_INNER), lambda b, c: (0, 0)),
    ]
    args = [xbc, xbc, xbc, dt_raw, dt_bias, a_log, expand]
    if add_skip:
        in_specs += [
            pl.BlockSpec((CHUNK, D_INNER), lambda b, c: (rblk(b, c), 0)),
            pl.BlockSpec((1, D_INNER), lambda b, c: (0, 0)),
        ]
        args += [y_prev, d_skip]
    return pl.pallas_call(
        functools.partial(_ssd_kernel, reverse=reverse, add_skip=add_skip),
        grid=(bsz, nc),
        in_specs=in_specs,
        out_specs=pl.BlockSpec((CHUNK, D_INNER), lambda b, c: (rblk(b, c), 0)),
        out_shape=jax.ShapeDtypeStruct((t, D_INNER), BF16),
        scratch_shapes=[pltpu.VMEM((N_GROUPS, D_STATE, HEADS_PER_GROUP * HEADDIM), F32)],
        compiler_params=_cparams(("parallel", "arbitrary")),
        name="ssd_bwd" if reverse else "ssd_fwd",
    )(*args)


def _merge_kernel(y_ref, z_ref, gate_ref, fm_ref, x_ref, snw_ref, wf_ref, ws_ref, wo_ref, fnw_ref,
                  wr_ref, br_ref, x1_ref, hn_ref, gates_ref):
    z = z_ref[...].astype(F32)
    yg = y_ref[...].astype(F32) * (z * jax.nn.sigmoid(z))
    ms = jnp.mean(yg * yg, axis=-1, keepdims=True)
    yn = (yg * lax.rsqrt(ms + EPS) * snw_ref[...]).astype(BF16)
    u_s = jnp.dot(yn, ws_ref[...], preferred_element_type=F32)
    u_f = jnp.dot(fm_ref[...], wf_ref[...], preferred_element_type=F32)
    gl = gate_ref[...].astype(F32)
    merged = (jax.nn.sigmoid(gl[:, :D_MODEL]) * u_f + jax.nn.sigmoid(gl[:, D_MODEL:]) * u_s).astype(BF16)
    x1 = x_ref[...] + jnp.dot(merged, wo_ref[...], preferred_element_type=F32)
    x1_ref[...] = x1
    ms1 = jnp.mean(x1 * x1, axis=-1, keepdims=True)
    hn = (x1 * lax.rsqrt(ms1 + EPS) * fnw_ref[...]).astype(BF16)
    hn_ref[...] = hn
    logits = jnp.dot(hn, wr_ref[...], preferred_element_type=F32) + br_ref[...]
    lane = lax.broadcasted_iota(jnp.int32, logits.shape, 1).astype(F32)
    work = logits
    picked = jnp.zeros(logits.shape, F32)
    top = None
    for k in range(TOP_K):
        m = jnp.max(work, axis=-1, keepdims=True)
        if k == 0:
            top = m
        first = jnp.min(jnp.where(work == m, lane, float(LANES)), axis=-1, keepdims=True)
        sel = lane == first
        picked = jnp.where(sel, 1.0, picked)
        work = jnp.where(sel, NEG_BIG * 2, work)
    p = picked * jnp.exp(jnp.minimum(logits - top, 0.0))
    gates_ref[...] = p / jnp.sum(p, axis=-1, keepdims=True)


def _merge_route(y, proj, f_mix, x, ssm_norm, w_fourier, w_ssm_out, w_out, norm_ffn, w_router, b_router,
                 tm=512):
    t = x.shape[0]
    tm = min(tm, t)
    full = lambda r, c: pl.BlockSpec((r, c), lambda i: (0, 0))
    return pl.pallas_call(
        _merge_kernel,
        grid=(t // tm,),
        in_specs=[
            pl.BlockSpec((tm, D_INNER), lambda i: (i, 0)),
            pl.BlockSpec((tm, D_INNER), lambda i: (i, COL_Z // D_INNER)),
            pl.BlockSpec((tm, 2 * D_MODEL), lambda i: (i, COL_GATE // (2 * D_MODEL))),
            pl.BlockSpec((tm, D_F), lambda i: (i, 0)),
            pl.BlockSpec((tm, D_MODEL), lambda i: (i, 0)),
            full(1, D_INNER), full(D_F, D_MODEL), full(D_INNER, D_MODEL), full(D_MODEL, D_MODEL),
            full(1, D_MODEL), full(D_MODEL, LANES), full(1, LANES),
        ],
        out_specs=[
            pl.BlockSpec((tm, D_MODEL), lambda i: (i, 0)),
            pl.BlockSpec((tm, D_MODEL), lambda i: (i, 0)),
            pl.BlockSpec((tm, LANES), lambda i: (i, 0)),
        ],
        out_shape=[
            jax.ShapeDtypeStruct((t, D_MODEL), F32),
            jax.ShapeDtypeStruct((t, D_MODEL), BF16),
            jax.ShapeDtypeStruct((t, LANES), F32),
        ],
        compiler_params=_cparams(("parallel",)),
        name="merge_route",
    )(y, proj, proj, f_mix, x, ssm_norm, w_fourier, w_ssm_out, w_out, norm_ffn, w_router, b_router)


def _moe_kernel(hn_ref, gates_ref, x1_ref, w1_ref, b1_ref, w2_ref, b2_ref, nf_ref, o_ref, acc_ref):
    e = pl.program_id(1)

    @pl.when(e == 0)
    def _():
        acc_ref[...] = jnp.zeros(acc_ref.shape, F32)

    hu = jnp.dot(hn_ref[...], w1_ref[0], preferred_element_type=F32) + b1_ref[0]
    glu = jnp.minimum(hu[:, :D_FF], SWIGLU_LIMIT)
    lin = jnp.clip(hu[:, D_FF:], -SWIGLU_LIMIT, SWIGLU_LIMIT)
    act = (glu * jax.nn.sigmoid(SWIGLU_ALPHA * glu) * (lin + 1.0)).astype(BF16)
    out = jnp.dot(act, w2_ref[0], preferred_element_type=F32) + b2_ref[0]
    gates = gates_ref[...]
    lane = lax.broadcasted_iota(jnp.int32, gates.shape, 1)
    g = jnp.sum(jnp.where(lane == e, gates, 0.0), axis=-1, keepdims=True)
    acc_ref[...] += g * out

    @pl.when(e == N_EXPERTS - 1)
    def _():
        x2 = x1_ref[...] + acc_ref[...]
        ms = jnp.mean(x2 * x2, axis=-1, keepdims=True)
        o_ref[...] = x2 * lax.rsqrt(ms + EPS) * nf_ref[...]


def _moe_dense(hn, gates, x1, w1, b1, w2, b2, norm_final, tm=512):
    t = hn.shape[0]
    tm = min(tm, t)
    return pl.pallas_call(
        _moe_kernel,
        grid=(t // tm, N_EXPERTS),
        in_specs=[
            pl.BlockSpec((tm, D_MODEL), lambda i, e: (i, 0)),
            pl.BlockSpec((tm, LANES), lambda i, e: (i, 0)),
            pl.BlockSpec((tm, D_MODEL), lambda i, e: (i, 0)),
            pl.BlockSpec((1, D_MODEL, 2 * D_FF), lambda i, e: (e, 0, 0)),
            pl.BlockSpec((1, 1, 2 * D_FF), lambda i, e: (e, 0, 0)),
            pl.BlockSpec((1, D_FF, D_MODEL), lambda i, e: (e, 0, 0)),
            pl.BlockSpec((1, 1, D_MODEL), lambda i, e: (e, 0, 0)),
            pl.BlockSpec((1, D_MODEL), lambda i, e: (0, 0)),
        ],
        out_specs=pl.BlockSpec((tm, D_MODEL), lambda i, e: (i, 0)),
        out_shape=jax.ShapeDtypeStruct((t, D_MODEL), F32),
        scratch_shapes=[pltpu.VMEM((tm, D_MODEL), F32)],
        compiler_params=_cparams(("parallel", "arbitrary")),
        name="moe_dense",
    )(hn, gates, x1, w1, b1, w2, b2, norm_final)


def _pad_lanes(v, fill=0.0):
    v = v.reshape(1, -1).astype(F32)
    return jnp.pad(v, ((0, 0), (0, LANES - v.shape[1])), constant_values=fill)


def _stream(x3, p):
    bsz, seq, _ = x3.shape
    x = x3.reshape(bsz * seq, D_MODEL)
    proj, dt_f, dt_b = _inproj(x, p["norm_mix"], p["w_main"], p["w_dtf"], p["w_dtb"])
    f_in = proj[:, COL_F:].reshape(bsz, seq, D_F)
    f_mix = _fourier_mix(f_in, _dft_tables(seq)).reshape(bsz * seq, D_F)
    xbc = _conv_silu(proj, p["conv_w"], p["conv_b"], seq)
    y_f = _ssd_pass(xbc, dt_f, p["dtb_f"], p["alog_f"], p["expand"], seq, reverse=False)
    y = _ssd_pass(xbc, dt_b, p["dtb_b"], p["alog_b"], p["expand"], seq, reverse=True,
                  y_prev=y_f, d_skip=p["d_skip"])
    x1, hn, gates = _merge_route(y, proj, f_mix, x, p["ssm_norm"], p["w_fourier"], p["w_ssm_out"],
                                 p["w_out"], p["norm_ffn"], p["w_router"], p["b_router"])
    out = _moe_dense(hn, gates, x1, p["w1"], p["b1"], p["w2"], p["b2"], p["norm_final"])
    return out.reshape(bsz, seq, D_MODEL)


def kernel(x_prompt, x_sample, norm_mix, w_in, conv_w, conv_b, dt_bias_fwd, dt_bias_bwd, a_log_fwd, a_log_bwd, d_skip, ssm_norm, w_fourier, w_ssm_out, w_out, norm_ffn, w_router, b_router, w_gate_up, b_gate_up, w_down, b_down, norm_final):
    assert norm_mix.shape[0] == 1, "single-layer block"
    w = w_in[0]
    o_z, o_xbc, o_dt, o_gate = D_F, D_F + D_INNER, D_F + D_INNER + CONV_DIM, D_F + D_INNER + CONV_DIM + 2 * N_HEADS
    w_main = jnp.concatenate([w[:, o_z:o_xbc], w[:, o_xbc:o_dt], w[:, o_gate:], w[:, :D_F]], axis=1).astype(BF16)
    pad_dt = lambda m: jnp.pad(m, ((0, 0), (0, LANES - N_HEADS))).astype(BF16)
    head_of_chan = jnp.arange(D_INNER, dtype=jnp.int32) // HEADDIM
    expand = (jnp.arange(LANES, dtype=jnp.int32)[:, None] == head_of_chan[None, :]).astype(BF16)
    p = dict(
        norm_mix=norm_mix[0].reshape(1, D_MODEL),
        w_main=w_main,
        w_dtf=pad_dt(w[:, o_dt:o_dt + N_HEADS]),
        w_dtb=pad_dt(w[:, o_dt + N_HEADS:o_gate]),
        conv_w=conv_w[0], conv_b=conv_b[0].reshape(1, CONV_DIM),
        dtb_f=_pad_lanes(dt_bias_fwd[0]), dtb_b=_pad_lanes(dt_bias_bwd[0]),
        alog_f=_pad_lanes(a_log_fwd[0], NEG_BIG), alog_b=_pad_lanes(a_log_bwd[0], NEG_BIG),
        expand=expand,
        d_skip=jnp.repeat(d_skip[0].astype(F32), HEADDIM).reshape(1, D_INNER),
        ssm_norm=ssm_norm[0].reshape(1, D_INNER),
        w_fourier=w_fourier[0].astype(BF16), w_ssm_out=w_ssm_out[0].astype(BF16), w_out=w_out[0].astype(BF16),
        norm_ffn=norm_ffn[0].reshape(1, D_MODEL),
        w_router=jnp.pad(w_router[0], ((0, 0), (0, LANES - N_EXPERTS))).astype(BF16),
        b_router=_pad_lanes(b_router[0], NEG_BIG),
        w1=w_gate_up[0].astype(BF16), b1=b_gate_up[0].reshape(N_EXPERTS, 1, 2 * D_FF),
        w2=w_down[0].astype(BF16), b2=b_down[0].reshape(N_EXPERTS, 1, D_MODEL),
        norm_final=norm_final.reshape(1, D_MODEL),
    )
    return (_stream(x_prompt, p), _stream(x_sample, p))
```

```python
import functools
import math

import jax
import jax.numpy as jnp
import numpy as np
from jax import lax
from jax.experimental import pallas as pl
from jax.experimental.pallas import tpu as pltpu

F32 = jnp.float32
BF16 = jnp.bfloat16

D_MODEL = 1024
D_F = 1024
FGROUP = 256
D_INNER = 2048
HEADDIM = 64
N_HEADS = 32
N_GROUPS = 8
HEADS_PER_GROUP = N_HEADS // N_GROUPS
D_STATE = 128
D_CONV = 5
CHUNK = 128
CONV_DIM = D_INNER + 2 * N_GROUPS * D_STATE
N_EXPERTS = 32
TOP_K = 4
D_FF = 1024
SWIGLU_ALPHA = 1.702
SWIGLU_LIMIT = 7.0
EPS = 1e-5
NEG_BIG = -1e30

LANES = 128
HALO = 16
VMEM_LIMIT = 56 * 1024 * 1024

COL_Z = 0
COL_XBC = D_INNER
COL_GATE = COL_XBC + CONV_DIM
COL_F = COL_GATE + 2 * D_MODEL
PROJ_MAIN = COL_F + D_F


def _cparams(sem):
    return pltpu.CompilerParams(dimension_semantics=sem, vmem_limit_bytes=VMEM_LIMIT)


def _inproj_kernel(x_ref, nw_ref, w_ref, wdtf_ref, wdtb_ref, proj_ref, dtf_ref, dtb_ref, hn_ref):
    @pl.when(pl.program_id(1) == 0)
    def _():
        x = x_ref[...]
        ms = jnp.mean(x * x, axis=-1, keepdims=True)
        hn = (x * lax.rsqrt(ms + EPS) * nw_ref[...]).astype(BF16)
        hn_ref[...] = hn
        dtf_ref[...] = jnp.dot(hn, wdtf_ref[...], preferred_element_type=F32)
        dtb_ref[...] = jnp.dot(hn, wdtb_ref[...], preferred_element_type=F32)

    proj_ref[...] = jnp.dot(hn_ref[...], w_ref[...], preferred_element_type=F32).astype(BF16)


def _inproj(x, norm_w, w_main, w_dtf, w_dtb, tm=1024, tn=1024):
    t = x.shape[0]
    tm = min(tm, t)
    grid = (t // tm, PROJ_MAIN // tn)
    return pl.pallas_call(
        _inproj_kernel,
        grid=grid,
        in_specs=[
            pl.BlockSpec((tm, D_MODEL), lambda i, j: (i, 0)),
            pl.BlockSpec((1, D_MODEL), lambda i, j: (0, 0)),
            pl.BlockSpec((D_MODEL, tn), lambda i, j: (0, j)),
            pl.BlockSpec((D_MODEL, LANES), lambda i, j: (0, 0)),
            pl.BlockSpec((D_MODEL, LANES), lambda i, j: (0, 0)),
        ],
        out_specs=[
            pl.BlockSpec((tm, tn), lambda i, j: (i, j)),
            pl.BlockSpec((tm, LANES), lambda i, j: (i, 0)),
            pl.BlockSpec((tm, LANES), lambda i, j: (i, 0)),
        ],
        out_shape=[
            jax.ShapeDtypeStruct((t, PROJ_MAIN), BF16),
            jax.ShapeDtypeStruct((t, LANES), F32),
            jax.ShapeDtypeStruct((t, LANES), F32),
        ],
        scratch_shapes=[pltpu.VMEM((tm, D_MODEL), BF16)],
        compiler_params=_cparams(("parallel", "arbitrary")),
        name="inproj",
    )(x, norm_w, w_main, w_dtf, w_dtb)


def _dft_factors(seq):
    n1 = 1 << ((int(math.log2(seq)) + 1) // 2)
    return n1, seq // n1


def _dft_tables(seq):
    n1, n2 = _dft_factors(seq)
    two_pi = 2.0 * math.pi
    c = jnp.arange(FGROUP, dtype=jnp.int32)
    ph = ((c[:, None] * c[None, :]) % FGROUP).astype(F32) * (two_pi / FGROUP)
    cs_chan = jnp.concatenate([jnp.cos(ph), -jnp.sin(ph)], axis=1).astype(BF16)
    k1 = jnp.arange(n1, dtype=jnp.int32)
    nn = (n2 * jnp.arange(n1, dtype=jnp.int32)[None, :] + jnp.arange(n2, dtype=jnp.int32)[:, None])
    al = ((k1[None, :, None] * nn[:, None, :]) % seq).astype(F32) * (two_pi / seq)
    ca, sa = jnp.cos(al), jnp.sin(al)
    g1 = jnp.concatenate([jnp.concatenate([ca, sa], axis=2),
                          jnp.concatenate([-sa, ca], axis=2)], axis=1).astype(BF16)
    k2 = jnp.arange(n2, dtype=jnp.int32)
    be = ((k2[:, None] * k2[None, :]) % n2).astype(F32) * (two_pi / n2)
    g2 = jnp.concatenate([jnp.cos(be), jnp.sin(be)], axis=1).astype(BF16)
    return cs_chan, g1, g2


def _dft1_kernel(x_ref, cs_ref, g_ref, o_ref, *, tn2, n1):
    for j in range(tn2):
        x = x_ref[0, j]
        parts = []
        for q in range(D_F // FGROUP):
            uv = jnp.dot(x[:, q * FGROUP:(q + 1) * FGROUP], cs_ref[...],
                         preferred_element_type=F32).astype(BF16)
            parts.append(jnp.concatenate([uv[:, :FGROUP], uv[:, FGROUP:]], axis=0))
        z = jnp.concatenate(parts, axis=1)
        o_ref[0, j] = jnp.dot(g_ref[j], z, preferred_element_type=F32).astype(BF16)


def _dft2_kernel(a_ref, g_ref, o_ref, *, tk1, scale):
    for j in range(tk1):
        o_ref[0, j] = (jnp.dot(g_ref[...], a_ref[0, j], preferred_element_type=F32) * scale).astype(BF16)


def _fourier_mix(f_in, tables):
    bsz, seq, _ = f_in.shape
    n1, n2 = _dft_factors(seq)
    cs_chan, g1, g2 = tables
    xt = f_in.reshape(bsz, n1, n2, D_F).transpose(0, 2, 1, 3)
    tn2 = 4
    stage1 = pl.pallas_call(
        functools.partial(_dft1_kernel, tn2=tn2, n1=n1),
        grid=(bsz, n2 // tn2),
        in_specs=[
            pl.BlockSpec((1, tn2, n1, D_F), lambda b, i: (b, i, 0, 0)),
            pl.BlockSpec((FGROUP, 2 * FGROUP), lambda b, i: (0, 0)),
            pl.BlockSpec((tn2, 2 * n1, 2 * n1), lambda b, i: (i, 0, 0)),
        ],
        out_specs=pl.BlockSpec((1, tn2, 2 * n1, D_F), lambda b, i: (b, i, 0, 0)),
        out_shape=jax.ShapeDtypeStruct((bsz, n2, 2 * n1, D_F), BF16),
        compiler_params=_cparams(("parallel", "parallel")),
        name="dft_stage1",
    )(xt, cs_chan, g1)
    a2 = (stage1.reshape(bsz, n2, 2, n1, D_F).transpose(0, 3, 2, 1, 4)
          .reshape(bsz, n1, 2 * n2, D_F))
    tk1 = 4
    scale = 1.0 / math.sqrt(seq * FGROUP)
    stage2 = pl.pallas_call(
        functools.partial(_dft2_kernel, tk1=tk1, scale=scale),
        grid=(bsz, n1 // tk1),
        in_specs=[
            pl.BlockSpec((1, tk1, 2 * n2, D_F), lambda b, i: (b, i, 0, 0)),
            pl.BlockSpec((n2, 2 * n2), lambda b, i: (0, 0)),
        ],
        out_specs=pl.BlockSpec((1, tk1, n2, D_F), lambda b, i: (b, i, 0, 0)),
        out_shape=jax.ShapeDtypeStruct((bsz, n1, n2, D_F), BF16),
        compiler_params=_cparams(("parallel", "parallel")),
        name="dft_stage2",
    )(a2, g2)
    return stage2.transpose(0, 2, 1, 3).reshape(bsz, seq, D_F)


def _conv_kernel(prev_ref, main_ref, next_ref, w_ref, b_ref, o_ref, buf_ref, *, tl, tiles_per_seq):
    i = pl.program_id(0) % tiles_per_seq
    prev = prev_ref[...].astype(F32)
    nxt = next_ref[...].astype(F32)
    buf_ref[0:HALO, :] = jnp.where(i == 0, 0.0, prev)
    buf_ref[HALO:HALO + tl, :] = main_ref[...].astype(F32)
    buf_ref[HALO + tl:, :] = jnp.where(i == tiles_per_seq - 1, 0.0, nxt)
    pad = D_CONV // 2
    acc = jnp.zeros(o_ref.shape, F32) + b_ref[...]
    for k in range(D_CONV):
        acc = acc + w_ref[k:k + 1, :] * buf_ref[HALO + k - pad:HALO + k - pad + tl, :]
    o_ref[...] = (acc * jax.nn.sigmoid(acc)).astype(BF16)


def _conv_silu(proj, conv_w, conv_b, seq, tl=512, tc=1024):
    t = proj.shape[0]
    tl = min(tl, seq)
    tiles_per_seq = seq // tl
    cb0 = COL_XBC // tc
    hb = tl // HALO
    last_hb = t // HALO - 1
    return pl.pallas_call(
        functools.partial(_conv_kernel, tl=tl, tiles_per_seq=tiles_per_seq),
        grid=(t // tl, CONV_DIM // tc),
        in_specs=[
            pl.BlockSpec((HALO, tc), lambda i, c: (jnp.maximum(i * hb - 1, 0), cb0 + c)),
            pl.BlockSpec((tl, tc), lambda i, c: (i, cb0 + c)),
            pl.BlockSpec((HALO, tc), lambda i, c: (jnp.minimum((i + 1) * hb, last_hb), cb0 + c)),
            pl.BlockSpec((D_CONV, tc), lambda i, c: (0, c)),
            pl.BlockSpec((1, tc), lambda i, c: (0, c)),
        ],
        out_specs=pl.BlockSpec((tl, tc), lambda i, c: (i, c)),
        out_shape=jax.ShapeDtypeStruct((t, CONV_DIM), BF16),
        scratch_shapes=[pltpu.VMEM((tl + 2 * HALO, tc), F32)],
        compiler_params=_cparams(("parallel", "parallel")),
        name="conv_silu",
    )(proj, proj, proj, conv_w, conv_b)


def _ssd_kernel(*refs, reverse, add_skip):
    if add_skip:
        (xs_ref, b_ref, c_ref, dt_ref, dtb_ref, alog_ref, exp_ref, yprev_ref, dskip_ref,
         o_ref, state_ref) = refs
    else:
        (xs_ref, b_ref, c_ref, dt_ref, dtb_ref, alog_ref, exp_ref, o_ref, state_ref) = refs

    @pl.when(pl.program_id(1) == 0)
    def _():
        state_ref[...] = jnp.zeros(state_ref.shape, F32)

    row = lax.broadcasted_iota(jnp.int32, (CHUNK, CHUNK), 0)
    col = lax.broadcasted_iota(jnp.int32, (CHUNK, CHUNK), 1)
    keep = (row <= col) if reverse else (row >= col)
    tri = keep.astype(F32)

    dt = jax.nn.softplus(dt_ref[...] + dtb_ref[...])
    a_neg = -jnp.exp(alog_ref[...])
    a = dt * a_neg
    cs = jnp.dot(tri, a, preferred_element_type=F32, precision=lax.Precision.HIGHEST)
    total = cs[0:1, :] if reverse else cs[CHUNK - 1:CHUNK, :]
    cs_t = cs.T
    dt_t = dt.T
    w_state = dt * jnp.exp(total - cs)
    e_cs = jnp.exp(cs)
    expand = exp_ref[...]
    w_e = jnp.dot(w_state.astype(BF16), expand, preferred_element_type=F32)
    ecs_e = jnp.dot(e_cs.astype(BF16), expand, preferred_element_type=F32)
    dec_e = jnp.dot(jnp.broadcast_to(jnp.exp(total), (8, LANES)), expand.astype(F32),
                    preferred_element_type=F32, precision=lax.Precision.HIGHEST)[0:1, :]

    xs = xs_ref[...]
    xd = (xs.astype(F32) * w_e).astype(BF16)
    lane = lax.broadcasted_iota(jnp.int32, (CHUNK, LANES), 1)
    lo = lane < HEADDIM
    gw = HEADS_PER_GROUP * HEADDIM
    for g in range(N_GROUPS):
        bg = b_ref[:, g * D_STATE:(g + 1) * D_STATE]
        cg = c_ref[:, g * D_STATE:(g + 1) * D_STATE]
        cb = lax.dot_general(cg, bg, (((1,), (1,)), ((), ())), preferred_element_type=F32)
        st_new = lax.dot_general(bg, xd[:, g * gw:(g + 1) * gw], (((0,), (0,)), ((), ())),
                                 preferred_element_type=F32)
        prev = state_ref[g]
        y_off = jnp.dot(cg, prev.astype(BF16), preferred_element_type=F32) * ecs_e[:, g * gw:(g + 1) * gw]
        state_ref[g] = prev * dec_e[:, g * gw:(g + 1) * gw] + st_new
        for q in range(HEADS_PER_GROUP // 2):
            ms = []
            for hh in range(2):
                h = g * HEADS_PER_GROUP + 2 * q + hh
                seg = cs[:, h:h + 1] - cs_t[h:h + 1, :]
                m = jnp.exp(jnp.where(keep, seg, NEG_BIG)) * cb * dt_t[h:h + 1, :]
                ms.append(m.astype(BF16))
            lhs = jnp.concatenate(ms, axis=1)
            c0 = g * gw + 2 * q * HEADDIM
            xp = xs[:, c0:c0 + LANES]
            zero = jnp.zeros_like(xp)
            rhs = jnp.concatenate([jnp.where(lo, xp, zero), jnp.where(lo, zero, xp)], axis=0)
            y = jnp.dot(lhs, rhs, preferred_element_type=F32) + y_off[:, 2 * q * HEADDIM:2 * q * HEADDIM + LANES]
            if add_skip:
                y = y + yprev_ref[:, c0:c0 + LANES].astype(F32) + dskip_ref[:, c0:c0 + LANES] * xp.astype(F32)
            o_ref[:, c0:c0 + LANES] = y.astype(BF16)


def _ssd_pass(xbc, dt_raw, dt_bias, a_log, expand, seq, reverse, y_prev=None, d_skip=None):
    t = xbc.shape[0]
    nc = seq // CHUNK
    bsz = t // seq
    add_skip = y_prev is not None

    def rblk(b, c):
        return b * nc + (nc - 1 - c if reverse else c)

    in_specs = [
        pl.BlockSpec((CHUNK, D_INNER), lambda b, c: (rblk(b, c), 0)),
        pl.BlockSpec((CHUNK, N_GROUPS * D_STATE), lambda b, c: (rblk(b, c), 2)),
        pl.BlockSpec((CHUNK, N_GROUPS * D_STATE), lambda b, c: (rblk(b, c), 3)),
        pl.BlockSpec((CHUNK, LANES), lambda b, c: (rblk(b, c), 0)),
        pl.BlockSpec((1, LANES), lambda b, c: (0, 0)),
        pl.BlockSpec((1, LANES), lambda b, c: (0, 0)),
        pl.BlockSpec((LANES, D_INNER), lambda b, c: (0, 0)),
    ]
    args = [xbc, xbc, xbc, dt_raw, dt_bias, a_log, expand]
    if add_skip:
        in_specs += [
            pl.BlockSpec((CHUNK, D_INNER), lambda b, c: (rblk(b, c), 0)),
            pl.BlockSpec((1, D_INNER), lambda b, c: (0, 0)),
        ]
        args += [y_prev, d_skip]
    return pl.pallas_call(
        functools.partial(_ssd_kernel, reverse=reverse, add_skip=add_skip),
        grid=(bsz, nc),
        in_specs=in_specs,
        out_specs=pl.BlockSpec((CHUNK, D_INNER), lambda b, c: (rblk(b, c), 0)),
        out_shape=jax.ShapeDtypeStruct((t, D_INNER), BF16),
        scratch_shapes=[pltpu.VMEM((N_GROUPS, D_STATE, HEADS_PER_GROUP * HEADDIM), F32)],
        compiler_params=_cparams(("parallel", "arbitrary")),
        name="ssd_bwd" if reverse else "ssd_fwd",
    )(*args)


def _merge_kernel(y_ref, z_ref, gate_ref, fm_ref, x_ref, snw_ref, wf_ref, ws_ref, wo_ref, fnw_ref,
                  wr_ref, br_ref, x1_ref, hn_ref, gates_ref):
    z = z_ref[...].astype(F32)
    yg = y_ref[...].astype(F32) * (z * jax.nn.sigmoid(z))
    ms = jnp.mean(yg * yg, axis=-1, keepdims=True)
    yn = (yg * lax.rsqrt(ms + EPS) * snw_ref[...]).astype(BF16)
    u_s = jnp.dot(yn, ws_ref[...], preferred_element_type=F32)
    u_f = jnp.dot(fm_ref[...], wf_ref[...], preferred_element_type=F32)
    gl = gate_ref[...].astype(F32)
    merged = (jax.nn.sigmoid(gl[:, :D_MODEL]) * u_f + jax.nn.sigmoid(gl[:, D_MODEL:]) * u_s).astype(BF16)
    x1 = x_ref[...] + jnp.dot(merged, wo_ref[...], preferred_element_type=F32)
    x1_ref[...] = x1
    ms1 = jnp.mean(x1 * x1, axis=-1, keepdims=True)
    hn = (x1 * lax.rsqrt(ms1 + EPS) * fnw_ref[...]).astype(BF16)
    hn_ref[...] = hn
    logits = jnp.dot(hn, wr_ref[...], preferred_element_type=F32) + br_ref[...]
    lane = lax.broadcasted_iota(jnp.int32, logits.shape, 1).astype(F32)
    work = logits
    top = None
    denom = jnp.zeros((logits.shape[0], 1), F32)
    route = jnp.zeros(logits.shape, F32)
    probs = []
    for k in range(TOP_K):
        m = jnp.max(work, axis=-1, keepdims=True)
        if k == 0:
            top = m
        first = jnp.min(jnp.where(work == m, lane, float(LANES)), axis=-1, keepdims=True)
        work = jnp.where(lane == first, NEG_BIG * 2, work)
        pk = jnp.exp(m - top)
        denom = denom + pk
        probs.append(pk)
        route = jnp.where(lane == float(k), first, route)
    inv = 1.0 / denom
    for k in range(TOP_K):
        route = jnp.where(lane == float(TOP_K + k), probs[k] * inv, route)
    gates_ref[...] = route


def _merge_route(y, proj, f_mix, x, ssm_norm, w_fourier, w_ssm_out, w_out, norm_ffn, w_router, b_router,
                 tm=512):
    t = x.shape[0]
    tm = min(tm, t)
    full = lambda r, c: pl.BlockSpec((r, c), lambda i: (0, 0))
    return pl.pallas_call(
        _merge_kernel,
        grid=(t // tm,),
        in_specs=[
            pl.BlockSpec((tm, D_INNER), lambda i: (i, 0)),
            pl.BlockSpec((tm, D_INNER), lambda i: (i, COL_Z // D_INNER)),
            pl.BlockSpec((tm, 2 * D_MODEL), lambda i: (i, COL_GATE // (2 * D_MODEL))),
            pl.BlockSpec((tm, D_F), lambda i: (i, 0)),
            pl.BlockSpec((tm, D_MODEL), lambda i: (i, 0)),
            full(1, D_INNER), full(D_F, D_MODEL), full(D_INNER, D_MODEL), full(D_MODEL, D_MODEL),
            full(1, D_MODEL), full(D_MODEL, LANES), full(1, LANES),
        ],
        out_specs=[
            pl.BlockSpec((tm, D_MODEL), lambda i: (i, 0)),
            pl.BlockSpec((tm, D_MODEL), lambda i: (i, 0)),
            pl.BlockSpec((tm, LANES), lambda i: (i, 0)),
        ],
        out_shape=[
            jax.ShapeDtypeStruct((t, D_MODEL), F32),
            jax.ShapeDtypeStruct((t, D_MODEL), BF16),
            jax.ShapeDtypeStruct((t, LANES), F32),
        ],
        compiler_params=_cparams(("parallel",)),
        name="merge_route",
    )(y, proj, proj, f_mix, x, ssm_norm, w_fourier, w_ssm_out, w_out, norm_ffn, w_router, b_router)


TOK_BLOCK = 512
PIECE = 16
FFN_TM = 512
_WORST_PIECES = (TOK_BLOCK * TOP_K + N_EXPERTS * (PIECE - 1) + PIECE - 1) // PIECE
_PIECES_PER_TILE = FFN_TM // PIECE
PIECES_PER_BLOCK = (_WORST_PIECES + _PIECES_PER_TILE - 1) // _PIECES_PER_TILE * _PIECES_PER_TILE
ROWS_PER_BLOCK = PIECES_PER_BLOCK * PIECE
ITEM_NONE, ITEM_WRITE, ITEM_MERGE, ITEM_ZERO = 0, 1, 2, 3


def _route_tables(route, t):
    i32 = jnp.int32
    nb = t // TOK_BLOCK
    idx = route[:, :TOP_K].astype(i32).reshape(nb, TOK_BLOCK * TOP_K)
    onehot = (idx[:, :, None] == jnp.arange(N_EXPERTS, dtype=i32)[None, None, :]).astype(i32)
    csum = jnp.cumsum(onehot, axis=1)
    rank = jnp.sum((csum - 1) * onehot, axis=2)
    cnt = csum[:, -1, :]
    pcs = (cnt + PIECE - 1) // PIECE
    seg_end = jnp.cumsum(pcs, axis=1)
    seg_start = seg_end - pcs
    pos = jnp.take_along_axis(seg_start, idx, axis=1) * PIECE + rank
    used = seg_end[:, -1]
    per_e = jnp.sum(pcs, axis=0)
    g_end = jnp.cumsum(per_e)
    g_start = g_end - per_e
    b_prefix = jnp.cumsum(pcs, axis=0) - pcs
    j = jnp.arange(PIECES_PER_BLOCK, dtype=i32)[None, :]
    e_of_j = jnp.minimum(jnp.sum((j[:, :, None] >= seg_end[:, None, :]).astype(i32), axis=2), N_EXPERTS - 1)
    dst_used = (jnp.take(g_start, e_of_j) + jnp.take_along_axis(b_prefix, e_of_j, axis=1)
                + j - jnp.take_along_axis(seg_start, e_of_j, axis=1))
    free = PIECES_PER_BLOCK - used
    dst_unused = g_end[-1] + (jnp.cumsum(free) - free)[:, None] + j - used[:, None]
    is_used = j < used[:, None]
    dst = jnp.where(is_used, dst_used, dst_unused).reshape(-1)
    src = jnp.where(is_used, dst_used, 0).reshape(-1)
    n_items_max = nb * ROWS_PER_BLOCK // FFN_TM + N_EXPERTS
    start_rows, end_rows = g_start * PIECE, g_end * PIECE
    t0 = start_rows // FFN_TM
    n_it = jnp.where(end_rows > start_rows, (end_rows - 1) // FFN_TM - t0 + 1, 0)
    it_end = jnp.cumsum(n_it)
    it_start = it_end - n_it
    n_exp = it_end[-1]
    n_tiles = nb * ROWS_PER_BLOCK // FFN_TM
    tiles_used = (end_rows[-1] + FFN_TM - 1) // FFN_TM
    q = jnp.arange(n_items_max, dtype=i32)
    qc = jnp.minimum(q, n_exp - 1)
    e_q = jnp.minimum(jnp.sum((qc[:, None] >= it_end[None, :]).astype(i32), axis=1), N_EXPERTS - 1)
    tile_exp = jnp.take(t0, e_q) + qc - jnp.take(it_start, e_q)
    tile_q = jnp.where(q < n_exp, tile_exp, jnp.minimum(tiles_used + q - n_exp, n_tiles - 1))
    lo = jnp.maximum(jnp.take(start_rows, e_q), tile_q * FFN_TM) - tile_q * FFN_TM
    hi = jnp.minimum(jnp.take(end_rows, e_q), (tile_q + 1) * FFN_TM) - tile_q * FFN_TM
    new_tile = jnp.concatenate([jnp.ones((1,), i32), (tile_q[1:] != tile_q[:-1]).astype(i32)])
    kind = jnp.where(q < n_exp, jnp.where(new_tile == 1, ITEM_WRITE, ITEM_MERGE),
                     jnp.where(new_tile == 1, ITEM_ZERO, ITEM_NONE))
    pos3 = pos.reshape(nb, TOK_BLOCK, TOP_K)
    return dict(pos_t=pos3.transpose(0, 2, 1), pos_c=pos3.reshape(t, TOP_K), dst=dst, src=src,
                item_e=e_q, item_tile=tile_q, item_lo=lo, item_hi=hi, item_kind=kind)


def _piece_copy_out(buf_ref, hbm_ref, sem, slot, j, dst_piece):
    return pltpu.make_async_copy(buf_ref.at[slot, pl.ds(pl.multiple_of(j * PIECE, PIECE), PIECE)],
                                 hbm_ref.at[pl.ds(pl.multiple_of(dst_piece * PIECE, PIECE), PIECE)],
                                 sem.at[slot])


def _piece_copy_in(hbm_ref, buf_ref, sem, slot, j, src_piece):
    return pltpu.make_async_copy(hbm_ref.at[pl.ds(pl.multiple_of(src_piece * PIECE, PIECE), PIECE)],
                                 buf_ref.at[slot, pl.ds(pl.multiple_of(j * PIECE, PIECE), PIECE)],
                                 sem.at[slot])


def _dispatch_kernel(dst_ref, hn_ref, post_ref, xs_hbm, buf_ref, sem, *, nb):
    b = pl.program_id(0)
    slot = b % 2

    def wait_slot(s):
        def body(j, c):
            _piece_copy_out(buf_ref, xs_hbm, sem, s, j, 0).wait()
            return c
        lax.fori_loop(0, PIECES_PER_BLOCK, body, 0)

    @pl.when(b >= 2)
    def _():
        wait_slot(slot)

    r = lax.broadcasted_iota(jnp.int32, (ROWS_PER_BLOCK, TOK_BLOCK), 0)
    onehot = jnp.zeros((ROWS_PER_BLOCK, TOK_BLOCK), F32)
    for k in range(TOP_K):
        onehot = jnp.where(r == post_ref[0, k:k + 1, :], 1.0, onehot)
    buf_ref[slot] = jnp.dot(onehot.astype(BF16), hn_ref[...], preferred_element_type=F32).astype(BF16)

    def start(j, c):
        _piece_copy_out(buf_ref, xs_hbm, sem, slot, j, dst_ref[b * PIECES_PER_BLOCK + j]).start()
        return c
    lax.fori_loop(0, PIECES_PER_BLOCK, start, 0)

    @pl.when(b == nb - 1)
    def _():
        wait_slot(slot)
        if nb >= 2:
            wait_slot(1 - slot)


def _ffn_kernel(e_ref, tile_ref, lo_ref, hi_ref, kind_ref, x_ref, w1_ref, b1_ref, w2_ref, b2_ref, o_ref):
    q = pl.program_id(0)
    lo, hi, kind = lo_ref[q], hi_ref[q], kind_ref[q]

    def ffn():
        hu = jnp.dot(x_ref[...], w1_ref[0], preferred_element_type=F32) + b1_ref[0]
        glu = jnp.minimum(hu[:, :D_FF], SWIGLU_LIMIT)
        lin = jnp.clip(hu[:, D_FF:], -SWIGLU_LIMIT, SWIGLU_LIMIT)
        act = (glu * jax.nn.sigmoid(SWIGLU_ALPHA * glu) * (lin + 1.0)).astype(BF16)
        return (jnp.dot(act, w2_ref[0], preferred_element_type=F32) + b2_ref[0]).astype(BF16)

    @pl.when(kind == ITEM_WRITE)
    def _():
        o_ref[...] = ffn()

    @pl.when(kind == ITEM_MERGE)
    def _():
        row = lax.broadcasted_iota(jnp.int32, (FFN_TM, 1), 0)
        mine = jnp.logical_and(row >= lo, row < hi)
        o_ref[...] = jnp.where(mine, ffn(), o_ref[...])

    @pl.when(kind == ITEM_ZERO)
    def _():
        o_ref[...] = jnp.zeros(o_ref.shape, BF16)


def _combine_kernel(src_ref, pos_ref, route_ref, x1_ref, nf_ref, os_hbm, o_ref, buf_ref, sem, *, nb):
    b = pl.program_id(0)
    slot = b % 2

    def fetch(bb, s):
        def body(j, c):
            _piece_copy_in(os_hbm, buf_ref, sem, s, j, src_ref[bb * PIECES_PER_BLOCK + j]).start()
            return c
        lax.fori_loop(0, PIECES_PER_BLOCK, body, 0)

    @pl.when(b == 0)
    def _():
        fetch(0, 0)

    @pl.when(b + 1 < nb)
    def _():
        fetch(b + 1, 1 - slot)

    def wait(j, c):
        _piece_copy_in(os_hbm, buf_ref, sem, slot, j, 0).wait()
        return c
    lax.fori_loop(0, PIECES_PER_BLOCK, wait, 0)

    r = lax.broadcasted_iota(jnp.int32, (TOK_BLOCK, ROWS_PER_BLOCK), 1)
    route = route_ref[...]
    wmat = jnp.zeros((TOK_BLOCK, ROWS_PER_BLOCK), F32)
    for k in range(TOP_K):
        wmat = jnp.where(r == pos_ref[:, k:k + 1], route[:, TOP_K + k:TOP_K + k + 1], wmat)
    y = jnp.dot(wmat.astype(BF16), buf_ref[slot], preferred_element_type=F32)
    x2 = x1_ref[...] + y
    ms = jnp.mean(x2 * x2, axis=-1, keepdims=True)
    o_ref[...] = x2 * lax.rsqrt(ms + EPS) * nf_ref[...]


def _moe_routed(hn, route, x1, w1, b1, w2, b2, norm_final):
    t = hn.shape[0]
    assert t % TOK_BLOCK == 0 and ROWS_PER_BLOCK % FFN_TM == 0
    nb = t // TOK_BLOCK
    rows = nb * ROWS_PER_BLOCK
    tb = _route_tables(route, t)
    sorted_x = pl.pallas_call(
        functools.partial(_dispatch_kernel, nb=nb),
        grid_spec=pltpu.PrefetchScalarGridSpec(
            num_scalar_prefetch=1,
            grid=(nb,),
            in_specs=[
                pl.BlockSpec((TOK_BLOCK, D_MODEL), lambda b, d: (b, 0)),
                pl.BlockSpec((1, TOP_K, TOK_BLOCK), lambda b, d: (b, 0, 0)),
            ],
            out_specs=pl.BlockSpec(memory_space=pl.ANY),
            scratch_shapes=[pltpu.VMEM((2, ROWS_PER_BLOCK, D_MODEL), BF16), pltpu.SemaphoreType.DMA((2,))],
        ),
        out_shape=jax.ShapeDtypeStruct((rows, D_MODEL), BF16),
        compiler_params=_cparams(("arbitrary",)),
        name="moe_dispatch",
    )(tb["dst"], hn, tb["pos_t"])
    n_items = rows // FFN_TM + N_EXPERTS
    sorted_o = pl.pallas_call(
        _ffn_kernel,
        grid_spec=pltpu.PrefetchScalarGridSpec(
            num_scalar_prefetch=5,
            grid=(n_items,),
            in_specs=[
                pl.BlockSpec((FFN_TM, D_MODEL), lambda q, e, ti, lo, hi, fi: (ti[q], 0)),
                pl.BlockSpec((1, D_MODEL, 2 * D_FF), lambda q, e, ti, lo, hi, fi: (e[q], 0, 0)),
                pl.BlockSpec((1, 1, 2 * D_FF), lambda q, e, ti, lo, hi, fi: (e[q], 0, 0)),
                pl.BlockSpec((1, D_FF, D_MODEL), lambda q, e, ti, lo, hi, fi: (e[q], 0, 0)),
                pl.BlockSpec((1, 1, D_MODEL), lambda q, e, ti, lo, hi, fi: (e[q], 0, 0)),
            ],
            out_specs=pl.BlockSpec((FFN_TM, D_MODEL), lambda q, e, ti, lo, hi, fi: (ti[q], 0)),
        ),
        out_shape=jax.ShapeDtypeStruct((rows, D_MODEL), BF16),
        compiler_params=_cparams(("arbitrary",)),
        name="moe_ffn",
    )(tb["item_e"], tb["item_tile"], tb["item_lo"], tb["item_hi"], tb["item_kind"], sorted_x, w1, b1, w2, b2)
    return pl.pallas_call(
        functools.partial(_combine_kernel, nb=nb),
        grid_spec=pltpu.PrefetchScalarGridSpec(
            num_scalar_prefetch=1,
            grid=(nb,),
            in_specs=[
                pl.BlockSpec((TOK_BLOCK, TOP_K), lambda b, s: (b, 0)),
                pl.BlockSpec((TOK_BLOCK, LANES), lambda b, s: (b, 0)),
                pl.BlockSpec((TOK_BLOCK, D_MODEL), lambda b, s: (b, 0)),
                pl.BlockSpec((1, D_MODEL), lambda b, s: (0, 0)),
                pl.BlockSpec(memory_space=pl.ANY),
            ],
            out_specs=pl.BlockSpec((TOK_BLOCK, D_MODEL), lambda b, s: (b, 0)),
            scratch_shapes=[pltpu.VMEM((2, ROWS_PER_BLOCK, D_MODEL), BF16), pltpu.SemaphoreType.DMA((2,))],
        ),
        out_shape=jax.ShapeDtypeStruct((t, D_MODEL), F32),
        compiler_params=_cparams(("arbitrary",)),
        name="moe_combine",
    )(tb["src"], tb["pos_c"], route, x1, norm_final, sorted_o)


def _pad_lanes(v, fill=0.0):
    v = v.reshape(1, -1).astype(F32)
    return jnp.pad(v, ((0, 0), (0, LANES - v.shape[1])), constant_values=fill)


def _stream(x3, p):
    bsz, seq, _ = x3.shape
    x = x3.reshape(bsz * seq, D_MODEL)
    proj, dt_f, dt_b = _inproj(x, p["norm_mix"], p["w_main"], p["w_dtf"], p["w_dtb"])
    f_in = proj[:, COL_F:].reshape(bsz, seq, D_F)
    f_mix = _fourier_mix(f_in, _dft_tables(seq)).reshape(bsz * seq, D_F)
    xbc = _conv_silu(proj, p["conv_w"], p["conv_b"], seq)
    y_f = _ssd_pass(xbc, dt_f, p["dtb_f"], p["alog_f"], p["expand"], seq, reverse=False)
    y = _ssd_pass(xbc, dt_b, p["dtb_b"], p["alog_b"], p["expand"], seq, reverse=True,
                  y_prev=y_f, d_skip=p["d_skip"])
    x1, hn, gates = _merge_route(y, proj, f_mix, x, p["ssm_norm"], p["w_fourier"], p["w_ssm_out"],
                                 p["w_out"], p["norm_ffn"], p["w_router"], p["b_router"])
    out = _moe_routed(hn, gates, x1, p["w1"], p["b1"], p["w2"], p["b2"], p["norm_final"])
    return out.reshape(bsz, seq, D_MODEL)


def kernel(x_prompt, x_sample, norm_mix, w_in, conv_w, conv_b, dt_bias_fwd, dt_bias_bwd, a_log_fwd, a_log_bwd, d_skip, ssm_norm, w_fourier, w_ssm_out, w_out, norm_ffn, w_router, b_router, w_gate_up, b_gate_up, w_down, b_down, norm_final):
    assert norm_mix.shape[0] == 1, "single-layer block"
    w = w_in[0]
    o_z, o_xbc, o_dt, o_gate = D_F, D_F + D_INNER, D_F + D_INNER + CONV_DIM, D_F + D_INNER + CONV_DIM + 2 * N_HEADS
    w_main = jnp.concatenate([w[:, o_z:o_xbc], w[:, o_xbc:o_dt], w[:, o_gate:], w[:, :D_F]], axis=1).astype(BF16)
    pad_dt = lambda m: jnp.pad(m, ((0, 0), (0, LANES - N_HEADS))).astype(BF16)
    head_of_chan = jnp.arange(D_INNER, dtype=jnp.int32) // HEADDIM
    expand = (jnp.arange(LANES, dtype=jnp.int32)[:, None] == head_of_chan[None, :]).astype(BF16)
    p = dict(
        norm_mix=norm_mix[0].reshape(1, D_MODEL),
        w_main=w_main,
        w_dtf=pad_dt(w[:, o_dt:o_dt + N_HEADS]),
        w_dtb=pad_dt(w[:, o_dt + N_HEADS:o_gate]),
        conv_w=conv_w[0], conv_b=conv_b[0].reshape(1, CONV_DIM),
        dtb_f=_pad_lanes(dt_bias_fwd[0]), dtb_b=_pad_lanes(dt_bias_bwd[0]),
        alog_f=_pad_lanes(a_log_fwd[0], NEG_BIG), alog_b=_pad_lanes(a_log_bwd[0], NEG_BIG),
        expand=expand,
        d_skip=jnp.repeat(d_skip[0].astype(F32), HEADDIM).reshape(1, D_INNER),
        ssm_norm=ssm_norm[0].reshape(1, D_INNER),
        w_fourier=w_fourier[0].astype(BF16), w_ssm_out=w_ssm_out[0].astype(BF16), w_out=w_out[0].astype(BF16),
        norm_ffn=norm_ffn[0].reshape(1, D_MODEL),
        w_router=jnp.pad(w_router[0], ((0, 0), (0, LANES - N_EXPERTS))).astype(BF16),
        b_router=_pad_lanes(b_router[0], NEG_BIG),
        w1=w_gate_up[0].astype(BF16), b1=b_gate_up[0].reshape(N_EXPERTS, 1, 2 * D_FF),
        w2=w_down[0].astype(BF16), b2=b_down[0].reshape(N_EXPERTS, 1, D_MODEL),
        norm_final=norm_final.reshape(1, D_MODEL),
    )
    return (_stream(x_prompt, p), _stream(x_sample, p))
```

```python
import functools
import math

import jax
import jax.numpy as jnp
import numpy as np
from jax import lax
from jax.experimental import pallas as pl
from jax.experimental.pallas import tpu as pltpu

F32 = jnp.float32
BF16 = jnp.bfloat16

D_MODEL = 1024
D_F = 1024
FGROUP = 256
D_INNER = 2048
HEADDIM = 64
N_HEADS = 32
N_GROUPS = 8
HEADS_PER_GROUP = N_HEADS // N_GROUPS
D_STATE = 128
D_CONV = 5
CHUNK = 128
CONV_DIM = D_INNER + 2 * N_GROUPS * D_STATE
N_EXPERTS = 32
TOP_K = 4
D_FF = 1024
SWIGLU_ALPHA = 1.702
SWIGLU_LIMIT = 7.0
EPS = 1e-5
NEG_BIG = -1e30

LANES = 128
HALO = 16
VMEM_LIMIT = 56 * 1024 * 1024

TOK_BLOCK = 512
PIECE = 16
FFN_TM = 512
_WORST_PIECES = (TOK_BLOCK * TOP_K + N_EXPERTS * (PIECE - 1) + PIECE - 1) // PIECE
_PIECES_PER_TILE = FFN_TM // PIECE
PIECES_PER_BLOCK = (_WORST_PIECES + _PIECES_PER_TILE - 1) // _PIECES_PER_TILE * _PIECES_PER_TILE
ROWS_PER_BLOCK = PIECES_PER_BLOCK * PIECE
ITEM_NONE, ITEM_WRITE, ITEM_MERGE, ITEM_ZERO = 0, 1, 2, 3

COL_Z = 0
COL_XBC = D_INNER
COL_GATE = COL_XBC + CONV_DIM
COL_F = COL_GATE + 2 * D_MODEL
PROJ_MAIN = COL_F + D_F


def _cparams(sem):
    return pltpu.CompilerParams(dimension_semantics=sem, vmem_limit_bytes=VMEM_LIMIT)


def _inproj_kernel(x_ref, nw_ref, w_ref, wdtf_ref, wdtb_ref, proj_ref, dtf_ref, dtb_ref, hn_ref):
    @pl.when(pl.program_id(1) == 0)
    def _():
        x = x_ref[...]
        ms = jnp.mean(x * x, axis=-1, keepdims=True)
        hn = (x * lax.rsqrt(ms + EPS) * nw_ref[...]).astype(BF16)
        hn_ref[...] = hn
        dtf_ref[...] = jnp.dot(hn, wdtf_ref[...], preferred_element_type=F32)
        dtb_ref[...] = jnp.dot(hn, wdtb_ref[...], preferred_element_type=F32)

    proj_ref[...] = jnp.dot(hn_ref[...], w_ref[...], preferred_element_type=F32).astype(BF16)


def _inproj(x, norm_w, w_main, w_dtf, w_dtb, tm=2048, tn=1024):
    t = x.shape[0]
    tm = min(tm, t)
    grid = (t // tm, PROJ_MAIN // tn)
    return pl.pallas_call(
        _inproj_kernel,
        grid=grid,
        in_specs=[
            pl.BlockSpec((tm, D_MODEL), lambda i, j: (i, 0)),
            pl.BlockSpec((1, D_MODEL), lambda i, j: (0, 0)),
            pl.BlockSpec((D_MODEL, tn), lambda i, j: (0, j)),
            pl.BlockSpec((D_MODEL, LANES), lambda i, j: (0, 0)),
            pl.BlockSpec((D_MODEL, LANES), lambda i, j: (0, 0)),
        ],
        out_specs=[
            pl.BlockSpec((tm, tn), lambda i, j: (i, j)),
            pl.BlockSpec((tm, LANES), lambda i, j: (i, 0)),
            pl.BlockSpec((tm, LANES), lambda i, j: (i, 0)),
        ],
        out_shape=[
            jax.ShapeDtypeStruct((t, PROJ_MAIN), BF16),
            jax.ShapeDtypeStruct((t, LANES), F32),
            jax.ShapeDtypeStruct((t, LANES), F32),
        ],
        scratch_shapes=[pltpu.VMEM((tm, D_MODEL), BF16)],
        compiler_params=_cparams(("parallel", "arbitrary")),
        name="inproj",
    )(x, norm_w, w_main, w_dtf, w_dtb)


def _dft_factors(seq):
    n1 = 1 << ((int(math.log2(seq)) + 1) // 2)
    return n1, seq // n1


def _dft_tables(seq):
    n1, n2 = _dft_factors(seq)
    two_pi = 2.0 * math.pi
    c = jnp.arange(FGROUP, dtype=jnp.int32)
    ph = ((c[:, None] * c[None, :]) % FGROUP).astype(F32) * (two_pi / FGROUP)
    cs_chan = jnp.concatenate([jnp.cos(ph), -jnp.sin(ph)], axis=1).astype(BF16)
    k1 = jnp.arange(n1, dtype=jnp.int32)
    nn = (n2 * jnp.arange(n1, dtype=jnp.int32)[None, :] + jnp.arange(n2, dtype=jnp.int32)[:, None])
    al = ((k1[None, :, None] * nn[:, None, :]) % seq).astype(F32) * (two_pi / seq)
    ca, sa = jnp.cos(al), jnp.sin(al)
    g1 = jnp.concatenate([jnp.concatenate([ca, sa], axis=2),
                          jnp.concatenate([-sa, ca], axis=2)], axis=1).astype(BF16)
    k2 = jnp.arange(n2, dtype=jnp.int32)
    be = ((k2[:, None] * k2[None, :]) % n2).astype(F32) * (two_pi / n2)
    g2 = jnp.concatenate([jnp.cos(be), jnp.sin(be)], axis=1).astype(BF16)
    return cs_chan, g1, g2


def _dft1_kernel(x_ref, cs_ref, g_ref, o_ref, *, tn2, n1):
    for j in range(tn2):
        x = x_ref[0, j]
        parts = []
        for q in range(D_F // FGROUP):
            uv = jnp.dot(x[:, q * FGROUP:(q + 1) * FGROUP], cs_ref[...],
                         preferred_element_type=F32).astype(BF16)
            parts.append(jnp.concatenate([uv[:, :FGROUP], uv[:, FGROUP:]], axis=0))
        z = jnp.concatenate(parts, axis=1)
        o_ref[0, j] = jnp.dot(g_ref[j], z, preferred_element_type=F32).astype(BF16)


def _dft2_kernel(a_ref, g_ref, o_ref, *, tk1, scale):
    for j in range(tk1):
        o_ref[0, j] = (jnp.dot(g_ref[...], a_ref[0, j], preferred_element_type=F32) * scale).astype(BF16)


def _fourier_mix(f_in, tables):
    bsz, seq, _ = f_in.shape
    n1, n2 = _dft_factors(seq)
    cs_chan, g1, g2 = tables
    xt = f_in.reshape(bsz, n1, n2, D_F).transpose(0, 2, 1, 3)
    tn2 = 4
    stage1 = pl.pallas_call(
        functools.partial(_dft1_kernel, tn2=tn2, n1=n1),
        grid=(bsz, n2 // tn2),
        in_specs=[
            pl.BlockSpec((1, tn2, n1, D_F), lambda b, i: (b, i, 0, 0)),
            pl.BlockSpec((FGROUP, 2 * FGROUP), lambda b, i: (0, 0)),
            pl.BlockSpec((tn2, 2 * n1, 2 * n1), lambda b, i: (i, 0, 0)),
        ],
        out_specs=pl.BlockSpec((1, tn2, 2 * n1, D_F), lambda b, i: (b, i, 0, 0)),
        out_shape=jax.ShapeDtypeStruct((bsz, n2, 2 * n1, D_F), BF16),
        compiler_params=_cparams(("parallel", "parallel")),
        name="dft_stage1",
    )(xt, cs_chan, g1)
    a2 = (stage1.reshape(bsz, n2, 2, n1, D_F).transpose(0, 3, 2, 1, 4)
          .reshape(bsz, n1, 2 * n2, D_F))
    tk1 = 4
    scale = 1.0 / math.sqrt(seq * FGROUP)
    stage2 = pl.pallas_call(
        functools.partial(_dft2_kernel, tk1=tk1, scale=scale),
        grid=(bsz, n1 // tk1),
        in_specs=[
            pl.BlockSpec((1, tk1, 2 * n2, D_F), lambda b, i: (b, i, 0, 0)),
            pl.BlockSpec((n2, 2 * n2), lambda b, i: (0, 0)),
        ],
        out_specs=pl.BlockSpec((1, tk1, n2, D_F), lambda b, i: (b, i, 0, 0)),
        out_shape=jax.ShapeDtypeStruct((bsz, n1, n2, D_F), BF16),
        compiler_params=_cparams(("parallel", "parallel")),
        name="dft_stage2",
    )(a2, g2)
    return stage2.transpose(0, 2, 1, 3).reshape(bsz, seq, D_F)


def _conv_kernel(prev_ref, main_ref, next_ref, w_ref, b_ref, o_ref, buf_ref, *, tl, tiles_per_seq):
    i = pl.program_id(0) % tiles_per_seq
    prev = prev_ref[...].astype(F32)
    nxt = next_ref[...].astype(F32)
    buf_ref[0:HALO, :] = jnp.where(i == 0, 0.0, prev)
    buf_ref[HALO:HALO + tl, :] = main_ref[...].astype(F32)
    buf_ref[HALO + tl:, :] = jnp.where(i == tiles_per_seq - 1, 0.0, nxt)
    pad = D_CONV // 2
    acc = jnp.zeros(o_ref.shape, F32) + b_ref[...]
    for k in range(D_CONV):
        acc = acc + w_ref[k:k + 1, :] * buf_ref[HALO + k - pad:HALO + k - pad + tl, :]
    o_ref[...] = (acc * jax.nn.sigmoid(acc)).astype(BF16)


def _conv_silu(proj, conv_w, conv_b, seq, tl=512, tc=1024):
    t = proj.shape[0]
    tl = min(tl, seq)
    tiles_per_seq = seq // tl
    cb0 = COL_XBC // tc
    hb = tl // HALO
    last_hb = t // HALO - 1
    return pl.pallas_call(
        functools.partial(_conv_kernel, tl=tl, tiles_per_seq=tiles_per_seq),
        grid=(t // tl, CONV_DIM // tc),
        in_specs=[
            pl.BlockSpec((HALO, tc), lambda i, c: (jnp.maximum(i * hb - 1, 0), cb0 + c)),
            pl.BlockSpec((tl, tc), lambda i, c: (i, cb0 + c)),
            pl.BlockSpec((HALO, tc), lambda i, c: (jnp.minimum((i + 1) * hb, last_hb), cb0 + c)),
            pl.BlockSpec((D_CONV, tc), lambda i, c: (0, c)),
            pl.BlockSpec((1, tc), lambda i, c: (0, c)),
        ],
        out_specs=pl.BlockSpec((tl, tc), lambda i, c: (i, c)),
        out_shape=jax.ShapeDtypeStruct((t, CONV_DIM), BF16),
        scratch_shapes=[pltpu.VMEM((tl + 2 * HALO, tc), F32)],
        compiler_params=_cparams(("parallel", "parallel")),
        name="conv_silu",
    )(proj, proj, proj, conv_w, conv_b)


def _split3(v):
    hi = v.astype(BF16)
    r1 = v - hi.astype(F32)
    mid = r1.astype(BF16)
    lo = (r1 - mid.astype(F32)).astype(BF16)
    return [hi, mid, lo]


def _masked_sums(tri, vals):
    parts = []
    for v in vals:
        parts += _split3(v)
    out = jnp.dot(tri, jnp.concatenate(parts, axis=1), preferred_element_type=F32)
    return [out[:, (3 * i) * LANES:(3 * i + 1) * LANES] + out[:, (3 * i + 1) * LANES:(3 * i + 2) * LANES]
            + out[:, (3 * i + 2) * LANES:(3 * i + 3) * LANES] for i in range(len(vals))]


def _ssd_kernel(xs_ref, b_ref, c_ref, dtf_ref, dtb_ref, xs2_ref, b2_ref, c2_ref, dtb2_ref,
                biasf_ref, biasb_ref, alogf_ref, alogb_ref, exp_ref, dskip_ref,
                y1_ref, y2_ref, statef_ref, stateb_ref):
    @pl.when(pl.program_id(1) == 0)
    def _():
        statef_ref[...] = jnp.zeros(statef_ref.shape, F32)
        stateb_ref[...] = jnp.zeros(stateb_ref.shape, F32)

    row = lax.broadcasted_iota(jnp.int32, (CHUNK, CHUNK), 0)
    col = lax.broadcasted_iota(jnp.int32, (CHUNK, CHUNK), 1)
    lower = row >= col
    diag = row == col
    tri_f = jnp.where(lower, 1.0, 0.0).astype(BF16)
    tri_b = jnp.where(row <= col, 1.0, 0.0).astype(BF16)

    af_neg = -jnp.exp(alogf_ref[...])
    ab_neg = -jnp.exp(alogb_ref[...])
    dt_f = jax.nn.softplus(dtf_ref[...] + biasf_ref[...])
    dt_b = jax.nn.softplus(dtb_ref[...] + biasb_ref[...])
    dt_b2 = jax.nn.softplus(dtb2_ref[...] + biasb_ref[...])
    (cf,) = _masked_sums(tri_f, [dt_f * af_neg])
    cb, cb2 = _masked_sums(tri_b, [dt_b * ab_neg, dt_b2 * ab_neg])
    tot_f = cf[CHUNK - 1:CHUNK, :]
    tot_b2 = cb2[0:1, :]
    cf_t, cb_t, dtf_t, dtb_t = cf.T, cb.T, dt_f.T, dt_b.T

    dec = jnp.concatenate(_split3(jnp.exp(tot_f)) + _split3(jnp.exp(tot_b2))
                          + [jnp.zeros((PIECE - 6, LANES), BF16)], axis=0)
    stack = jnp.concatenate([(dt_f * jnp.exp(tot_f - cf)).astype(BF16), jnp.exp(cf).astype(BF16),
                             (dt_b2 * jnp.exp(tot_b2 - cb2)).astype(BF16), jnp.exp(cb2).astype(BF16), dec], axis=0)
    ex = jnp.dot(stack, exp_ref[...], preferred_element_type=F32)
    wf_e, ecf_e = ex[0:CHUNK], ex[CHUNK:2 * CHUNK]
    wb_e, ecb_e = ex[2 * CHUNK:3 * CHUNK], ex[3 * CHUNK:4 * CHUNK]
    d0 = 4 * CHUNK
    decf_e = ex[d0:d0 + 1] + ex[d0 + 1:d0 + 2] + ex[d0 + 2:d0 + 3]
    decb_e = ex[d0 + 3:d0 + 4] + ex[d0 + 4:d0 + 5] + ex[d0 + 5:d0 + 6]

    xs = xs_ref[...]
    xd_f = (xs.astype(F32) * wf_e).astype(BF16)
    xd_b = (xs2_ref[...].astype(F32) * wb_e).astype(BF16)
    lane = lax.broadcasted_iota(jnp.int32, (CHUNK, LANES), 1)
    lo = lane < HEADDIM
    gw = HEADS_PER_GROUP * HEADDIM
    tn = (((0,), (0,)), ((), ()))
    for g in range(N_GROUPS):
        gs = slice(g * gw, (g + 1) * gw)
        ns = slice(g * D_STATE, (g + 1) * D_STATE)
        bg, cg = b_ref[:, ns], c_ref[:, ns]
        cbm = lax.dot_general(cg, bg, (((1,), (1,)), ((), ())), preferred_element_type=F32)
        prev_f = statef_ref[g]
        y_off = jnp.dot(cg, prev_f.astype(BF16), preferred_element_type=F32) * ecf_e[:, gs]
        statef_ref[g] = prev_f * decf_e[:, gs] + lax.dot_general(bg, xd_f[:, gs], tn, preferred_element_type=F32)
        prev_b = stateb_ref[g]
        y2_ref[:, gs] = (jnp.dot(c2_ref[:, ns], prev_b.astype(BF16), preferred_element_type=F32)
                         * ecb_e[:, gs]).astype(BF16)
        stateb_ref[g] = prev_b * decb_e[:, gs] + lax.dot_general(b2_ref[:, ns], xd_b[:, gs], tn,
                                                                 preferred_element_type=F32)
        for q in range(HEADS_PER_GROUP // 2):
            ms = []
            for hh in range(2):
                h = g * HEADS_PER_GROUP + 2 * q + hh
                seg = jnp.where(lower, cf[:, h:h + 1] - cf_t[h:h + 1, :], cb[:, h:h + 1] - cb_t[h:h + 1, :])
                wgt = jnp.exp(seg) * jnp.where(lower, dtf_t[h:h + 1, :], dtb_t[h:h + 1, :])
                wgt = wgt + jnp.where(diag, dtb_t[h:h + 1, :], 0.0)
                ms.append((wgt * cbm).astype(BF16))
            lhs = jnp.concatenate(ms, axis=1)
            c0 = g * gw + 2 * q * HEADDIM
            xp = xs[:, c0:c0 + LANES]
            zero = jnp.zeros_like(xp)
            rhs = jnp.concatenate([jnp.where(lo, xp, zero), jnp.where(lo, zero, xp)], axis=0)
            y = (jnp.dot(lhs, rhs, preferred_element_type=F32) + y_off[:, 2 * q * HEADDIM:2 * q * HEADDIM + LANES]
                 + dskip_ref[:, c0:c0 + LANES] * xp.astype(F32))
            y1_ref[:, c0:c0 + LANES] = y.astype(BF16)


def _ssd_scan(xbc, dt_f, dt_b, p, seq):
    t = xbc.shape[0]
    nc = seq // CHUNK
    near = lambda col: (lambda b, c: (b * nc + c, col))
    far = lambda col: (lambda b, c: (b * nc + nc - 1 - c, col))
    const = lambda b, c: (0, 0)
    gn = N_GROUPS * D_STATE
    state = pltpu.VMEM((N_GROUPS, D_STATE, HEADS_PER_GROUP * HEADDIM), F32)
    return pl.pallas_call(
        _ssd_kernel,
        grid=(t // seq, nc),
        in_specs=[
            pl.BlockSpec((CHUNK, D_INNER), near(0)),
            pl.BlockSpec((CHUNK, gn), near(D_INNER // gn)),
            pl.BlockSpec((CHUNK, gn), near(D_INNER // gn + 1)),
            pl.BlockSpec((CHUNK, LANES), near(0)),
            pl.BlockSpec((CHUNK, LANES), near(0)),
            pl.BlockSpec((CHUNK, D_INNER), far(0)),
            pl.BlockSpec((CHUNK, gn), far(D_INNER // gn)),
            pl.BlockSpec((CHUNK, gn), far(D_INNER // gn + 1)),
            pl.BlockSpec((CHUNK, LANES), far(0)),
            pl.BlockSpec((1, LANES), const), pl.BlockSpec((1, LANES), const),
            pl.BlockSpec((1, LANES), const), pl.BlockSpec((1, LANES), const),
            pl.BlockSpec((LANES, D_INNER), const),
            pl.BlockSpec((1, D_INNER), const),
        ],
        out_specs=[pl.BlockSpec((CHUNK, D_INNER), near(0)), pl.BlockSpec((CHUNK, D_INNER), far(0))],
        out_shape=[jax.ShapeDtypeStruct((t, D_INNER), BF16), jax.ShapeDtypeStruct((t, D_INNER), BF16)],
        scratch_shapes=[state, state],
        compiler_params=_cparams(("parallel", "arbitrary")),
        name="ssd_scan",
    )(xbc, xbc, xbc, dt_f, dt_b, xbc, xbc, xbc, dt_b,
      p["dtb_f"], p["dtb_b"], p["alog_f"], p["alog_b"], p["expand"], p["d_skip"])


def _merge_kernel(y1_ref, y2_ref, z_ref, gate_ref, fm_ref, x_ref, snw_ref, wf_ref, ws_ref, wo_ref, fnw_ref,
                  wr_ref, br_ref, x1_ref, hn_ref, gates_ref, blk_ref):
    z = z_ref[...].astype(F32)
    yg = (y1_ref[...].astype(F32) + y2_ref[...].astype(F32)) * (z * jax.nn.sigmoid(z))
    ms = jnp.mean(yg * yg, axis=-1, keepdims=True)
    yn = (yg * lax.rsqrt(ms + EPS) * snw_ref[...]).astype(BF16)
    u_s = jnp.dot(yn, ws_ref[...], preferred_element_type=F32)
    u_f = jnp.dot(fm_ref[...], wf_ref[...], preferred_element_type=F32)
    gl = gate_ref[...].astype(F32)
    merged = (jax.nn.sigmoid(gl[:, :D_MODEL]) * u_f + jax.nn.sigmoid(gl[:, D_MODEL:]) * u_s).astype(BF16)
    x1 = x_ref[...] + jnp.dot(merged, wo_ref[...], preferred_element_type=F32)
    x1_ref[...] = x1
    ms1 = jnp.mean(x1 * x1, axis=-1, keepdims=True)
    hn = (x1 * lax.rsqrt(ms1 + EPS) * fnw_ref[...]).astype(BF16)
    hn_ref[...] = hn
    logits = jnp.dot(hn, wr_ref[...], preferred_element_type=F32) + br_ref[...]
    lane = lax.broadcasted_iota(jnp.int32, logits.shape, 1).astype(F32)
    tm = logits.shape[0]
    work = logits
    top = None
    denom = jnp.zeros((tm, 1), F32)
    route = jnp.zeros(logits.shape, F32)
    probs, onehots = [], []
    for k in range(TOP_K):
        m = jnp.max(work, axis=-1, keepdims=True)
        if k == 0:
            top = m
        first = jnp.min(jnp.where(work == m, lane, float(LANES)), axis=-1, keepdims=True)
        sel = lane == first
        onehots.append(jnp.where(sel, 1.0, 0.0))
        work = jnp.where(sel, NEG_BIG * 2, work)
        pk = jnp.exp(m - top)
        denom = denom + pk
        probs.append(pk)
        route = jnp.where(lane == float(k), first, route)
    inv = 1.0 / denom
    for k in range(TOP_K):
        route = jnp.where(lane == float(TOP_K + k), probs[k] * inv, route)
    ti = lax.broadcasted_iota(jnp.int32, (tm, tm), 0)
    tj = lax.broadcasted_iota(jnp.int32, (tm, tm), 1)
    earlier = jnp.where(tj < ti, 1.0, 0.0).astype(BF16)
    prefix = jnp.dot(earlier, jnp.concatenate(onehots, axis=1).astype(BF16), preferred_element_type=F32)
    cnts = [prefix[tm - 1:tm, k * LANES:(k + 1) * LANES] + onehots[k][tm - 1:tm, :] for k in range(TOP_K)]
    pieces = jnp.floor((cnts[0] + cnts[1] + cnts[2] + cnts[3] + (PIECE - 1.0)) * (1.0 / PIECE))
    ei = lax.broadcasted_iota(jnp.int32, (LANES, LANES), 0)
    ej = lax.broadcasted_iota(jnp.int32, (LANES, LANES), 1)
    lower_experts = jnp.where(ei < ej, 1.0, 0.0).astype(BF16)
    seg_start = jnp.dot(jnp.broadcast_to(pieces, (8, LANES)).astype(BF16), lower_experts,
                        preferred_element_type=F32)[0:1, :]
    base = seg_start * float(PIECE)
    for k in range(TOP_K):
        row_of = base + prefix[:, k * LANES:(k + 1) * LANES]
        pos = jnp.sum(onehots[k] * row_of, axis=-1, keepdims=True)
        route = jnp.where(lane == float(2 * TOP_K + k), pos, route)
        base = base + cnts[k]
    gates_ref[...] = route
    blk_ref[0] = jnp.concatenate([pieces, seg_start, jnp.zeros((6, LANES), F32)], axis=0)


def _merge_route(y1, y2, proj, f_mix, x, ssm_norm, w_fourier, w_ssm_out, w_out, norm_ffn, w_router, b_router):
    t = x.shape[0]
    tm = TOK_BLOCK
    assert t % tm == 0
    full = lambda r, c: pl.BlockSpec((r, c), lambda i: (0, 0))
    return pl.pallas_call(
        _merge_kernel,
        grid=(t // tm,),
        in_specs=[
            pl.BlockSpec((tm, D_INNER), lambda i: (i, 0)),
            pl.BlockSpec((tm, D_INNER), lambda i: (i, 0)),
            pl.BlockSpec((tm, D_INNER), lambda i: (i, COL_Z // D_INNER)),
            pl.BlockSpec((tm, 2 * D_MODEL), lambda i: (i, COL_GATE // (2 * D_MODEL))),
            pl.BlockSpec((tm, D_F), lambda i: (i, 0)),
            pl.BlockSpec((tm, D_MODEL), lambda i: (i, 0)),
            full(1, D_INNER), full(D_F, D_MODEL), full(D_INNER, D_MODEL), full(D_MODEL, D_MODEL),
            full(1, D_MODEL), full(D_MODEL, LANES), full(1, LANES),
        ],
        out_specs=[
            pl.BlockSpec((tm, D_MODEL), lambda i: (i, 0)),
            pl.BlockSpec((tm, D_MODEL), lambda i: (i, 0)),
            pl.BlockSpec((tm, LANES), lambda i: (i, 0)),
            pl.BlockSpec((1, 8, LANES), lambda i: (i, 0, 0)),
        ],
        out_shape=[
            jax.ShapeDtypeStruct((t, D_MODEL), F32),
            jax.ShapeDtypeStruct((t, D_MODEL), BF16),
            jax.ShapeDtypeStruct((t, LANES), F32),
            jax.ShapeDtypeStruct((t // tm, 8, LANES), F32),
        ],
        compiler_params=_cparams(("parallel",)),
        name="merge_route",
    )(y1, y2, proj, proj, f_mix, x, ssm_norm, w_fourier, w_ssm_out, w_out, norm_ffn, w_router, b_router)


def _route_tables(route, blk, t):
    i32 = jnp.int32
    nb = t // TOK_BLOCK
    pcs = blk[:, 0, :N_EXPERTS].astype(i32)
    seg_start = blk[:, 1, :N_EXPERTS].astype(i32)
    seg_end = seg_start + pcs
    used = seg_end[:, -1]
    per_e = jnp.sum(pcs, axis=0)
    g_end = jnp.cumsum(per_e)
    g_start = g_end - per_e
    b_prefix = jnp.cumsum(pcs, axis=0) - pcs
    j = jnp.arange(PIECES_PER_BLOCK, dtype=i32)[None, :, None]
    in_seg = jnp.logical_and(j >= seg_start[:, None, :], j < seg_end[:, None, :])
    shift = (g_start[None, :] + b_prefix - seg_start)[:, None, :]
    j2 = j[:, :, 0]
    dst_used = jnp.sum(jnp.where(in_seg, shift, 0), axis=2) + j2
    free = PIECES_PER_BLOCK - used
    dst_unused = g_end[-1] + (jnp.cumsum(free) - free)[:, None] + j2 - used[:, None]
    is_used = j2 < used[:, None]
    dst = jnp.where(is_used, dst_used, dst_unused).reshape(-1)
    src = jnp.where(is_used, dst_used, 0).reshape(-1)
    n_tiles = nb * ROWS_PER_BLOCK // FFN_TM
    n_items_max = n_tiles + N_EXPERTS
    start_rows, end_rows = g_start * PIECE, g_end * PIECE
    t0 = start_rows // FFN_TM
    n_it = jnp.where(end_rows > start_rows, (end_rows - 1) // FFN_TM - t0 + 1, 0)
    it_end = jnp.cumsum(n_it)
    it_start = it_end - n_it
    n_exp = it_end[-1]
    tiles_used = (end_rows[-1] + FFN_TM - 1) // FFN_TM
    q = jnp.arange(n_items_max, dtype=i32)
    qc = jnp.minimum(q, n_exp - 1)[:, None]
    mine = jnp.logical_and(qc >= it_start[None, :], qc < it_end[None, :])
    pick = lambda v: jnp.sum(jnp.where(mine, v[None, :], 0), axis=1)
    e_q = pick(jnp.arange(N_EXPERTS, dtype=i32))
    tile_exp = pick(t0 - it_start) + qc[:, 0]
    tile_q = jnp.where(q < n_exp, tile_exp, jnp.minimum(tiles_used + q - n_exp, n_tiles - 1))
    lo = jnp.maximum(pick(start_rows), tile_q * FFN_TM) - tile_q * FFN_TM
    hi = jnp.minimum(pick(end_rows), (tile_q + 1) * FFN_TM) - tile_q * FFN_TM
    new_tile = jnp.concatenate([jnp.ones((1,), i32), (tile_q[1:] != tile_q[:-1]).astype(i32)])
    kind = jnp.where(q < n_exp, jnp.where(new_tile == 1, ITEM_WRITE, ITEM_MERGE),
                     jnp.where(new_tile == 1, ITEM_ZERO, ITEM_NONE))
    pos_t = (route[:, 2 * TOP_K:3 * TOP_K].astype(i32).reshape(nb, TOK_BLOCK, TOP_K).transpose(0, 2, 1))
    return dict(pos_t=pos_t, dst=dst, src=src,
                item_e=e_q, item_tile=tile_q, item_lo=lo, item_hi=hi, item_kind=kind)


def _piece_copy_out(buf_ref, hbm_ref, sem, slot, j, dst_piece):
    return pltpu.make_async_copy(buf_ref.at[slot, pl.ds(pl.multiple_of(j * PIECE, PIECE), PIECE)],
                                 hbm_ref.at[pl.ds(pl.multiple_of(dst_piece * PIECE, PIECE), PIECE)],
                                 sem.at[slot])


def _piece_copy_in(hbm_ref, buf_ref, sem, slot, j, src_piece):
    return pltpu.make_async_copy(hbm_ref.at[pl.ds(pl.multiple_of(src_piece * PIECE, PIECE), PIECE)],
                                 buf_ref.at[slot, pl.ds(pl.multiple_of(j * PIECE, PIECE), PIECE)],
                                 sem.at[slot])


def _dispatch_kernel(dst_ref, hn_ref, post_ref, xs_hbm, buf_ref, sem, *, nb):
    b = pl.program_id(0)
    slot = b % 2

    def wait_slot(s):
        def body(j, c):
            _piece_copy_out(buf_ref, xs_hbm, sem, s, j, 0).wait()
            return c
        lax.fori_loop(0, PIECES_PER_BLOCK, body, 0)

    @pl.when(b >= 2)
    def _():
        wait_slot(slot)

    r = lax.broadcasted_iota(jnp.int32, (ROWS_PER_BLOCK, TOK_BLOCK), 0)
    onehot = jnp.zeros((ROWS_PER_BLOCK, TOK_BLOCK), F32)
    for k in range(TOP_K):
        onehot = jnp.where(r == post_ref[0, k:k + 1, :], 1.0, onehot)
    buf_ref[slot] = jnp.dot(onehot.astype(BF16), hn_ref[...], preferred_element_type=F32).astype(BF16)

    def start(j, c):
        _piece_copy_out(buf_ref, xs_hbm, sem, slot, j, dst_ref[b * PIECES_PER_BLOCK + j]).start()
        return c
    lax.fori_loop(0, PIECES_PER_BLOCK, start, 0)

    @pl.when(b == nb - 1)
    def _():
        wait_slot(slot)
        if nb >= 2:
            wait_slot(1 - slot)


def _ffn_kernel(e_ref, tile_ref, lo_ref, hi_ref, kind_ref, x_ref, w1_ref, b1_ref, w2_ref, b2_ref, o_ref):
    q = pl.program_id(0)
    lo, hi, kind = lo_ref[q], hi_ref[q], kind_ref[q]

    def ffn():
        hu = jnp.dot(x_ref[...], w1_ref[0], preferred_element_type=F32) + b1_ref[0]
        glu = jnp.minimum(hu[:, :D_FF], SWIGLU_LIMIT)
        lin = jnp.clip(hu[:, D_FF:], -SWIGLU_LIMIT, SWIGLU_LIMIT)
        act = (glu * jax.nn.sigmoid(SWIGLU_ALPHA * glu) * (lin + 1.0)).astype(BF16)
        return (jnp.dot(act, w2_ref[0], preferred_element_type=F32) + b2_ref[0]).astype(BF16)

    @pl.when(kind == ITEM_WRITE)
    def _():
        o_ref[...] = ffn()

    @pl.when(kind == ITEM_MERGE)
    def _():
        row = lax.broadcasted_iota(jnp.int32, (FFN_TM, 1), 0)
        mine = jnp.logical_and(row >= lo, row < hi)
        o_ref[...] = jnp.where(mine, ffn(), o_ref[...])

    @pl.when(kind == ITEM_ZERO)
    def _():
        o_ref[...] = jnp.zeros(o_ref.shape, BF16)


def _combine_kernel(src_ref, route_ref, x1_ref, nf_ref, os_hbm, o_ref, buf_ref, sem, *, nb):
    b = pl.program_id(0)
    slot = b % 2

    def fetch(bb, s):
        def body(j, c):
            _piece_copy_in(os_hbm, buf_ref, sem, s, j, src_ref[bb * PIECES_PER_BLOCK + j]).start()
            return c
        lax.fori_loop(0, PIECES_PER_BLOCK, body, 0)

    @pl.when(b == 0)
    def _():
        fetch(0, 0)

    @pl.when(b + 1 < nb)
    def _():
        fetch(b + 1, 1 - slot)

    def wait(j, c):
        _piece_copy_in(os_hbm, buf_ref, sem, slot, j, 0).wait()
        return c
    lax.fori_loop(0, PIECES_PER_BLOCK, wait, 0)

    r = lax.broadcasted_iota(jnp.int32, (TOK_BLOCK, ROWS_PER_BLOCK), 1).astype(F32)
    route = route_ref[...]
    wmat = jnp.zeros((TOK_BLOCK, ROWS_PER_BLOCK), F32)
    for k in range(TOP_K):
        wmat = jnp.where(r == route[:, 2 * TOP_K + k:2 * TOP_K + k + 1], route[:, TOP_K + k:TOP_K + k + 1], wmat)
    y = jnp.dot(wmat.astype(BF16), buf_ref[slot], preferred_element_type=F32)
    x2 = x1_ref[...] + y
    ms = jnp.mean(x2 * x2, axis=-1, keepdims=True)
    o_ref[...] = x2 * lax.rsqrt(ms + EPS) * nf_ref[...]


def _moe_routed(hn, route, blk, x1, w1, b1, w2, b2, norm_final):
    t = hn.shape[0]
    assert t % TOK_BLOCK == 0 and ROWS_PER_BLOCK % FFN_TM == 0
    nb = t // TOK_BLOCK
    rows = nb * ROWS_PER_BLOCK
    tb = _route_tables(route, blk, t)
    sorted_x = pl.pallas_call(
        functools.partial(_dispatch_kernel, nb=nb),
        grid_spec=pltpu.PrefetchScalarGridSpec(
            num_scalar_prefetch=1,
            grid=(nb,),
            in_specs=[
                pl.BlockSpec((TOK_BLOCK, D_MODEL), lambda b, d: (b, 0)),
                pl.BlockSpec((1, TOP_K, TOK_BLOCK), lambda b, d: (b, 0, 0)),
            ],
            out_specs=pl.BlockSpec(memory_space=pl.ANY),
            scratch_shapes=[pltpu.VMEM((2, ROWS_PER_BLOCK, D_MODEL), BF16), pltpu.SemaphoreType.DMA((2,))],
        ),
        out_shape=jax.ShapeDtypeStruct((rows, D_MODEL), BF16),
        compiler_params=_cparams(("arbitrary",)),
        name="moe_dispatch",
    )(tb["dst"], hn, tb["pos_t"])
    n_items = rows // FFN_TM + N_EXPERTS
    sorted_o = pl.pallas_call(
        _ffn_kernel,
        grid_spec=pltpu.PrefetchScalarGridSpec(
            num_scalar_prefetch=5,
            grid=(n_items,),
            in_specs=[
                pl.BlockSpec((FFN_TM, D_MODEL), lambda q, e, ti, lo, hi, fi: (ti[q], 0)),
                pl.BlockSpec((1, D_MODEL, 2 * D_FF), lambda q, e, ti, lo, hi, fi: (e[q], 0, 0)),
                pl.BlockSpec((1, 1, 2 * D_FF), lambda q, e, ti, lo, hi, fi: (e[q], 0, 0)),
                pl.BlockSpec((1, D_FF, D_MODEL), lambda q, e, ti, lo, hi, fi: (e[q], 0, 0)),
                pl.BlockSpec((1, 1, D_MODEL), lambda q, e, ti, lo, hi, fi: (e[q], 0, 0)),
            ],
            out_specs=pl.BlockSpec((FFN_TM, D_MODEL), lambda q, e, ti, lo, hi, fi: (ti[q], 0)),
        ),
        out_shape=jax.ShapeDtypeStruct((rows, D_MODEL), BF16),
        compiler_params=_cparams(("arbitrary",)),
        name="moe_ffn",
    )(tb["item_e"], tb["item_tile"], tb["item_lo"], tb["item_hi"], tb["item_kind"], sorted_x, w1, b1, w2, b2)
    return pl.pallas_call(
        functools.partial(_combine_kernel, nb=nb),
        grid_spec=pltpu.PrefetchScalarGridSpec(
            num_scalar_prefetch=1,
            grid=(nb,),
            in_specs=[
                pl.BlockSpec((TOK_BLOCK, LANES), lambda b, s: (b, 0)),
                pl.BlockSpec((TOK_BLOCK, D_MODEL), lambda b, s: (b, 0)),
                pl.BlockSpec((1, D_MODEL), lambda b, s: (0, 0)),
                pl.BlockSpec(memory_space=pl.ANY),
            ],
            out_specs=pl.BlockSpec((TOK_BLOCK, D_MODEL), lambda b, s: (b, 0)),
            scratch_shapes=[pltpu.VMEM((2, ROWS_PER_BLOCK, D_MODEL), BF16), pltpu.SemaphoreType.DMA((2,))],
        ),
        out_shape=jax.ShapeDtypeStruct((t, D_MODEL), F32),
        compiler_params=_cparams(("arbitrary",)),
        name="moe_combine",
    )(tb["src"], route, x1, norm_final, sorted_o)


def _pad_lanes(v, fill=0.0):
    v = v.reshape(1, -1).astype(F32)
    return jnp.pad(v, ((0, 0), (0, LANES - v.shape[1])), constant_values=fill)


def _stream(x3, p):
    bsz, seq, _ = x3.shape
    x = x3.reshape(bsz * seq, D_MODEL)
    proj, dt_f, dt_b = _inproj(x, p["norm_mix"], p["w_main"], p["w_dtf"], p["w_dtb"])
    f_in = proj[:, COL_F:].reshape(bsz, seq, D_F)
    f_mix = _fourier_mix(f_in, _dft_tables(seq)).reshape(bsz * seq, D_F)
    xbc = _conv_silu(proj, p["conv_w"], p["conv_b"], seq)
    y1, y2 = _ssd_scan(xbc, dt_f, dt_b, p, seq)
    x1, hn, route, blk = _merge_route(y1, y2, proj, f_mix, x, p["ssm_norm"], p["w_fourier"], p["w_ssm_out"],
                                 p["w_out"], p["norm_ffn"], p["w_router"], p["b_router"])
    out = _moe_routed(hn, route, blk, x1, p["w1"], p["b1"], p["w2"], p["b2"], p["norm_final"])
    return out.reshape(bsz, seq, D_MODEL)


def kernel(x_prompt, x_sample, norm_mix, w_in, conv_w, conv_b, dt_bias_fwd, dt_bias_bwd, a_log_fwd, a_log_bwd, d_skip, ssm_norm, w_fourier, w_ssm_out, w_out, norm_ffn, w_router, b_router, w_gate_up, b_gate_up, w_down, b_down, norm_final):
    assert norm_mix.shape[0] == 1, "single-layer block"
    w = w_in[0]
    o_z, o_xbc, o_dt, o_gate = D_F, D_F + D_INNER, D_F + D_INNER + CONV_DIM, D_F + D_INNER + CONV_DIM + 2 * N_HEADS
    w_main = jnp.concatenate([w[:, o_z:o_xbc], w[:, o_xbc:o_dt], w[:, o_gate:], w[:, :D_F]], axis=1).astype(BF16)
    pad_dt = lambda m: jnp.pad(m, ((0, 0), (0, LANES - N_HEADS))).astype(BF16)
    head_of_chan = jnp.arange(D_INNER, dtype=jnp.int32) // HEADDIM
    expand = (jnp.arange(LANES, dtype=jnp.int32)[:, None] == head_of_chan[None, :]).astype(BF16)
    p = dict(
        norm_mix=norm_mix[0].reshape(1, D_MODEL),
        w_main=w_main,
        w_dtf=pad_dt(w[:, o_dt:o_dt + N_HEADS]),
        w_dtb=pad_dt(w[:, o_dt + N_HEADS:o_gate]),
        conv_w=conv_w[0], conv_b=conv_b[0].reshape(1, CONV_DIM),
        dtb_f=_pad_lanes(dt_bias_fwd[0]), dtb_b=_pad_lanes(dt_bias_bwd[0]),
        alog_f=_pad_lanes(a_log_fwd[0], NEG_BIG), alog_b=_pad_lanes(a_log_bwd[0], NEG_BIG),
        expand=expand,
        d_skip=jnp.repeat(d_skip[0].astype(F32), HEADDIM).reshape(1, D_INNER),
        ssm_norm=ssm_norm[0].reshape(1, D_INNER),
        w_fourier=w_fourier[0].astype(BF16), w_ssm_out=w_ssm_out[0].astype(BF16), w_out=w_out[0].astype(BF16),
        norm_ffn=norm_ffn[0].reshape(1, D_MODEL),
        w_router=jnp.pad(w_router[0], ((0, 0), (0, LANES - N_EXPERTS))).astype(BF16),
        b_router=_pad_lanes(b_router[0], NEG_BIG),
        w1=w_gate_up[0].astype(BF16), b1=b_gate_up[0].reshape(N_EXPERTS, 1, 2 * D_FF),
        w2=w_down[0].astype(BF16), b2=b_down[0].reshape(N_EXPERTS, 1, D_MODEL),
        norm_final=norm_final.reshape(1, D_MODEL),
    )
    return (_stream(x_prompt, p), _stream(x_sample, p))
```

```python
import functools
import math

import jax
import jax.numpy as jnp
import numpy as np
from jax import lax
from jax.experimental import pallas as pl
from jax.experimental.pallas import tpu as pltpu

F32 = jnp.float32
BF16 = jnp.bfloat16

D_MODEL = 1024
D_F = 1024
FGROUP = 256
D_INNER = 2048
HEADDIM = 64
N_HEADS = 32
N_GROUPS = 8
HEADS_PER_GROUP = N_HEADS // N_GROUPS
D_STATE = 128
D_CONV = 5
CHUNK = 128
CONV_DIM = D_INNER + 2 * N_GROUPS * D_STATE
N_EXPERTS = 32
TOP_K = 4
D_FF = 1024
SWIGLU_ALPHA = 1.702
SWIGLU_LIMIT = 7.0
EPS = 1e-5
NEG_BIG = -1e30

LANES = 128
HALO = 16
CONV_SUB = 128
VMEM_LIMIT = 56 * 1024 * 1024

TOK_BLOCK = 512
PIECE = 16
FFN_TM = 512
_WORST_PIECES = (TOK_BLOCK * TOP_K + N_EXPERTS * (PIECE - 1) + PIECE - 1) // PIECE
_PIECES_PER_TILE = FFN_TM // PIECE
PIECES_PER_BLOCK = (_WORST_PIECES + _PIECES_PER_TILE - 1) // _PIECES_PER_TILE * _PIECES_PER_TILE
ROWS_PER_BLOCK = PIECES_PER_BLOCK * PIECE
DMA_UNROLL = 8
ITEM_NONE, ITEM_WRITE, ITEM_MERGE, ITEM_ZERO = 0, 1, 2, 3

COL_Z = 0
COL_XBC = D_INNER
COL_GATE = COL_XBC + CONV_DIM
COL_F = COL_GATE + 2 * D_MODEL
PROJ_MAIN = COL_F + D_F


def _cparams(sem):
    return pltpu.CompilerParams(dimension_semantics=sem, vmem_limit_bytes=VMEM_LIMIT)


def _inproj_kernel(x_ref, nw_ref, w_ref, wdtf_ref, wdtb_ref, proj_ref, dtf_ref, dtb_ref, hn_ref):
    @pl.when(pl.program_id(1) == 0)
    def _():
        x = x_ref[...]
        ms = jnp.mean(x * x, axis=-1, keepdims=True)
        hn = (x * lax.rsqrt(ms + EPS) * nw_ref[...]).astype(BF16)
        hn_ref[...] = hn
        dtf_ref[...] = jnp.dot(hn, wdtf_ref[...], preferred_element_type=F32)
        dtb_ref[...] = jnp.dot(hn, wdtb_ref[...], preferred_element_type=F32)

    proj_ref[...] = jnp.dot(hn_ref[...], w_ref[...], preferred_element_type=F32).astype(BF16)


def _inproj(x, norm_w, w_main, w_dtf, w_dtb, tm=2048, tn=1024):
    t = x.shape[0]
    tm = min(tm, t)
    grid = (t // tm, PROJ_MAIN // tn)
    return pl.pallas_call(
        _inproj_kernel,
        grid=grid,
        in_specs=[
            pl.BlockSpec((tm, D_MODEL), lambda i, j: (i, 0)),
            pl.BlockSpec((1, D_MODEL), lambda i, j: (0, 0)),
            pl.BlockSpec((D_MODEL, tn), lambda i, j: (0, j)),
            pl.BlockSpec((D_MODEL, LANES), lambda i, j: (0, 0)),
            pl.BlockSpec((D_MODEL, LANES), lambda i, j: (0, 0)),
        ],
        out_specs=[
            pl.BlockSpec((tm, tn), lambda i, j: (i, j)),
            pl.BlockSpec((tm, LANES), lambda i, j: (i, 0)),
            pl.BlockSpec((tm, LANES), lambda i, j: (i, 0)),
        ],
        out_shape=[
            jax.ShapeDtypeStruct((t, PROJ_MAIN), BF16),
            jax.ShapeDtypeStruct((t, LANES), F32),
            jax.ShapeDtypeStruct((t, LANES), F32),
        ],
        scratch_shapes=[pltpu.VMEM((tm, D_MODEL), BF16)],
        compiler_params=_cparams(("parallel", "arbitrary")),
        name="inproj",
    )(x, norm_w, w_main, w_dtf, w_dtb)


def _dft_factors(seq):
    n1 = 1 << ((int(math.log2(seq)) + 1) // 2)
    return n1, seq // n1


def _dft_tables(seq):
    n1, n2 = _dft_factors(seq)
    two_pi = 2.0 * math.pi
    c = jnp.arange(FGROUP, dtype=jnp.int32)
    ph = ((c[:, None] * c[None, :]) % FGROUP).astype(F32) * (two_pi / FGROUP)
    cs_chan = jnp.concatenate([jnp.cos(ph), -jnp.sin(ph)], axis=1).astype(BF16)
    k1 = jnp.arange(n1, dtype=jnp.int32)
    nn = (n2 * jnp.arange(n1, dtype=jnp.int32)[None, :] + jnp.arange(n2, dtype=jnp.int32)[:, None])
    al = ((k1[None, :, None] * nn[:, None, :]) % seq).astype(F32) * (two_pi / seq)
    ca, sa = jnp.cos(al), jnp.sin(al)
    g1 = jnp.concatenate([jnp.concatenate([ca, sa], axis=2),
                          jnp.concatenate([-sa, ca], axis=2)], axis=1).astype(BF16)
    k2 = jnp.arange(n2, dtype=jnp.int32)
    be = ((k2[:, None] * k2[None, :]) % n2).astype(F32) * (two_pi / n2)
    g2 = jnp.concatenate([jnp.cos(be), jnp.sin(be)], axis=1).astype(BF16)
    return cs_chan, g1, g2


def _dft1_kernel(x_ref, cs_ref, g_ref, o_ref, *, tn2, n1):
    for j in range(tn2):
        x = x_ref[0, j]
        parts = []
        for q in range(D_F // FGROUP):
            uv = jnp.dot(x[:, q * FGROUP:(q + 1) * FGROUP], cs_ref[...],
                         preferred_element_type=F32).astype(BF16)
            parts.append(jnp.concatenate([uv[:, :FGROUP], uv[:, FGROUP:]], axis=0))
        z = jnp.concatenate(parts, axis=1)
        o_ref[0, j] = jnp.dot(g_ref[j], z, preferred_element_type=F32).astype(BF16)


def _dft2_kernel(a_ref, g_ref, o_ref, *, tk1, scale):
    for j in range(tk1):
        o_ref[0, j] = (jnp.dot(g_ref[...], a_ref[0, j], preferred_element_type=F32) * scale).astype(BF16)


def _fourier_mix(f_in, tables):
    bsz, seq, _ = f_in.shape
    n1, n2 = _dft_factors(seq)
    cs_chan, g1, g2 = tables
    xt = f_in.reshape(bsz, n1, n2, D_F).transpose(0, 2, 1, 3)
    tn2 = 4
    stage1 = pl.pallas_call(
        functools.partial(_dft1_kernel, tn2=tn2, n1=n1),
        grid=(bsz, n2 // tn2),
        in_specs=[
            pl.BlockSpec((1, tn2, n1, D_F), lambda b, i: (b, i, 0, 0)),
            pl.BlockSpec((FGROUP, 2 * FGROUP), lambda b, i: (0, 0)),
            pl.BlockSpec((tn2, 2 * n1, 2 * n1), lambda b, i: (i, 0, 0)),
        ],
        out_specs=pl.BlockSpec((1, tn2, 2 * n1, D_F), lambda b, i: (b, i, 0, 0)),
        out_shape=jax.ShapeDtypeStruct((bsz, n2, 2 * n1, D_F), BF16),
        compiler_params=_cparams(("parallel", "parallel")),
        name="dft_stage1",
    )(xt, cs_chan, g1)
    a2 = (stage1.reshape(bsz, n2, 2, n1, D_F).transpose(0, 3, 2, 1, 4)
          .reshape(bsz, n1, 2 * n2, D_F))
    tk1 = 4
    scale = 1.0 / math.sqrt(seq * FGROUP)
    stage2 = pl.pallas_call(
        functools.partial(_dft2_kernel, tk1=tk1, scale=scale),
        grid=(bsz, n1 // tk1),
        in_specs=[
            pl.BlockSpec((1, tk1, 2 * n2, D_F), lambda b, i: (b, i, 0, 0)),
            pl.BlockSpec((n2, 2 * n2), lambda b, i: (0, 0)),
        ],
        out_specs=pl.BlockSpec((1, tk1, n2, D_F), lambda b, i: (b, i, 0, 0)),
        out_shape=jax.ShapeDtypeStruct((bsz, n1, n2, D_F), BF16),
        compiler_params=_cparams(("parallel", "parallel")),
        name="dft_stage2",
    )(a2, g2)
    return stage2.transpose(0, 2, 1, 3).reshape(bsz, seq, D_F)


def _conv_kernel(prev_ref, main_ref, next_ref, w_ref, b_ref, o_ref, *, tl, tiles_per_seq):
    i = pl.program_id(0) % tiles_per_seq
    halo_zero = jnp.zeros(prev_ref.shape, BF16)
    prev = jnp.where(i == 0, halo_zero, prev_ref[...])
    nxt = jnp.where(i == tiles_per_seq - 1, halo_zero, next_ref[...])
    full = jnp.concatenate([prev, main_ref[...], nxt], axis=0)
    pad = D_CONV // 2
    win = CONV_SUB + 2 * HALO
    r = lax.broadcasted_iota(jnp.int32, (CONV_SUB, win), 0)
    c = lax.broadcasted_iota(jnp.int32, (CONV_SUB, win), 1)
    taps = [k for k in range(D_CONV) if k != pad]
    shift = jnp.concatenate([jnp.where(c == r + (HALO + k - pad), 1.0, 0.0) for k in taps], axis=0).astype(BF16)
    for j in range(tl // CONV_SUB):
        window = full[j * CONV_SUB:j * CONV_SUB + win]
        moved = jnp.dot(shift, window, preferred_element_type=F32)
        acc = b_ref[...] + w_ref[pad:pad + 1, :] * window[HALO:HALO + CONV_SUB].astype(F32)
        for n, k in enumerate(taps):
            acc = acc + w_ref[k:k + 1, :] * moved[n * CONV_SUB:(n + 1) * CONV_SUB]
        o_ref[j * CONV_SUB:(j + 1) * CONV_SUB, :] = (acc * jax.nn.sigmoid(acc)).astype(BF16)


def _conv_silu(proj, conv_w, conv_b, seq, tl=512, tc=1024):
    t = proj.shape[0]
    tl = min(tl, seq)
    tiles_per_seq = seq // tl
    cb0 = COL_XBC // tc
    hb = tl // HALO
    last_hb = t // HALO - 1
    return pl.pallas_call(
        functools.partial(_conv_kernel, tl=tl, tiles_per_seq=tiles_per_seq),
        grid=(t // tl, CONV_DIM // tc),
        in_specs=[
            pl.BlockSpec((HALO, tc), lambda i, c: (jnp.maximum(i * hb - 1, 0), cb0 + c)),
            pl.BlockSpec((tl, tc), lambda i, c: (i, cb0 + c)),
            pl.BlockSpec((HALO, tc), lambda i, c: (jnp.minimum((i + 1) * hb, last_hb), cb0 + c)),
            pl.BlockSpec((D_CONV, tc), lambda i, c: (0, c)),
            pl.BlockSpec((1, tc), lambda i, c: (0, c)),
        ],
        out_specs=pl.BlockSpec((tl, tc), lambda i, c: (i, c)),
        out_shape=jax.ShapeDtypeStruct((t, CONV_DIM), BF16),
        compiler_params=_cparams(("parallel", "parallel")),
        name="conv_silu",
    )(proj, proj, proj, conv_w, conv_b)


def _split3(v):
    hi = v.astype(BF16)
    r1 = v - hi.astype(F32)
    mid = r1.astype(BF16)
    lo = (r1 - mid.astype(F32)).astype(BF16)
    return [hi, mid, lo]


def _masked_sums(tri, vals):
    parts = []
    for v in vals:
        parts += _split3(v)
    out = jnp.dot(tri, jnp.concatenate(parts, axis=1), preferred_element_type=F32)
    return [out[:, (3 * i) * LANES:(3 * i + 1) * LANES] + out[:, (3 * i + 1) * LANES:(3 * i + 2) * LANES]
            + out[:, (3 * i + 2) * LANES:(3 * i + 3) * LANES] for i in range(len(vals))]


def _ssd_kernel(xs_ref, b_ref, c_ref, dtf_ref, dtb_ref, xs2_ref, b2_ref, c2_ref, dtb2_ref,
                biasf_ref, biasb_ref, alogf_ref, alogb_ref, exp_ref, dskip_ref,
                y1_ref, y2_ref, statef_ref, stateb_ref):
    @pl.when(pl.program_id(1) == 0)
    def _():
        statef_ref[...] = jnp.zeros(statef_ref.shape, F32)
        stateb_ref[...] = jnp.zeros(stateb_ref.shape, F32)

    row = lax.broadcasted_iota(jnp.int32, (CHUNK, CHUNK), 0)
    col = lax.broadcasted_iota(jnp.int32, (CHUNK, CHUNK), 1)
    lower = row >= col
    diag = row == col
    tri_f = jnp.where(lower, 1.0, 0.0).astype(BF16)
    tri_b = jnp.where(row <= col, 1.0, 0.0).astype(BF16)

    af_neg = -jnp.exp(alogf_ref[...])
    ab_neg = -jnp.exp(alogb_ref[...])
    dt_f = jax.nn.softplus(dtf_ref[...] + biasf_ref[...])
    dt_b = jax.nn.softplus(dtb_ref[...] + biasb_ref[...])
    dt_b2 = jax.nn.softplus(dtb2_ref[...] + biasb_ref[...])
    (cf,) = _masked_sums(tri_f, [dt_f * af_neg])
    cb, cb2 = _masked_sums(tri_b, [dt_b * ab_neg, dt_b2 * ab_neg])
    tot_f = cf[CHUNK - 1:CHUNK, :]
    tot_b2 = cb2[0:1, :]
    cf_t, cb_t, dtf_t, dtb_t = cf.T, cb.T, dt_f.T, dt_b.T

    dec = jnp.concatenate(_split3(jnp.exp(tot_f)) + _split3(jnp.exp(tot_b2))
                          + [jnp.zeros((PIECE - 6, LANES), BF16)], axis=0)
    stack = jnp.concatenate([(dt_f * jnp.exp(tot_f - cf)).astype(BF16), jnp.exp(cf).astype(BF16),
                             (dt_b2 * jnp.exp(tot_b2 - cb2)).astype(BF16), jnp.exp(cb2).astype(BF16), dec], axis=0)
    ex = jnp.dot(stack, exp_ref[...], preferred_element_type=F32)
    wf_e, ecf_e = ex[0:CHUNK], ex[CHUNK:2 * CHUNK]
    wb_e, ecb_e = ex[2 * CHUNK:3 * CHUNK], ex[3 * CHUNK:4 * CHUNK]
    d0 = 4 * CHUNK
    decf_e = ex[d0:d0 + 1] + ex[d0 + 1:d0 + 2] + ex[d0 + 2:d0 + 3]
    decb_e = ex[d0 + 3:d0 + 4] + ex[d0 + 4:d0 + 5] + ex[d0 + 5:d0 + 6]

    xs = xs_ref[...]
    xd_f = (xs.astype(F32) * wf_e).astype(BF16)
    xd_b = (xs2_ref[...].astype(F32) * wb_e).astype(BF16)
    lane = lax.broadcasted_iota(jnp.int32, (CHUNK, LANES), 1)
    lo = lane < HEADDIM
    gw = HEADS_PER_GROUP * HEADDIM
    tn = (((0,), (0,)), ((), ()))
    for g in range(N_GROUPS):
        gs = slice(g * gw, (g + 1) * gw)
        ns = slice(g * D_STATE, (g + 1) * D_STATE)
        bg, cg = b_ref[:, ns], c_ref[:, ns]
        cbm = lax.dot_general(cg, bg, (((1,), (1,)), ((), ())), preferred_element_type=F32)
        prev_f = statef_ref[g]
        y_off = jnp.dot(cg, prev_f.astype(BF16), preferred_element_type=F32) * ecf_e[:, gs]
        statef_ref[g] = prev_f * decf_e[:, gs] + lax.dot_general(bg, xd_f[:, gs], tn, preferred_element_type=F32)
        prev_b = stateb_ref[g]
        y2_ref[:, gs] = (jnp.dot(c2_ref[:, ns], prev_b.astype(BF16), preferred_element_type=F32)
                         * ecb_e[:, gs]).astype(BF16)
        stateb_ref[g] = prev_b * decb_e[:, gs] + lax.dot_general(b2_ref[:, ns], xd_b[:, gs], tn,
                                                                 preferred_element_type=F32)
        for q in range(HEADS_PER_GROUP // 2):
            ms = []
            for hh in range(2):
                h = g * HEADS_PER_GROUP + 2 * q + hh
                seg = jnp.where(lower, cf[:, h:h + 1] - cf_t[h:h + 1, :], cb[:, h:h + 1] - cb_t[h:h + 1, :])
                wgt = jnp.exp(seg) * jnp.where(lower, dtf_t[h:h + 1, :], dtb_t[h:h + 1, :])
                wgt = wgt + jnp.where(diag, dtb_t[h:h + 1, :], 0.0)
                ms.append((wgt * cbm).astype(BF16))
            lhs = jnp.concatenate(ms, axis=1)
            c0 = g * gw + 2 * q * HEADDIM
            xp = xs[:, c0:c0 + LANES]
            zero = jnp.zeros_like(xp)
            rhs = jnp.concatenate([jnp.where(lo, xp, zero), jnp.where(lo, zero, xp)], axis=0)
            y = (jnp.dot(lhs, rhs, preferred_element_type=F32) + y_off[:, 2 * q * HEADDIM:2 * q * HEADDIM + LANES]
                 + dskip_ref[:, c0:c0 + LANES] * xp.astype(F32))
            y1_ref[:, c0:c0 + LANES] = y.astype(BF16)


def _ssd_scan(xbc, dt_f, dt_b, p, seq):
    t = xbc.shape[0]
    nc = seq // CHUNK
    near = lambda col: (lambda b, c: (b * nc + c, col))
    far = lambda col: (lambda b, c: (b * nc + nc - 1 - c, col))
    const = lambda b, c: (0, 0)
    gn = N_GROUPS * D_STATE
    state = pltpu.VMEM((N_GROUPS, D_STATE, HEADS_PER_GROUP * HEADDIM), F32)
    return pl.pallas_call(
        _ssd_kernel,
        grid=(t // seq, nc),
        in_specs=[
            pl.BlockSpec((CHUNK, D_INNER), near(0)),
            pl.BlockSpec((CHUNK, gn), near(D_INNER // gn)),
            pl.BlockSpec((CHUNK, gn), near(D_INNER // gn + 1)),
            pl.BlockSpec((CHUNK, LANES), near(0)),
            pl.BlockSpec((CHUNK, LANES), near(0)),
            pl.BlockSpec((CHUNK, D_INNER), far(0)),
            pl.BlockSpec((CHUNK, gn), far(D_INNER // gn)),
            pl.BlockSpec((CHUNK, gn), far(D_INNER // gn + 1)),
            pl.BlockSpec((CHUNK, LANES), far(0)),
            pl.BlockSpec((1, LANES), const), pl.BlockSpec((1, LANES), const),
            pl.BlockSpec((1, LANES), const), pl.BlockSpec((1, LANES), const),
            pl.BlockSpec((LANES, D_INNER), const),
            pl.BlockSpec((1, D_INNER), const),
        ],
        out_specs=[pl.BlockSpec((CHUNK, D_INNER), near(0)), pl.BlockSpec((CHUNK, D_INNER), far(0))],
        out_shape=[jax.ShapeDtypeStruct((t, D_INNER), BF16), jax.ShapeDtypeStruct((t, D_INNER), BF16)],
        scratch_shapes=[state, state],
        compiler_params=_cparams(("parallel", "arbitrary")),
        name="ssd_scan",
    )(xbc, xbc, xbc, dt_f, dt_b, xbc, xbc, xbc, dt_b,
      p["dtb_f"], p["dtb_b"], p["alog_f"], p["alog_b"], p["expand"], p["d_skip"])


def _merge_kernel(y1_ref, y2_ref, z_ref, gate_ref, fm_ref, x_ref, snw_ref, wf_ref, ws_ref, wo_ref, fnw_ref,
                  wr_ref, br_ref, x1_ref, hn_ref, gates_ref, blk_ref):
    z = z_ref[...].astype(F32)
    yg = (y1_ref[...].astype(F32) + y2_ref[...].astype(F32)) * (z * jax.nn.sigmoid(z))
    ms = jnp.mean(yg * yg, axis=-1, keepdims=True)
    yn = (yg * lax.rsqrt(ms + EPS) * snw_ref[...]).astype(BF16)
    u_s = jnp.dot(yn, ws_ref[...], preferred_element_type=F32)
    u_f = jnp.dot(fm_ref[...], wf_ref[...], preferred_element_type=F32)
    gl = gate_ref[...].astype(F32)
    merged = (jax.nn.sigmoid(gl[:, :D_MODEL]) * u_f + jax.nn.sigmoid(gl[:, D_MODEL:]) * u_s).astype(BF16)
    x1 = x_ref[...] + jnp.dot(merged, wo_ref[...], preferred_element_type=F32)
    x1_ref[...] = x1
    ms1 = jnp.mean(x1 * x1, axis=-1, keepdims=True)
    hn = (x1 * lax.rsqrt(ms1 + EPS) * fnw_ref[...]).astype(BF16)
    hn_ref[...] = hn
    logits = jnp.dot(hn, wr_ref[...], preferred_element_type=F32) + br_ref[...]
    lane = lax.broadcasted_iota(jnp.int32, logits.shape, 1).astype(F32)
    tm = logits.shape[0]
    work = logits
    top = None
    denom = jnp.zeros((tm, 1), F32)
    route = jnp.zeros(logits.shape, F32)
    probs, onehots = [], []
    for k in range(TOP_K):
        m = jnp.max(work, axis=-1, keepdims=True)
        if k == 0:
            top = m
        first = jnp.min(jnp.where(work == m, lane, float(LANES)), axis=-1, keepdims=True)
        sel = lane == first
        onehots.append(jnp.where(sel, 1.0, 0.0))
        work = jnp.where(sel, NEG_BIG * 2, work)
        pk = jnp.exp(m - top)
        denom = denom + pk
        probs.append(pk)
        route = jnp.where(lane == float(k), first, route)
    inv = 1.0 / denom
    for k in range(TOP_K):
        route = jnp.where(lane == float(TOP_K + k), probs[k] * inv, route)
    ti = lax.broadcasted_iota(jnp.int32, (tm, tm), 0)
    tj = lax.broadcasted_iota(jnp.int32, (tm, tm), 1)
    earlier = jnp.where(tj < ti, 1.0, 0.0).astype(BF16)
    prefix = jnp.dot(earlier, jnp.concatenate(onehots, axis=1).astype(BF16), preferred_element_type=F32)
    cnts = [prefix[tm - 1:tm, k * LANES:(k + 1) * LANES] + onehots[k][tm - 1:tm, :] for k in range(TOP_K)]
    pieces = jnp.floor((cnts[0] + cnts[1] + cnts[2] + cnts[3] + (PIECE - 1.0)) * (1.0 / PIECE))
    ei = lax.broadcasted_iota(jnp.int32, (LANES, LANES), 0)
    ej = lax.broadcasted_iota(jnp.int32, (LANES, LANES), 1)
    lower_experts = jnp.where(ei < ej, 1.0, 0.0).astype(BF16)
    seg_start = jnp.dot(jnp.broadcast_to(pieces, (8, LANES)).astype(BF16), lower_experts,
                        preferred_element_type=F32)[0:1, :]
    base = seg_start * float(PIECE)
    for k in range(TOP_K):
        row_of = base + prefix[:, k * LANES:(k + 1) * LANES]
        pos = jnp.sum(onehots[k] * row_of, axis=-1, keepdims=True)
        route = jnp.where(lane == float(2 * TOP_K + k), pos, route)
        base = base + cnts[k]
    gates_ref[...] = route
    blk_ref[0] = jnp.concatenate([pieces, seg_start, jnp.zeros((6, LANES), F32)], axis=0)


def _merge_route(y1, y2, proj, f_mix, x, ssm_norm, w_fourier, w_ssm_out, w_out, norm_ffn, w_router, b_router):
    t = x.shape[0]
    tm = TOK_BLOCK
    assert t % tm == 0
    full = lambda r, c: pl.BlockSpec((r, c), lambda i: (0, 0))
    return pl.pallas_call(
        _merge_kernel,
        grid=(t // tm,),
        in_specs=[
            pl.BlockSpec((tm, D_INNER), lambda i: (i, 0)),
            pl.BlockSpec((tm, D_INNER), lambda i: (i, 0)),
            pl.BlockSpec((tm, D_INNER), lambda i: (i, COL_Z // D_INNER)),
            pl.BlockSpec((tm, 2 * D_MODEL), lambda i: (i, COL_GATE // (2 * D_MODEL))),
            pl.BlockSpec((tm, D_F), lambda i: (i, 0)),
            pl.BlockSpec((tm, D_MODEL), lambda i: (i, 0)),
            full(1, D_INNER), full(D_F, D_MODEL), full(D_INNER, D_MODEL), full(D_MODEL, D_MODEL),
            full(1, D_MODEL), full(D_MODEL, LANES), full(1, LANES),
        ],
        out_specs=[
            pl.BlockSpec((tm, D_MODEL), lambda i: (i, 0)),
            pl.BlockSpec((tm, D_MODEL), lambda i: (i, 0)),
            pl.BlockSpec((tm, LANES), lambda i: (i, 0)),
            pl.BlockSpec((1, 8, LANES), lambda i: (i, 0, 0)),
        ],
        out_shape=[
            jax.ShapeDtypeStruct((t, D_MODEL), F32),
            jax.ShapeDtypeStruct((t, D_MODEL), BF16),
            jax.ShapeDtypeStruct((t, LANES), F32),
            jax.ShapeDtypeStruct((t // tm, 8, LANES), F32),
        ],
        compiler_params=_cparams(("parallel",)),
        name="merge_route",
    )(y1, y2, proj, proj, f_mix, x, ssm_norm, w_fourier, w_ssm_out, w_out, norm_ffn, w_router, b_router)


def _route_tables(route, blk, t):
    i32 = jnp.int32
    nb = t // TOK_BLOCK
    pcs = blk[:, 0, :N_EXPERTS].astype(i32)
    seg_start = blk[:, 1, :N_EXPERTS].astype(i32)
    seg_end = seg_start + pcs
    used = seg_end[:, -1]
    per_e = jnp.sum(pcs, axis=0)
    g_end = jnp.cumsum(per_e)
    g_start = g_end - per_e
    b_prefix = jnp.cumsum(pcs, axis=0) - pcs
    j = jnp.arange(PIECES_PER_BLOCK, dtype=i32)[None, :, None]
    in_seg = jnp.logical_and(j >= seg_start[:, None, :], j < seg_end[:, None, :])
    shift = (g_start[None, :] + b_prefix - seg_start)[:, None, :]
    j2 = j[:, :, 0]
    dst_used = jnp.sum(jnp.where(in_seg, shift, 0), axis=2) + j2
    free = PIECES_PER_BLOCK - used
    dst_unused = g_end[-1] + (jnp.cumsum(free) - free)[:, None] + j2 - used[:, None]
    is_used = j2 < used[:, None]
    dst = jnp.where(is_used, dst_used, dst_unused).reshape(-1)
    src = jnp.where(is_used, dst_used, 0).reshape(-1)
    n_tiles = nb * ROWS_PER_BLOCK // FFN_TM
    n_items_max = n_tiles + N_EXPERTS
    start_rows, end_rows = g_start * PIECE, g_end * PIECE
    t0 = start_rows // FFN_TM
    n_it = jnp.where(end_rows > start_rows, (end_rows - 1) // FFN_TM - t0 + 1, 0)
    it_end = jnp.cumsum(n_it)
    it_start = it_end - n_it
    n_exp = it_end[-1]
    tiles_used = (end_rows[-1] + FFN_TM - 1) // FFN_TM
    q = jnp.arange(n_items_max, dtype=i32)
    qc = jnp.minimum(q, n_exp - 1)[:, None]
    mine = jnp.logical_and(qc >= it_start[None, :], qc < it_end[None, :])
    pick = lambda v: jnp.sum(jnp.where(mine, v[None, :], 0), axis=1)
    e_q = pick(jnp.arange(N_EXPERTS, dtype=i32))
    tile_exp = pick(t0 - it_start) + qc[:, 0]
    tile_q = jnp.where(q < n_exp, tile_exp, jnp.minimum(tiles_used + q - n_exp, n_tiles - 1))
    lo = jnp.maximum(pick(start_rows), tile_q * FFN_TM) - tile_q * FFN_TM
    hi = jnp.minimum(pick(end_rows), (tile_q + 1) * FFN_TM) - tile_q * FFN_TM
    new_tile = jnp.concatenate([jnp.ones((1,), i32), (tile_q[1:] != tile_q[:-1]).astype(i32)])
    kind = jnp.where(q < n_exp, jnp.where(new_tile == 1, ITEM_WRITE, ITEM_MERGE),
                     jnp.where(new_tile == 1, ITEM_ZERO, ITEM_NONE))
    pos_t = (route[:, 2 * TOP_K:3 * TOP_K].astype(i32).reshape(nb, TOK_BLOCK, TOP_K).transpose(0, 2, 1))
    return dict(pos_t=pos_t, dst=dst, src=src,
                item_e=e_q, item_tile=tile_q, item_lo=lo, item_hi=hi, item_kind=kind)


def _piece_copy_out(buf_ref, hbm_ref, sem, slot, j, dst_piece):
    return pltpu.make_async_copy(buf_ref.at[slot, pl.ds(pl.multiple_of(j * PIECE, PIECE), PIECE)],
                                 hbm_ref.at[pl.ds(pl.multiple_of(dst_piece * PIECE, PIECE), PIECE)],
                                 sem.at[slot])


def _piece_copy_in(hbm_ref, buf_ref, sem, slot, j, src_piece):
    return pltpu.make_async_copy(hbm_ref.at[pl.ds(pl.multiple_of(src_piece * PIECE, PIECE), PIECE)],
                                 buf_ref.at[slot, pl.ds(pl.multiple_of(j * PIECE, PIECE), PIECE)],
                                 sem.at[slot])


def _dispatch_kernel(dst_ref, hn_ref, post_ref, xs_hbm, buf_ref, sem, *, nb):
    b = pl.program_id(0)
    slot = b % 2

    def wait_slot(s):
        pltpu.make_async_copy(buf_ref.at[s], xs_hbm.at[pl.ds(0, ROWS_PER_BLOCK)], sem.at[s]).wait()

    @pl.when(b >= 2)
    def _():
        wait_slot(slot)

    r = lax.broadcasted_iota(jnp.int32, (ROWS_PER_BLOCK, TOK_BLOCK), 0)
    onehot = jnp.zeros((ROWS_PER_BLOCK, TOK_BLOCK), F32)
    for k in range(TOP_K):
        onehot = jnp.where(r == post_ref[0, k:k + 1, :], 1.0, onehot)
    buf_ref[slot] = jnp.dot(onehot.astype(BF16), hn_ref[...], preferred_element_type=F32).astype(BF16)

    def start(j, c):
        _piece_copy_out(buf_ref, xs_hbm, sem, slot, j, dst_ref[b * PIECES_PER_BLOCK + j]).start()
        return c
    lax.fori_loop(0, PIECES_PER_BLOCK, start, 0, unroll=DMA_UNROLL)

    @pl.when(b == nb - 1)
    def _():
        wait_slot(slot)
        if nb >= 2:
            wait_slot(1 - slot)


def _ffn_kernel(e_ref, tile_ref, lo_ref, hi_ref, kind_ref, x_ref, w1_ref, b1_ref, w2_ref, b2_ref, o_ref):
    q = pl.program_id(0)
    lo, hi, kind = lo_ref[q], hi_ref[q], kind_ref[q]

    def ffn():
        hu = jnp.dot(x_ref[...], w1_ref[0], preferred_element_type=F32) + b1_ref[0]
        glu = jnp.minimum(hu[:, :D_FF], SWIGLU_LIMIT)
        lin = jnp.clip(hu[:, D_FF:], -SWIGLU_LIMIT, SWIGLU_LIMIT)
        act = (glu * jax.nn.sigmoid(SWIGLU_ALPHA * glu) * (lin + 1.0)).astype(BF16)
        return (jnp.dot(act, w2_ref[0], preferred_element_type=F32) + b2_ref[0]).astype(BF16)

    @pl.when(kind == ITEM_WRITE)
    def _():
        o_ref[...] = ffn()

    @pl.when(kind == ITEM_MERGE)
    def _():
        row = lax.broadcasted_iota(jnp.int32, (FFN_TM, 1), 0)
        mine = jnp.logical_and(row >= lo, row < hi)
        o_ref[...] = jnp.where(mine, ffn(), o_ref[...])

    @pl.when(kind == ITEM_ZERO)
    def _():
        o_ref[...] = jnp.zeros(o_ref.shape, BF16)


def _combine_kernel(src_ref, route_ref, x1_ref, nf_ref, os_hbm, o_ref, buf_ref, sem, *, nb):
    b = pl.program_id(0)
    slot = b % 2

    def fetch(bb, s):
        def body(j, c):
            _piece_copy_in(os_hbm, buf_ref, sem, s, j, src_ref[bb * PIECES_PER_BLOCK + j]).start()
            return c
        lax.fori_loop(0, PIECES_PER_BLOCK, body, 0, unroll=DMA_UNROLL)

    @pl.when(b == 0)
    def _():
        fetch(0, 0)

    @pl.when(b + 1 < nb)
    def _():
        fetch(b + 1, 1 - slot)

    pltpu.make_async_copy(os_hbm.at[pl.ds(0, ROWS_PER_BLOCK)], buf_ref.at[slot], sem.at[slot]).wait()

    r = lax.broadcasted_iota(jnp.int32, (TOK_BLOCK, ROWS_PER_BLOCK), 1).astype(F32)
    route = route_ref[...]
    wmat = jnp.zeros((TOK_BLOCK, ROWS_PER_BLOCK), F32)
    for k in range(TOP_K):
        wmat = jnp.where(r == route[:, 2 * TOP_K + k:2 * TOP_K + k + 1], route[:, TOP_K + k:TOP_K + k + 1], wmat)
    y = jnp.dot(wmat.astype(BF16), buf_ref[slot], preferred_element_type=F32)
    x2 = x1_ref[...] + y
    ms = jnp.mean(x2 * x2, axis=-1, keepdims=True)
    o_ref[...] = x2 * lax.rsqrt(ms + EPS) * nf_ref[...]


def _moe_routed(hn, route, blk, x1, w1, b1, w2, b2, norm_final):
    t = hn.shape[0]
    assert t % TOK_BLOCK == 0 and ROWS_PER_BLOCK % FFN_TM == 0
    nb = t // TOK_BLOCK
    rows = nb * ROWS_PER_BLOCK
    tb = _route_tables(route, blk, t)
    sorted_x = pl.pallas_call(
        functools.partial(_dispatch_kernel, nb=nb),
        grid_spec=pltpu.PrefetchScalarGridSpec(
            num_scalar_prefetch=1,
            grid=(nb,),
            in_specs=[
                pl.BlockSpec((TOK_BLOCK, D_MODEL), lambda b, d: (b, 0)),
                pl.BlockSpec((1, TOP_K, TOK_BLOCK), lambda b, d: (b, 0, 0)),
            ],
            out_specs=pl.BlockSpec(memory_space=pl.ANY),
            scratch_shapes=[pltpu.VMEM((2, ROWS_PER_BLOCK, D_MODEL), BF16), pltpu.SemaphoreType.DMA((2,))],
        ),
        out_shape=jax.ShapeDtypeStruct((rows, D_MODEL), BF16),
        compiler_params=_cparams(("arbitrary",)),
        name="moe_dispatch",
    )(tb["dst"], hn, tb["pos_t"])
    n_items = rows // FFN_TM + N_EXPERTS
    sorted_o = pl.pallas_call(
        _ffn_kernel,
        grid_spec=pltpu.PrefetchScalarGridSpec(
            num_scalar_prefetch=5,
            grid=(n_items,),
            in_specs=[
                pl.BlockSpec((FFN_TM, D_MODEL), lambda q, e, ti, lo, hi, fi: (ti[q], 0)),
                pl.BlockSpec((1, D_MODEL, 2 * D_FF), lambda q, e, ti, lo, hi, fi: (e[q], 0, 0)),
                pl.BlockSpec((1, 1, 2 * D_FF), lambda q, e, ti, lo, hi, fi: (e[q], 0, 0)),
                pl.BlockSpec((1, D_FF, D_MODEL), lambda q, e, ti, lo, hi, fi: (e[q], 0, 0)),
                pl.BlockSpec((1, 1, D_MODEL), lambda q, e, ti, lo, hi, fi: (e[q], 0, 0)),
            ],
            out_specs=pl.BlockSpec((FFN_TM, D_MODEL), lambda q, e, ti, lo, hi, fi: (ti[q], 0)),
        ),
        out_shape=jax.ShapeDtypeStruct((rows, D_MODEL), BF16),
        compiler_params=_cparams(("arbitrary",)),
        name="moe_ffn",
    )(tb["item_e"], tb["item_tile"], tb["item_lo"], tb["item_hi"], tb["item_kind"], sorted_x, w1, b1, w2, b2)
    return pl.pallas_call(
        functools.partial(_combine_kernel, nb=nb),
        grid_spec=pltpu.PrefetchScalarGridSpec(
            num_scalar_prefetch=1,
            grid=(nb,),
            in_specs=[
                pl.BlockSpec((TOK_BLOCK, LANES), lambda b, s: (b, 0)),
                pl.BlockSpec((TOK_BLOCK, D_MODEL), lambda b, s: (b, 0)),
                pl.BlockSpec((1, D_MODEL), lambda b, s: (0, 0)),
                pl.BlockSpec(memory_space=pl.ANY),
            ],
            out_specs=pl.BlockSpec((TOK_BLOCK, D_MODEL), lambda b, s: (b, 0)),
            scratch_shapes=[pltpu.VMEM((2, ROWS_PER_BLOCK, D_MODEL), BF16), pltpu.SemaphoreType.DMA((2,))],
        ),
        out_shape=jax.ShapeDtypeStruct((t, D_MODEL), F32),
        compiler_params=_cparams(("arbitrary",)),
        name="moe_combine",
    )(tb["src"], route, x1, norm_final, sorted_o)


def _pad_lanes(v, fill=0.0):
    v = v.reshape(1, -1).astype(F32)
    return jnp.pad(v, ((0, 0), (0, LANES - v.shape[1])), constant_values=fill)


def _stream(x3, p):
    bsz, seq, _ = x3.shape
    x = x3.reshape(bsz * seq, D_MODEL)
    proj, dt_f, dt_b = _inproj(x, p["norm_mix"], p["w_main"], p["w_dtf"], p["w_dtb"])
    f_in = proj[:, COL_F:].reshape(bsz, seq, D_F)
    f_mix = _fourier_mix(f_in, _dft_tables(seq)).reshape(bsz * seq, D_F)
    xbc = _conv_silu(proj, p["conv_w"], p["conv_b"], seq)
    y1, y2 = _ssd_scan(xbc, dt_f, dt_b, p, seq)
    x1, hn, route, blk = _merge_route(y1, y2, proj, f_mix, x, p["ssm_norm"], p["w_fourier"], p["w_ssm_out"],
                                 p["w_out"], p["norm_ffn"], p["w_router"], p["b_router"])
    out = _moe_routed(hn, route, blk, x1, p["w1"], p["b1"], p["w2"], p["b2"], p["norm_final"])
    return out.reshape(bsz, seq, D_MODEL)


def kernel(x_prompt, x_sample, norm_mix, w_in, conv_w, conv_b, dt_bias_fwd, dt_bias_bwd, a_log_fwd, a_log_bwd, d_skip, ssm_norm, w_fourier, w_ssm_out, w_out, norm_ffn, w_router, b_router, w_gate_up, b_gate_up, w_down, b_down, norm_final):
    assert norm_mix.shape[0] == 1, "single-layer block"
    w = w_in[0]
    o_z, o_xbc, o_dt, o_gate = D_F, D_F + D_INNER, D_F + D_INNER + CONV_DIM, D_F + D_INNER + CONV_DIM + 2 * N_HEADS
    w_main = jnp.concatenate([w[:, o_z:o_xbc], w[:, o_xbc:o_dt], w[:, o_gate:], w[:, :D_F]], axis=1).astype(BF16)
    pad_dt = lambda m: jnp.pad(m, ((0, 0), (0, LANES - N_HEADS))).astype(BF16)
    head_of_chan = jnp.arange(D_INNER, dtype=jnp.int32) // HEADDIM
    expand = (jnp.arange(LANES, dtype=jnp.int32)[:, None] == head_of_chan[None, :]).astype(BF16)
    p = dict(
        norm_mix=norm_mix[0].reshape(1, D_MODEL),
        w_main=w_main,
        w_dtf=pad_dt(w[:, o_dt:o_dt + N_HEADS]),
        w_dtb=pad_dt(w[:, o_dt + N_HEADS:o_gate]),
        conv_w=conv_w[0], conv_b=conv_b[0].reshape(1, CONV_DIM),
        dtb_f=_pad_lanes(dt_bias_fwd[0]), dtb_b=_pad_lanes(dt_bias_bwd[0]),
        alog_f=_pad_lanes(a_log_fwd[0], NEG_BIG), alog_b=_pad_lanes(a_log_bwd[0], NEG_BIG),
        expand=expand,
        d_skip=jnp.repeat(d_skip[0].astype(F32), HEADDIM).reshape(1, D_INNER),
        ssm_norm=ssm_norm[0].reshape(1, D_INNER),
        w_fourier=w_fourier[0].astype(BF16), w_ssm_out=w_ssm_out[0].astype(BF16), w_out=w_out[0].astype(BF16),
        norm_ffn=norm_ffn[0].reshape(1, D_MODEL),
        w_router=jnp.pad(w_router[0], ((0, 0), (0, LANES - N_EXPERTS))).astype(BF16),
        b_router=_pad_lanes(b_router[0], NEG_BIG),
        w1=w_gate_up[0].astype(BF16), b1=b_gate_up[0].reshape(N_EXPERTS, 1, 2 * D_FF),
        w2=w_down[0].astype(BF16), b2=b_down[0].reshape(N_EXPERTS, 1, D_MODEL),
        norm_final=norm_final.reshape(1, D_MODEL),
    )
    return (_stream(x_prompt, p), _stream(x_sample, p))
```

```python
import functools
import math

import jax
import jax.numpy as jnp
import numpy as np
from jax import lax
from jax.experimental import pallas as pl
from jax.experimental.pallas import tpu as pltpu

F32 = jnp.float32
BF16 = jnp.bfloat16

D_MODEL = 1024
D_F = 1024
FGROUP = 256
D_INNER = 2048
HEADDIM = 64
N_HEADS = 32
N_GROUPS = 8
HEADS_PER_GROUP = N_HEADS // N_GROUPS
D_STATE = 128
D_CONV = 5
CHUNK = 128
CONV_DIM = D_INNER + 2 * N_GROUPS * D_STATE
N_EXPERTS = 32
TOP_K = 4
D_FF = 1024
SWIGLU_ALPHA = 1.702
SWIGLU_LIMIT = 7.0
EPS = 1e-5
NEG_BIG = -1e30

LANES = 128
HALO = 16
CONV_SUB = 128
MERGE_CHUNK = 512
VMEM_LIMIT = 56 * 1024 * 1024

TOK_BLOCK = 512
PIECE = 16
FFN_TM = 512
_WORST_PIECES = (TOK_BLOCK * TOP_K + N_EXPERTS * (PIECE - 1) + PIECE - 1) // PIECE
_PIECES_PER_TILE = FFN_TM // PIECE
PIECES_PER_BLOCK = (_WORST_PIECES + _PIECES_PER_TILE - 1) // _PIECES_PER_TILE * _PIECES_PER_TILE
ROWS_PER_BLOCK = PIECES_PER_BLOCK * PIECE
DMA_UNROLL = 8
ITEM_NONE, ITEM_WRITE, ITEM_MERGE, ITEM_ZERO = 0, 1, 2, 3

COL_Z = 0
COL_XBC = D_INNER
COL_GATE = COL_XBC + CONV_DIM
COL_F = COL_GATE + 2 * D_MODEL
PROJ_MAIN = COL_F + D_F


def _cparams(sem):
    return pltpu.CompilerParams(dimension_semantics=sem, vmem_limit_bytes=VMEM_LIMIT)


def _inproj_kernel(x_ref, nw_ref, w_ref, wdtf_ref, wdtb_ref, proj_ref, dtf_ref, dtb_ref, hn_ref):
    @pl.when(pl.program_id(1) == 0)
    def _():
        x = x_ref[...]
        ms = jnp.mean(x * x, axis=-1, keepdims=True)
        hn = (x * lax.rsqrt(ms + EPS) * nw_ref[...]).astype(BF16)
        hn_ref[...] = hn
        dtf_ref[...] = jnp.dot(hn, wdtf_ref[...], preferred_element_type=F32)
        dtb_ref[...] = jnp.dot(hn, wdtb_ref[...], preferred_element_type=F32)

    proj_ref[...] = jnp.dot(hn_ref[...], w_ref[...], preferred_element_type=F32).astype(BF16)


def _inproj(x, norm_w, w_main, w_dtf, w_dtb, tm=2048, tn=1024):
    t = x.shape[0]
    tm = min(tm, t)
    grid = (t // tm, PROJ_MAIN // tn)
    return pl.pallas_call(
        _inproj_kernel,
        grid=grid,
        in_specs=[
            pl.BlockSpec((tm, D_MODEL), lambda i, j: (i, 0)),
            pl.BlockSpec((1, D_MODEL), lambda i, j: (0, 0)),
            pl.BlockSpec((D_MODEL, tn), lambda i, j: (0, j)),
            pl.BlockSpec((D_MODEL, LANES), lambda i, j: (0, 0)),
            pl.BlockSpec((D_MODEL, LANES), lambda i, j: (0, 0)),
        ],
        out_specs=[
            pl.BlockSpec((tm, tn), lambda i, j: (i, j)),
            pl.BlockSpec((tm, LANES), lambda i, j: (i, 0)),
            pl.BlockSpec((tm, LANES), lambda i, j: (i, 0)),
        ],
        out_shape=[
            jax.ShapeDtypeStruct((t, PROJ_MAIN), BF16),
            jax.ShapeDtypeStruct((t, LANES), F32),
            jax.ShapeDtypeStruct((t, LANES), F32),
        ],
        scratch_shapes=[pltpu.VMEM((tm, D_MODEL), BF16)],
        compiler_params=_cparams(("parallel", "arbitrary")),
        name="inproj",
    )(x, norm_w, w_main, w_dtf, w_dtb)


def _dft_factors(seq):
    n1 = 1 << ((int(math.log2(seq)) + 1) // 2)
    return n1, seq // n1


def _dft_tables(seq):
    n1, n2 = _dft_factors(seq)
    two_pi = 2.0 * math.pi
    c = jnp.arange(FGROUP, dtype=jnp.int32)
    ph = ((c[:, None] * c[None, :]) % FGROUP).astype(F32) * (two_pi / FGROUP)
    cs_chan = jnp.concatenate([jnp.cos(ph), -jnp.sin(ph)], axis=1).astype(BF16)
    k1 = jnp.arange(n1, dtype=jnp.int32)
    nn = (n2 * jnp.arange(n1, dtype=jnp.int32)[None, :] + jnp.arange(n2, dtype=jnp.int32)[:, None])
    al = ((k1[None, :, None] * nn[:, None, :]) % seq).astype(F32) * (two_pi / seq)
    ca, sa = jnp.cos(al), jnp.sin(al)
    g1 = jnp.concatenate([jnp.concatenate([ca, sa], axis=2),
                          jnp.concatenate([-sa, ca], axis=2)], axis=1).astype(BF16)
    k2 = jnp.arange(n2, dtype=jnp.int32)
    be = ((k2[:, None] * k2[None, :]) % n2).astype(F32) * (two_pi / n2)
    g2 = jnp.concatenate([jnp.cos(be), jnp.sin(be)], axis=1).astype(BF16)
    return cs_chan, g1, g2


def _dft1_kernel(x_ref, cs_ref, g_ref, o_ref, *, tn2, n1):
    for j in range(tn2):
        x = x_ref[0, j]
        parts = []
        for q in range(D_F // FGROUP):
            uv = jnp.dot(x[:, q * FGROUP:(q + 1) * FGROUP], cs_ref[...],
                         preferred_element_type=F32).astype(BF16)
            parts.append(jnp.concatenate([uv[:, :FGROUP], uv[:, FGROUP:]], axis=0))
        z = jnp.concatenate(parts, axis=1)
        o_ref[0, j] = jnp.dot(g_ref[j], z, preferred_element_type=F32).astype(BF16)


def _dft2_kernel(a_ref, g_ref, o_ref, *, tk1, scale):
    for j in range(tk1):
        o_ref[0, j] = (jnp.dot(g_ref[...], a_ref[0, j], preferred_element_type=F32) * scale).astype(BF16)


def _fourier_mix(f_in, tables):
    bsz, seq, _ = f_in.shape
    n1, n2 = _dft_factors(seq)
    cs_chan, g1, g2 = tables
    xt = f_in.reshape(bsz, n1, n2, D_F).transpose(0, 2, 1, 3)
    tn2 = 4
    stage1 = pl.pallas_call(
        functools.partial(_dft1_kernel, tn2=tn2, n1=n1),
        grid=(bsz, n2 // tn2),
        in_specs=[
            pl.BlockSpec((1, tn2, n1, D_F), lambda b, i: (b, i, 0, 0)),
            pl.BlockSpec((FGROUP, 2 * FGROUP), lambda b, i: (0, 0)),
            pl.BlockSpec((tn2, 2 * n1, 2 * n1), lambda b, i: (i, 0, 0)),
        ],
        out_specs=pl.BlockSpec((1, tn2, 2 * n1, D_F), lambda b, i: (b, i, 0, 0)),
        out_shape=jax.ShapeDtypeStruct((bsz, n2, 2 * n1, D_F), BF16),
        compiler_params=_cparams(("parallel", "parallel")),
        name="dft_stage1",
    )(xt, cs_chan, g1)
    a2 = (stage1.reshape(bsz, n2, 2, n1, D_F).transpose(0, 3, 2, 1, 4)
          .reshape(bsz, n1, 2 * n2, D_F))
    tk1 = 4
    scale = 1.0 / math.sqrt(seq * FGROUP)
    stage2 = pl.pallas_call(
        functools.partial(_dft2_kernel, tk1=tk1, scale=scale),
        grid=(bsz, n1 // tk1),
        in_specs=[
            pl.BlockSpec((1, tk1, 2 * n2, D_F), lambda b, i: (b, i, 0, 0)),
            pl.BlockSpec((n2, 2 * n2), lambda b, i: (0, 0)),
        ],
        out_specs=pl.BlockSpec((1, tk1, n2, D_F), lambda b, i: (b, i, 0, 0)),
        out_shape=jax.ShapeDtypeStruct((bsz, n1, n2, D_F), BF16),
        compiler_params=_cparams(("parallel", "parallel")),
        name="dft_stage2",
    )(a2, g2)
    return stage2.transpose(0, 2, 1, 3).reshape(bsz, seq, D_F)


def _conv_kernel(prev_ref, main_ref, next_ref, w_ref, b_ref, o_ref, *, tl, tiles_per_seq):
    i = pl.program_id(0) % tiles_per_seq
    halo_zero = jnp.zeros(prev_ref.shape, BF16)
    prev = jnp.where(i == 0, halo_zero, prev_ref[...])
    nxt = jnp.where(i == tiles_per_seq - 1, halo_zero, next_ref[...])
    full = jnp.concatenate([prev, main_ref[...], nxt], axis=0)
    pad = D_CONV // 2
    win = CONV_SUB + 2 * HALO
    r = lax.broadcasted_iota(jnp.int32, (CONV_SUB, win), 0)
    c = lax.broadcasted_iota(jnp.int32, (CONV_SUB, win), 1)
    taps = [k for k in range(D_CONV) if k != pad]
    shift = jnp.concatenate([jnp.where(c == r + (HALO + k - pad), 1.0, 0.0) for k in taps], axis=1).astype(BF16)
    w_bf = w_ref[...].astype(BF16)
    for j in range(tl // CONV_SUB):
        window = full[j * CONV_SUB:j * CONV_SUB + win]
        stacked = jnp.concatenate([window * w_bf[k:k + 1, :] for k in taps], axis=0)
        acc = (b_ref[...] + w_ref[pad:pad + 1, :] * window[HALO:HALO + CONV_SUB].astype(F32)
               + jnp.dot(shift, stacked, preferred_element_type=F32))
        o_ref[j * CONV_SUB:(j + 1) * CONV_SUB, :] = (acc * jax.nn.sigmoid(acc)).astype(BF16)


def _conv_silu(proj, conv_w, conv_b, seq, tl=512, tc=2048):
    t = proj.shape[0]
    tl = min(tl, seq)
    tiles_per_seq = seq // tl
    cb0 = COL_XBC // tc
    hb = tl // HALO
    last_hb = t // HALO - 1
    return pl.pallas_call(
        functools.partial(_conv_kernel, tl=tl, tiles_per_seq=tiles_per_seq),
        grid=(t // tl, CONV_DIM // tc),
        in_specs=[
            pl.BlockSpec((HALO, tc), lambda i, c: (jnp.maximum(i * hb - 1, 0), cb0 + c)),
            pl.BlockSpec((tl, tc), lambda i, c: (i, cb0 + c)),
            pl.BlockSpec((HALO, tc), lambda i, c: (jnp.minimum((i + 1) * hb, last_hb), cb0 + c)),
            pl.BlockSpec((D_CONV, tc), lambda i, c: (0, c)),
            pl.BlockSpec((1, tc), lambda i, c: (0, c)),
        ],
        out_specs=pl.BlockSpec((tl, tc), lambda i, c: (i, c)),
        out_shape=jax.ShapeDtypeStruct((t, CONV_DIM), BF16),
        compiler_params=_cparams(("parallel", "parallel")),
        name="conv_silu",
    )(proj, proj, proj, conv_w, conv_b)


def _split3(v):
    hi = v.astype(BF16)
    r1 = v - hi.astype(F32)
    mid = r1.astype(BF16)
    lo = (r1 - mid.astype(F32)).astype(BF16)
    return [hi, mid, lo]


def _masked_sums(tri, vals):
    parts = []
    for v in vals:
        parts += _split3(v)
    out = jnp.dot(tri, jnp.concatenate(parts, axis=1), preferred_element_type=F32)
    return [out[:, (3 * i) * LANES:(3 * i + 1) * LANES] + out[:, (3 * i + 1) * LANES:(3 * i + 2) * LANES]
            + out[:, (3 * i + 2) * LANES:(3 * i + 3) * LANES] for i in range(len(vals))]


def _ssd_kernel(xs_ref, b_ref, c_ref, dtf_ref, dtb_ref, xs2_ref, b2_ref, c2_ref, dtb2_ref,
                biasf_ref, biasb_ref, alogf_ref, alogb_ref, exp_ref, dskip_ref,
                y1_ref, y2_ref, statef_ref, stateb_ref):
    @pl.when(pl.program_id(1) == 0)
    def _():
        statef_ref[...] = jnp.zeros(statef_ref.shape, F32)
        stateb_ref[...] = jnp.zeros(stateb_ref.shape, F32)

    row = lax.broadcasted_iota(jnp.int32, (CHUNK, CHUNK), 0)
    col = lax.broadcasted_iota(jnp.int32, (CHUNK, CHUNK), 1)
    lower = row >= col
    diag = row == col
    tri_f = jnp.where(lower, 1.0, 0.0).astype(BF16)
    tri_b = jnp.where(row <= col, 1.0, 0.0).astype(BF16)

    af_neg = -jnp.exp(alogf_ref[...])
    ab_neg = -jnp.exp(alogb_ref[...])
    dt_f = jax.nn.softplus(dtf_ref[...] + biasf_ref[...])
    dt_b = jax.nn.softplus(dtb_ref[...] + biasb_ref[...])
    dt_b2 = jax.nn.softplus(dtb2_ref[...] + biasb_ref[...])
    (cf,) = _masked_sums(tri_f, [dt_f * af_neg])
    cb, cb2 = _masked_sums(tri_b, [dt_b * ab_neg, dt_b2 * ab_neg])
    tot_f = cf[CHUNK - 1:CHUNK, :]
    tot_b2 = cb2[0:1, :]
    cf_t, cb_t, dtf_t, dtb_t = cf.T, cb.T, dt_f.T, dt_b.T

    dec = jnp.concatenate(_split3(jnp.exp(tot_f)) + _split3(jnp.exp(tot_b2))
                          + [jnp.zeros((PIECE - 6, LANES), BF16)], axis=0)
    stack = jnp.concatenate([(dt_f * jnp.exp(tot_f - cf)).astype(BF16), jnp.exp(cf).astype(BF16),
                             (dt_b2 * jnp.exp(tot_b2 - cb2)).astype(BF16), jnp.exp(cb2).astype(BF16), dec], axis=0)
    lane = lax.broadcasted_iota(jnp.int32, (CHUNK, LANES), 1)
    lo = lane < HEADDIM
    gw = HEADS_PER_GROUP * HEADDIM
    tn = (((0,), (0,)), ((), ()))
    d0 = 4 * CHUNK
    for g in range(N_GROUPS):
        gs = slice(g * gw, (g + 1) * gw)
        ns = slice(g * D_STATE, (g + 1) * D_STATE)
        ex = jnp.dot(stack, exp_ref[:, gs], preferred_element_type=F32)
        decf_e = ex[d0:d0 + 1] + ex[d0 + 1:d0 + 2] + ex[d0 + 2:d0 + 3]
        decb_e = ex[d0 + 3:d0 + 4] + ex[d0 + 4:d0 + 5] + ex[d0 + 5:d0 + 6]
        xd_f = (xs_ref[:, gs].astype(F32) * ex[0:CHUNK]).astype(BF16)
        xd_b = (xs2_ref[:, gs].astype(F32) * ex[2 * CHUNK:3 * CHUNK]).astype(BF16)
        bg, cg = b_ref[:, ns], c_ref[:, ns]
        cbm = lax.dot_general(cg, bg, (((1,), (1,)), ((), ())), preferred_element_type=F32)
        prev_f = statef_ref[g]
        y_off = jnp.dot(cg, prev_f.astype(BF16), preferred_element_type=F32) * ex[CHUNK:2 * CHUNK]
        statef_ref[g] = prev_f * decf_e + lax.dot_general(bg, xd_f, tn, preferred_element_type=F32)
        prev_b = stateb_ref[g]
        y2_ref[:, gs] = (jnp.dot(c2_ref[:, ns], prev_b.astype(BF16), preferred_element_type=F32)
                         * ex[3 * CHUNK:4 * CHUNK]).astype(BF16)
        stateb_ref[g] = prev_b * decb_e + lax.dot_general(b2_ref[:, ns], xd_b, tn, preferred_element_type=F32)
        for q in range(HEADS_PER_GROUP // 2):
            ms = []
            for hh in range(2):
                h = g * HEADS_PER_GROUP + 2 * q + hh
                seg = jnp.where(lower, cf[:, h:h + 1] - cf_t[h:h + 1, :], cb[:, h:h + 1] - cb_t[h:h + 1, :])
                wgt = jnp.exp(seg) * jnp.where(lower, dtf_t[h:h + 1, :], dtb_t[h:h + 1, :])
                wgt = wgt + jnp.where(diag, dtb_t[h:h + 1, :], 0.0)
                ms.append((wgt * cbm).astype(BF16))
            lhs = jnp.concatenate(ms, axis=1)
            c0 = g * gw + 2 * q * HEADDIM
            xp = xs_ref[:, c0:c0 + LANES]
            zero = jnp.zeros_like(xp)
            rhs = jnp.concatenate([jnp.where(lo, xp, zero), jnp.where(lo, zero, xp)], axis=0)
            y = (jnp.dot(lhs, rhs, preferred_element_type=F32) + y_off[:, 2 * q * HEADDIM:2 * q * HEADDIM + LANES]
                 + dskip_ref[:, c0:c0 + LANES] * xp.astype(F32))
            y1_ref[:, c0:c0 + LANES] = y.astype(BF16)


def _ssd_scan(xbc, dt_f, dt_b, p, seq):
    t = xbc.shape[0]
    nc = seq // CHUNK
    near = lambda col: (lambda b, c: (b * nc + c, col))
    far = lambda col: (lambda b, c: (b * nc + nc - 1 - c, col))
    const = lambda b, c: (0, 0)
    gn = N_GROUPS * D_STATE
    state = pltpu.VMEM((N_GROUPS, D_STATE, HEADS_PER_GROUP * HEADDIM), F32)
    return pl.pallas_call(
        _ssd_kernel,
        grid=(t // seq, nc),
        in_specs=[
            pl.BlockSpec((CHUNK, D_INNER), near(0)),
            pl.BlockSpec((CHUNK, gn), near(D_INNER // gn)),
            pl.BlockSpec((CHUNK, gn), near(D_INNER // gn + 1)),
            pl.BlockSpec((CHUNK, LANES), near(0)),
            pl.BlockSpec((CHUNK, LANES), near(0)),
            pl.BlockSpec((CHUNK, D_INNER), far(0)),
            pl.BlockSpec((CHUNK, gn), far(D_INNER // gn)),
            pl.BlockSpec((CHUNK, gn), far(D_INNER // gn + 1)),
            pl.BlockSpec((CHUNK, LANES), far(0)),
            pl.BlockSpec((1, LANES), const), pl.BlockSpec((1, LANES), const),
            pl.BlockSpec((1, LANES), const), pl.BlockSpec((1, LANES), const),
            pl.BlockSpec((LANES, D_INNER), const),
            pl.BlockSpec((1, D_INNER), const),
        ],
        out_specs=[pl.BlockSpec((CHUNK, D_INNER), near(0)), pl.BlockSpec((CHUNK, D_INNER), far(0))],
        out_shape=[jax.ShapeDtypeStruct((t, D_INNER), BF16), jax.ShapeDtypeStruct((t, D_INNER), BF16)],
        scratch_shapes=[state, state],
        compiler_params=_cparams(("parallel", "arbitrary")),
        name="ssd_scan",
    )(xbc, xbc, xbc, dt_f, dt_b, xbc, xbc, xbc, dt_b,
      p["dtb_f"], p["dtb_b"], p["alog_f"], p["alog_b"], p["expand"], p["d_skip"])


def _merge_kernel(y1_ref, y2_ref, z_ref, gate_ref, fm_ref, x_ref, snw_ref, wf_ref, ws_ref, wo_ref, fnw_ref,
                  wr_ref, br_ref, x1_ref, hn_ref, gates_ref, blk_ref):
    acc = None
    ssq = None
    for c0 in range(0, D_INNER, MERGE_CHUNK):
        cs = slice(c0, c0 + MERGE_CHUNK)
        z = z_ref[:, cs].astype(F32)
        yg = (y1_ref[:, cs].astype(F32) + y2_ref[:, cs].astype(F32)) * (z * jax.nn.sigmoid(z))
        sq = jnp.sum(yg * yg, axis=-1, keepdims=True)
        part = jnp.dot((yg * snw_ref[:, cs]).astype(BF16), ws_ref[cs, :], preferred_element_type=F32)
        acc = part if acc is None else acc + part
        ssq = sq if ssq is None else ssq + sq
    u_s = acc * lax.rsqrt(ssq * (1.0 / D_INNER) + EPS)
    x1 = x_ref[...]
    fm = fm_ref[...]
    for c0 in range(0, D_MODEL, MERGE_CHUNK):
        cs = slice(c0, c0 + MERGE_CHUNK)
        u_f = jnp.dot(fm, wf_ref[:, cs], preferred_element_type=F32)
        g_f = jax.nn.sigmoid(gate_ref[:, cs].astype(F32))
        g_s = jax.nn.sigmoid(gate_ref[:, D_MODEL + c0:D_MODEL + c0 + MERGE_CHUNK].astype(F32))
        merged = (g_f * u_f + g_s * u_s[:, cs]).astype(BF16)
        x1 = x1 + jnp.dot(merged, wo_ref[cs, :], preferred_element_type=F32)
    x1_ref[...] = x1
    ms1 = jnp.mean(x1 * x1, axis=-1, keepdims=True)
    hn = (x1 * lax.rsqrt(ms1 + EPS) * fnw_ref[...]).astype(BF16)
    hn_ref[...] = hn
    logits = jnp.dot(hn, wr_ref[...], preferred_element_type=F32) + br_ref[...]
    lane = lax.broadcasted_iota(jnp.int32, logits.shape, 1).astype(F32)
    tm = logits.shape[0]
    work = logits
    top = None
    denom = jnp.zeros((tm, 1), F32)
    route = jnp.zeros(logits.shape, F32)
    probs, onehots = [], []
    for k in range(TOP_K):
        m = jnp.max(work, axis=-1, keepdims=True)
        if k == 0:
            top = m
        first = jnp.min(jnp.where(work == m, lane, float(LANES)), axis=-1, keepdims=True)
        sel = lane == first
        onehots.append(jnp.where(sel, 1.0, 0.0))
        work = jnp.where(sel, NEG_BIG * 2, work)
        pk = jnp.exp(m - top)
        denom = denom + pk
        probs.append(pk)
        route = jnp.where(lane == float(k), first, route)
    inv = 1.0 / denom
    for k in range(TOP_K):
        route = jnp.where(lane == float(TOP_K + k), probs[k] * inv, route)
    ti = lax.broadcasted_iota(jnp.int32, (tm, tm), 0)
    tj = lax.broadcasted_iota(jnp.int32, (tm, tm), 1)
    earlier = jnp.where(tj < ti, 1.0, 0.0).astype(BF16)
    prefix = jnp.dot(earlier, jnp.concatenate(onehots, axis=1).astype(BF16), preferred_element_type=F32)
    cnts = [prefix[tm - 1:tm, k * LANES:(k + 1) * LANES] + onehots[k][tm - 1:tm, :] for k in range(TOP_K)]
    pieces = jnp.floor((cnts[0] + cnts[1] + cnts[2] + cnts[3] + (PIECE - 1.0)) * (1.0 / PIECE))
    ei = lax.broadcasted_iota(jnp.int32, (LANES, LANES), 0)
    ej = lax.broadcasted_iota(jnp.int32, (LANES, LANES), 1)
    lower_experts = jnp.where(ei < ej, 1.0, 0.0).astype(BF16)
    seg_start = jnp.dot(jnp.broadcast_to(pieces, (8, LANES)).astype(BF16), lower_experts,
                        preferred_element_type=F32)[0:1, :]
    base = seg_start * float(PIECE)
    for k in range(TOP_K):
        row_of = base + prefix[:, k * LANES:(k + 1) * LANES]
        pos = jnp.sum(onehots[k] * row_of, axis=-1, keepdims=True)
        route = jnp.where(lane == float(2 * TOP_K + k), pos, route)
        base = base + cnts[k]
    gates_ref[...] = route
    blk_ref[0] = jnp.concatenate([pieces, seg_start, jnp.zeros((6, LANES), F32)], axis=0)


def _merge_route(y1, y2, proj, f_mix, x, ssm_norm, w_fourier, w_ssm_out, w_out, norm_ffn, w_router, b_router):
    t = x.shape[0]
    tm = TOK_BLOCK
    assert t % tm == 0
    full = lambda r, c: pl.BlockSpec((r, c), lambda i: (0, 0))
    return pl.pallas_call(
        _merge_kernel,
        grid=(t // tm,),
        in_specs=[
            pl.BlockSpec((tm, D_INNER), lambda i: (i, 0)),
            pl.BlockSpec((tm, D_INNER), lambda i: (i, 0)),
            pl.BlockSpec((tm, D_INNER), lambda i: (i, COL_Z // D_INNER)),
            pl.BlockSpec((tm, 2 * D_MODEL), lambda i: (i, COL_GATE // (2 * D_MODEL))),
            pl.BlockSpec((tm, D_F), lambda i: (i, 0)),
            pl.BlockSpec((tm, D_MODEL), lambda i: (i, 0)),
            full(1, D_INNER), full(D_F, D_MODEL), full(D_INNER, D_MODEL), full(D_MODEL, D_MODEL),
            full(1, D_MODEL), full(D_MODEL, LANES), full(1, LANES),
        ],
        out_specs=[
            pl.BlockSpec((tm, D_MODEL), lambda i: (i, 0)),
            pl.BlockSpec((tm, D_MODEL), lambda i: (i, 0)),
            pl.BlockSpec((tm, LANES), lambda i: (i, 0)),
            pl.BlockSpec((1, 8, LANES), lambda i: (i, 0, 0)),
        ],
        out_shape=[
            jax.ShapeDtypeStruct((t, D_MODEL), F32),
            jax.ShapeDtypeStruct((t, D_MODEL), BF16),
            jax.ShapeDtypeStruct((t, LANES), F32),
            jax.ShapeDtypeStruct((t // tm, 8, LANES), F32),
        ],
        compiler_params=_cparams(("parallel",)),
        name="merge_route",
    )(y1, y2, proj, proj, f_mix, x, ssm_norm, w_fourier, w_ssm_out, w_out, norm_ffn, w_router, b_router)


def _route_tables(route, blk, t):
    i32 = jnp.int32
    nb = t // TOK_BLOCK
    pcs = blk[:, 0, :N_EXPERTS].astype(i32)
    seg_start = blk[:, 1, :N_EXPERTS].astype(i32)
    seg_end = seg_start + pcs
    used = seg_end[:, -1]
    per_e = jnp.sum(pcs, axis=0)
    g_end = jnp.cumsum(per_e)
    g_start = g_end - per_e
    b_prefix = jnp.cumsum(pcs, axis=0) - pcs
    j = jnp.arange(PIECES_PER_BLOCK, dtype=i32)[None, :, None]
    in_seg = jnp.logical_and(j >= seg_start[:, None, :], j < seg_end[:, None, :])
    shift = (g_start[None, :] + b_prefix - seg_start)[:, None, :]
    j2 = j[:, :, 0]
    dst_used = jnp.sum(jnp.where(in_seg, shift, 0), axis=2) + j2
    free = PIECES_PER_BLOCK - used
    dst_unused = g_end[-1] + (jnp.cumsum(free) - free)[:, None] + j2 - used[:, None]
    is_used = j2 < used[:, None]
    dst = jnp.where(is_used, dst_used, dst_unused).reshape(-1)
    src = jnp.where(is_used, dst_used, 0).reshape(-1)
    n_tiles = nb * ROWS_PER_BLOCK // FFN_TM
    n_items_max = n_tiles + N_EXPERTS
    start_rows, end_rows = g_start * PIECE, g_end * PIECE
    t0 = start_rows // FFN_TM
    n_it = jnp.where(end_rows > start_rows, (end_rows - 1) // FFN_TM - t0 + 1, 0)
    it_end = jnp.cumsum(n_it)
    it_start = it_end - n_it
    n_exp = it_end[-1]
    tiles_used = (end_rows[-1] + FFN_TM - 1) // FFN_TM
    q = jnp.arange(n_items_max, dtype=i32)
    qc = jnp.minimum(q, n_exp - 1)[:, None]
    mine = jnp.logical_and(qc >= it_start[None, :], qc < it_end[None, :])
    pick = lambda v: jnp.sum(jnp.where(mine, v[None, :], 0), axis=1)
    e_q = pick(jnp.arange(N_EXPERTS, dtype=i32))
    tile_exp = pick(t0 - it_start) + qc[:, 0]
    tile_q = jnp.where(q < n_exp, tile_exp, jnp.minimum(tiles_used + q - n_exp, n_tiles - 1))
    lo = jnp.maximum(pick(start_rows), tile_q * FFN_TM) - tile_q * FFN_TM
    hi = jnp.minimum(pick(end_rows), (tile_q + 1) * FFN_TM) - tile_q * FFN_TM
    new_tile = jnp.concatenate([jnp.ones((1,), i32), (tile_q[1:] != tile_q[:-1]).astype(i32)])
    kind = jnp.where(q < n_exp, jnp.where(new_tile == 1, ITEM_WRITE, ITEM_MERGE),
                     jnp.where(new_tile == 1, ITEM_ZERO, ITEM_NONE))
    pos_t = (route[:, 2 * TOP_K:3 * TOP_K].astype(i32).reshape(nb, TOK_BLOCK, TOP_K).transpose(0, 2, 1))
    return dict(pos_t=pos_t, dst=dst, src=src,
                item_e=e_q, item_tile=tile_q, item_lo=lo, item_hi=hi, item_kind=kind)


def _piece_copy_out(buf_ref, hbm_ref, sem, slot, j, dst_piece):
    return pltpu.make_async_copy(buf_ref.at[slot, pl.ds(pl.multiple_of(j * PIECE, PIECE), PIECE)],
                                 hbm_ref.at[pl.ds(pl.multiple_of(dst_piece * PIECE, PIECE), PIECE)],
                                 sem.at[slot])


def _piece_copy_in(hbm_ref, buf_ref, sem, slot, j, src_piece):
    return pltpu.make_async_copy(hbm_ref.at[pl.ds(pl.multiple_of(src_piece * PIECE, PIECE), PIECE)],
                                 buf_ref.at[slot, pl.ds(pl.multiple_of(j * PIECE, PIECE), PIECE)],
                                 sem.at[slot])


def _dispatch_kernel(dst_ref, hn_ref, post_ref, xs_hbm, buf_ref, sem, *, nb):
    b = pl.program_id(0)
    slot = b % 2

    def wait_slot(s):
        pltpu.make_async_copy(buf_ref.at[s], xs_hbm.at[pl.ds(0, ROWS_PER_BLOCK)], sem.at[s]).wait()

    @pl.when(b >= 2)
    def _():
        wait_slot(slot)

    r = lax.broadcasted_iota(jnp.int32, (ROWS_PER_BLOCK, TOK_BLOCK), 0)
    onehot = jnp.zeros((ROWS_PER_BLOCK, TOK_BLOCK), F32)
    for k in range(TOP_K):
        onehot = jnp.where(r == post_ref[0, k:k + 1, :], 1.0, onehot)
    buf_ref[slot] = jnp.dot(onehot.astype(BF16), hn_ref[...], preferred_element_type=F32).astype(BF16)

    def start(j, c):
        _piece_copy_out(buf_ref, xs_hbm, sem, slot, j, dst_ref[b * PIECES_PER_BLOCK + j]).start()
        return c
    lax.fori_loop(0, PIECES_PER_BLOCK, start, 0, unroll=DMA_UNROLL)

    @pl.when(b == nb - 1)
    def _():
        wait_slot(slot)
        if nb >= 2:
            wait_slot(1 - slot)


def _ffn_kernel(e_ref, tile_ref, lo_ref, hi_ref, kind_ref, x_ref, w1_ref, b1_ref, w2_ref, b2_ref, o_ref):
    q = pl.program_id(0)
    lo, hi, kind = lo_ref[q], hi_ref[q], kind_ref[q]

    def ffn():
        hu = jnp.dot(x_ref[...], w1_ref[0], preferred_element_type=F32) + b1_ref[0]
        glu = jnp.minimum(hu[:, :D_FF], SWIGLU_LIMIT)
        lin = jnp.clip(hu[:, D_FF:], -SWIGLU_LIMIT, SWIGLU_LIMIT)
        act = (glu * jax.nn.sigmoid(SWIGLU_ALPHA * glu) * (lin + 1.0)).astype(BF16)
        return (jnp.dot(act, w2_ref[0], preferred_element_type=F32) + b2_ref[0]).astype(BF16)

    @pl.when(kind == ITEM_WRITE)
    def _():
        o_ref[...] = ffn()

    @pl.when(kind == ITEM_MERGE)
    def _():
        row = lax.broadcasted_iota(jnp.int32, (FFN_TM, 1), 0)
        mine = jnp.logical_and(row >= lo, row < hi)
        o_ref[...] = jnp.where(mine, ffn(), o_ref[...])

    @pl.when(kind == ITEM_ZERO)
    def _():
        o_ref[...] = jnp.zeros(o_ref.shape, BF16)


def _combine_kernel(src_ref, route_ref, x1_ref, nf_ref, os_hbm, o_ref, buf_ref, sem, *, nb):
    b = pl.program_id(0)
    slot = b % 2

    def fetch(bb, s):
        def body(j, c):
            _piece_copy_in(os_hbm, buf_ref, sem, s, j, src_ref[bb * PIECES_PER_BLOCK + j]).start()
            return c
        lax.fori_loop(0, PIECES_PER_BLOCK, body, 0, unroll=DMA_UNROLL)

    @pl.when(b == 0)
    def _():
        fetch(0, 0)

    @pl.when(b + 1 < nb)
    def _():
        fetch(b + 1, 1 - slot)

    pltpu.make_async_copy(os_hbm.at[pl.ds(0, ROWS_PER_BLOCK)], buf_ref.at[slot], sem.at[slot]).wait()

    r = lax.broadcasted_iota(jnp.int32, (TOK_BLOCK, ROWS_PER_BLOCK), 1).astype(F32)
    route = route_ref[...]
    wmat = jnp.zeros((TOK_BLOCK, ROWS_PER_BLOCK), F32)
    for k in range(TOP_K):
        wmat = jnp.where(r == route[:, 2 * TOP_K + k:2 * TOP_K + k + 1], route[:, TOP_K + k:TOP_K + k + 1], wmat)
    y = jnp.dot(wmat.astype(BF16), buf_ref[slot], preferred_element_type=F32)
    x2 = x1_ref[...] + y
    ms = jnp.mean(x2 * x2, axis=-1, keepdims=True)
    o_ref[...] = x2 * lax.rsqrt(ms + EPS) * nf_ref[...]


def _moe_routed(hn, route, blk, x1, w1, b1, w2, b2, norm_final):
    t = hn.shape[0]
    assert t % TOK_BLOCK == 0 and ROWS_PER_BLOCK % FFN_TM == 0
    nb = t // TOK_BLOCK
    rows = nb * ROWS_PER_BLOCK
    tb = _route_tables(route, blk, t)
    sorted_x = pl.pallas_call(
        functools.partial(_dispatch_kernel, nb=nb),
        grid_spec=pltpu.PrefetchScalarGridSpec(
            num_scalar_prefetch=1,
            grid=(nb,),
            in_specs=[
                pl.BlockSpec((TOK_BLOCK, D_MODEL), lambda b, d: (b, 0)),
                pl.BlockSpec((1, TOP_K, TOK_BLOCK), lambda b, d: (b, 0, 0)),
            ],
            out_specs=pl.BlockSpec(memory_space=pl.ANY),
            scratch_shapes=[pltpu.VMEM((2, ROWS_PER_BLOCK, D_MODEL), BF16), pltpu.SemaphoreType.DMA((2,))],
        ),
        out_shape=jax.ShapeDtypeStruct((rows, D_MODEL), BF16),
        compiler_params=_cparams(("arbitrary",)),
        name="moe_dispatch",
    )(tb["dst"], hn, tb["pos_t"])
    n_items = rows // FFN_TM + N_EXPERTS
    sorted_o = pl.pallas_call(
        _ffn_kernel,
        grid_spec=pltpu.PrefetchScalarGridSpec(
            num_scalar_prefetch=5,
            grid=(n_items,),
            in_specs=[
                pl.BlockSpec((FFN_TM, D_MODEL), lambda q, e, ti, lo, hi, fi: (ti[q], 0)),
                pl.BlockSpec((1, D_MODEL, 2 * D_FF), lambda q, e, ti, lo, hi, fi: (e[q], 0, 0)),
                pl.BlockSpec((1, 1, 2 * D_FF), lambda q, e, ti, lo, hi, fi: (e[q], 0, 0)),
                pl.BlockSpec((1, D_FF, D_MODEL), lambda q, e, ti, lo, hi, fi: (e[q], 0, 0)),
                pl.BlockSpec((1, 1, D_MODEL), lambda q, e, ti, lo, hi, fi: (e[q], 0, 0)),
            ],
            out_specs=pl.BlockSpec((FFN_TM, D_MODEL), lambda q, e, ti, lo, hi, fi: (ti[q], 0)),
        ),
        out_shape=jax.ShapeDtypeStruct((rows, D_MODEL), BF16),
        compiler_params=_cparams(("arbitrary",)),
        name="moe_ffn",
    )(tb["item_e"], tb["item_tile"], tb["item_lo"], tb["item_hi"], tb["item_kind"], sorted_x, w1, b1, w2, b2)
    return pl.pallas_call(
        functools.partial(_combine_kernel, nb=nb),
        grid_spec=pltpu.PrefetchScalarGridSpec(
            num_scalar_prefetch=1,
            grid=(nb,),
            in_specs=[
                pl.BlockSpec((TOK_BLOCK, LANES), lambda b, s: (b, 0)),
                pl.BlockSpec((TOK_BLOCK, D_MODEL), lambda b, s: (b, 0)),
                pl.BlockSpec((1, D_MODEL), lambda b, s: (0, 0)),
                pl.BlockSpec(memory_space=pl.ANY),
            ],
            out_specs=pl.BlockSpec((TOK_BLOCK, D_MODEL), lambda b, s: (b, 0)),
            scratch_shapes=[pltpu.VMEM((2, ROWS_PER_BLOCK, D_MODEL), BF16), pltpu.SemaphoreType.DMA((2,))],
        ),
        out_shape=jax.ShapeDtypeStruct((t, D_MODEL), F32),
        compiler_params=_cparams(("arbitrary",)),
        name="moe_combine",
    )(tb["src"], route, x1, norm_final, sorted_o)


def _pad_lanes(v, fill=0.0):
    v = v.reshape(1, -1).astype(F32)
    return jnp.pad(v, ((0, 0), (0, LANES - v.shape[1])), constant_values=fill)


def _stream(x3, p):
    bsz, seq, _ = x3.shape
    x = x3.reshape(bsz * seq, D_MODEL)
    proj, dt_f, dt_b = _inproj(x, p["norm_mix"], p["w_main"], p["w_dtf"], p["w_dtb"])
    f_in = proj[:, COL_F:].reshape(bsz, seq, D_F)
    f_mix = _fourier_mix(f_in, _dft_tables(seq)).reshape(bsz * seq, D_F)
    xbc = _conv_silu(proj, p["conv_w"], p["conv_b"], seq)
    y1, y2 = _ssd_scan(xbc, dt_f, dt_b, p, seq)
    x1, hn, route, blk = _merge_route(y1, y2, proj, f_mix, x, p["ssm_norm"], p["w_fourier"], p["w_ssm_out"],
                                 p["w_out"], p["norm_ffn"], p["w_router"], p["b_router"])
    out = _moe_routed(hn, route, blk, x1, p["w1"], p["b1"], p["w2"], p["b2"], p["norm_final"])
    return out.reshape(bsz, seq, D_MODEL)


def kernel(x_prompt, x_sample, norm_mix, w_in, conv_w, conv_b, dt_bias_fwd, dt_bias_bwd, a_log_fwd, a_log_bwd, d_skip, ssm_norm, w_fourier, w_ssm_out, w_out, norm_ffn, w_router, b_router, w_gate_up, b_gate_up, w_down, b_down, norm_final):
    assert norm_mix.shape[0] == 1, "single-layer block"
    w = w_in[0]
    o_z, o_xbc, o_dt, o_gate = D_F, D_F + D_INNER, D_F + D_INNER + CONV_DIM, D_F + D_INNER + CONV_DIM + 2 * N_HEADS
    w_main = jnp.concatenate([w[:, o_z:o_xbc], w[:, o_xbc:o_dt], w[:, o_gate:], w[:, :D_F]], axis=1).astype(BF16)
    pad_dt = lambda m: jnp.pad(m, ((0, 0), (0, LANES - N_HEADS))).astype(BF16)
    head_of_chan = jnp.arange(D_INNER, dtype=jnp.int32) // HEADDIM
    expand = (jnp.arange(LANES, dtype=jnp.int32)[:, None] == head_of_chan[None, :]).astype(BF16)
    p = dict(
        norm_mix=norm_mix[0].reshape(1, D_MODEL),
        w_main=w_main,
        w_dtf=pad_dt(w[:, o_dt:o_dt + N_HEADS]),
        w_dtb=pad_dt(w[:, o_dt + N_HEADS:o_gate]),
        conv_w=conv_w[0], conv_b=conv_b[0].reshape(1, CONV_DIM),
        dtb_f=_pad_lanes(dt_bias_fwd[0]), dtb_b=_pad_lanes(dt_bias_bwd[0]),
        alog_f=_pad_lanes(a_log_fwd[0], NEG_BIG), alog_b=_pad_lanes(a_log_bwd[0], NEG_BIG),
        expand=expand,
        d_skip=jnp.repeat(d_skip[0].astype(F32), HEADDIM).reshape(1, D_INNER),
        ssm_norm=ssm_norm[0].reshape(1, D_INNER),
        w_fourier=w_fourier[0].astype(BF16), w_ssm_out=w_ssm_out[0].astype(BF16), w_out=w_out[0].astype(BF16),
        norm_ffn=norm_ffn[0].reshape(1, D_MODEL),
        w_router=jnp.pad(w_router[0], ((0, 0), (0, LANES - N_EXPERTS))).astype(BF16),
        b_router=_pad_lanes(b_router[0], NEG_BIG),
        w1=w_gate_up[0].astype(BF16), b1=b_gate_up[0].reshape(N_EXPERTS, 1, 2 * D_FF),
        w2=w_down[0].astype(BF16), b2=b_down[0].reshape(N_EXPERTS, 1, D_MODEL),
        norm_final=norm_final.reshape(1, D_MODEL),
    )
    return (_stream(x_prompt, p), _stream(x_sample, p))
```

```python
import functools
import math

import jax
import jax.numpy as jnp
import numpy as np
from jax import lax
from jax.experimental import pallas as pl
from jax.experimental.pallas import tpu as pltpu

F32 = jnp.float32
BF16 = jnp.bfloat16

D_MODEL = 1024
D_F = 1024
FGROUP = 256
D_INNER = 2048
HEADDIM = 64
N_HEADS = 32
N_GROUPS = 8
HEADS_PER_GROUP = N_HEADS // N_GROUPS
D_STATE = 128
D_CONV = 5
CHUNK = 128
CONV_DIM = D_INNER + 2 * N_GROUPS * D_STATE
N_EXPERTS = 32
TOP_K = 4
D_FF = 1024
SWIGLU_ALPHA = 1.702
SWIGLU_LIMIT = 7.0
EPS = 1e-5
NEG_BIG = -1e30

LANES = 128
HALO = 16
CONV_SUB = 128
MERGE_CHUNK = 512
VMEM_LIMIT = 56 * 1024 * 1024

TOK_BLOCK = 512
PIECE = 16
FFN_TM = 512
_WORST_PIECES = (TOK_BLOCK * TOP_K + N_EXPERTS * (PIECE - 1) + PIECE - 1) // PIECE
_PIECES_PER_TILE = FFN_TM // PIECE
PIECES_PER_BLOCK = (_WORST_PIECES + _PIECES_PER_TILE - 1) // _PIECES_PER_TILE * _PIECES_PER_TILE
ROWS_PER_BLOCK = PIECES_PER_BLOCK * PIECE
DMA_UNROLL = 8
ITEM_NONE, ITEM_WRITE, ITEM_MERGE, ITEM_ZERO = 0, 1, 2, 3

COL_Z = 0
COL_XBC = D_INNER
COL_GATE = COL_XBC + CONV_DIM
COL_F = COL_GATE + 2 * D_MODEL
PROJ_MAIN = COL_F + D_F


def _cparams(sem):
    return pltpu.CompilerParams(dimension_semantics=sem, vmem_limit_bytes=VMEM_LIMIT)


def _inproj_kernel(x_ref, nw_ref, w_ref, wdt_ref, proj_ref, f_ref, dt_ref, hn_ref, *, n_main):
    j = pl.program_id(1)

    @pl.when(j == 0)
    def _():
        x = x_ref[...]
        ms = jnp.mean(x * x, axis=-1, keepdims=True)
        hn = (x * lax.rsqrt(ms + EPS) * nw_ref[...]).astype(BF16)
        hn_ref[...] = hn
        dt_ref[...] = jnp.dot(hn, wdt_ref[...], preferred_element_type=F32)

    def tile():
        return jnp.dot(hn_ref[...], w_ref[...], preferred_element_type=F32).astype(BF16)

    @pl.when(j < n_main)
    def _():
        proj_ref[...] = tile()

    @pl.when(j == n_main)
    def _():
        f_ref[...] = tile()


def _inproj(x, norm_w, w_main, w_dt, tm=2048, tn=1024):
    t = x.shape[0]
    tm = min(tm, t)
    assert tn == D_F and COL_F % tn == 0
    n_main = COL_F // tn
    return pl.pallas_call(
        functools.partial(_inproj_kernel, n_main=n_main),
        grid=(t // tm, n_main + 1),
        in_specs=[
            pl.BlockSpec((tm, D_MODEL), lambda i, j: (i, 0)),
            pl.BlockSpec((1, D_MODEL), lambda i, j: (0, 0)),
            pl.BlockSpec((D_MODEL, tn), lambda i, j: (0, j)),
            pl.BlockSpec((D_MODEL, 2 * LANES), lambda i, j: (0, 0)),
        ],
        out_specs=[
            pl.BlockSpec((tm, tn), lambda i, j: (i, jnp.minimum(j, n_main - 1))),
            pl.BlockSpec((tm, D_F), lambda i, j: (i, 0)),
            pl.BlockSpec((tm, 2 * LANES), lambda i, j: (i, 0)),
        ],
        out_shape=[
            jax.ShapeDtypeStruct((t, COL_F), BF16),
            jax.ShapeDtypeStruct((t, D_F), BF16),
            jax.ShapeDtypeStruct((t, 2 * LANES), F32),
        ],
        scratch_shapes=[pltpu.VMEM((tm, D_MODEL), BF16)],
        compiler_params=_cparams(("parallel", "arbitrary")),
        name="inproj",
    )(x, norm_w, w_main, w_dt)


def _dft_factors(seq):
    n1 = 1 << ((int(math.log2(seq)) + 1) // 2)
    return n1, seq // n1


def _dft_tables(seq):
    n1, n2 = _dft_factors(seq)
    two_pi = 2.0 * math.pi
    c = jnp.arange(FGROUP, dtype=jnp.int32)
    ph = ((c[:, None] * c[None, :]) % FGROUP).astype(F32) * (two_pi / FGROUP)
    cs_chan = jnp.concatenate([jnp.cos(ph), -jnp.sin(ph)], axis=1).astype(BF16)
    k1 = jnp.arange(n1, dtype=jnp.int32)
    nn = (n2 * jnp.arange(n1, dtype=jnp.int32)[None, :] + jnp.arange(n2, dtype=jnp.int32)[:, None])
    al = ((k1[None, :, None] * nn[:, None, :]) % seq).astype(F32) * (two_pi / seq)
    ca, sa = jnp.cos(al), jnp.sin(al)
    g1 = jnp.concatenate([jnp.concatenate([ca, sa], axis=2),
                          jnp.concatenate([-sa, ca], axis=2)], axis=1).astype(BF16)
    k2 = jnp.arange(n2, dtype=jnp.int32)
    be = ((k2[:, None] * k2[None, :]) % n2).astype(F32) * (two_pi / n2)
    g2 = jnp.concatenate([jnp.cos(be), jnp.sin(be)], axis=1).astype(BF16)
    return cs_chan, g1, g2


def _dft1_kernel(x_ref, cs_ref, g_ref, o_ref, *, tn2, n1):
    for j in range(tn2):
        x = x_ref[0, j]
        parts = []
        for q in range(D_F // FGROUP):
            uv = jnp.dot(x[:, q * FGROUP:(q + 1) * FGROUP], cs_ref[...],
                         preferred_element_type=F32).astype(BF16)
            parts.append(jnp.concatenate([uv[:, :FGROUP], uv[:, FGROUP:]], axis=0))
        z = jnp.concatenate(parts, axis=1)
        o_ref[0, j] = jnp.dot(g_ref[j], z, preferred_element_type=F32).astype(BF16)


def _dft2_kernel(a_ref, g_ref, o_ref, *, tk1, scale):
    for j in range(tk1):
        o_ref[0, j] = (jnp.dot(g_ref[...], a_ref[0, j], preferred_element_type=F32) * scale).astype(BF16)


def _fourier_mix(f_in, tables):
    bsz, seq, _ = f_in.shape
    n1, n2 = _dft_factors(seq)
    cs_chan, g1, g2 = tables
    xt = f_in.reshape(bsz, n1, n2, D_F).transpose(0, 2, 1, 3)
    tn2 = 4
    stage1 = pl.pallas_call(
        functools.partial(_dft1_kernel, tn2=tn2, n1=n1),
        grid=(bsz, n2 // tn2),
        in_specs=[
            pl.BlockSpec((1, tn2, n1, D_F), lambda b, i: (b, i, 0, 0)),
            pl.BlockSpec((FGROUP, 2 * FGROUP), lambda b, i: (0, 0)),
            pl.BlockSpec((tn2, 2 * n1, 2 * n1), lambda b, i: (i, 0, 0)),
        ],
        out_specs=pl.BlockSpec((1, tn2, 2 * n1, D_F), lambda b, i: (b, i, 0, 0)),
        out_shape=jax.ShapeDtypeStruct((bsz, n2, 2 * n1, D_F), BF16),
        compiler_params=_cparams(("parallel", "parallel")),
        name="dft_stage1",
    )(xt, cs_chan, g1)
    a2 = (stage1.reshape(bsz, n2, 2, n1, D_F).transpose(0, 3, 2, 1, 4)
          .reshape(bsz, n1, 2 * n2, D_F))
    tk1 = 4
    scale = 1.0 / math.sqrt(seq * FGROUP)
    stage2 = pl.pallas_call(
        functools.partial(_dft2_kernel, tk1=tk1, scale=scale),
        grid=(bsz, n1 // tk1),
        in_specs=[
            pl.BlockSpec((1, tk1, 2 * n2, D_F), lambda b, i: (b, i, 0, 0)),
            pl.BlockSpec((n2, 2 * n2), lambda b, i: (0, 0)),
        ],
        out_specs=pl.BlockSpec((1, tk1, n2, D_F), lambda b, i: (b, i, 0, 0)),
        out_shape=jax.ShapeDtypeStruct((bsz, n1, n2, D_F), BF16),
        compiler_params=_cparams(("parallel", "parallel")),
        name="dft_stage2",
    )(a2, g2)
    return stage2.transpose(0, 2, 1, 3).reshape(bsz, seq, D_F)


def _conv_kernel(prev_ref, main_ref, next_ref, w_ref, b_ref, o_ref, *, tl, tiles_per_seq):
    i = pl.program_id(0) % tiles_per_seq
    halo_zero = jnp.zeros(prev_ref.shape, BF16)
    prev = jnp.where(i == 0, halo_zero, prev_ref[...])
    nxt = jnp.where(i == tiles_per_seq - 1, halo_zero, next_ref[...])
    full = jnp.concatenate([prev, main_ref[...], nxt], axis=0)
    pad = D_CONV // 2
    win = CONV_SUB + 2 * HALO
    r = lax.broadcasted_iota(jnp.int32, (CONV_SUB, win), 0)
    c = lax.broadcasted_iota(jnp.int32, (CONV_SUB, win), 1)
    taps = [k for k in range(D_CONV) if k != pad]
    shift = jnp.concatenate([jnp.where(c == r + (HALO + k - pad), 1.0, 0.0) for k in taps], axis=1).astype(BF16)
    w_bf = w_ref[...].astype(BF16)
    for j in range(tl // CONV_SUB):
        window = full[j * CONV_SUB:j * CONV_SUB + win]
        stacked = jnp.concatenate([window * w_bf[k:k + 1, :] for k in taps], axis=0)
        acc = (b_ref[...] + w_ref[pad:pad + 1, :] * window[HALO:HALO + CONV_SUB].astype(F32)
               + jnp.dot(shift, stacked, preferred_element_type=F32))
        o_ref[j * CONV_SUB:(j + 1) * CONV_SUB, :] = (acc * jax.nn.sigmoid(acc)).astype(BF16)


def _conv_silu(proj, conv_w, conv_b, seq, tl=512, tc=2048):
    t = proj.shape[0]
    tl = min(tl, seq)
    tiles_per_seq = seq // tl
    cb0 = COL_XBC // tc
    hb = tl // HALO
    last_hb = t // HALO - 1
    return pl.pallas_call(
        functools.partial(_conv_kernel, tl=tl, tiles_per_seq=tiles_per_seq),
        grid=(t // tl, CONV_DIM // tc),
        in_specs=[
            pl.BlockSpec((HALO, tc), lambda i, c: (jnp.maximum(i * hb - 1, 0), cb0 + c)),
            pl.BlockSpec((tl, tc), lambda i, c: (i, cb0 + c)),
            pl.BlockSpec((HALO, tc), lambda i, c: (jnp.minimum((i + 1) * hb, last_hb), cb0 + c)),
            pl.BlockSpec((D_CONV, tc), lambda i, c: (0, c)),
            pl.BlockSpec((1, tc), lambda i, c: (0, c)),
        ],
        out_specs=pl.BlockSpec((tl, tc), lambda i, c: (i, c)),
        out_shape=jax.ShapeDtypeStruct((t, CONV_DIM), BF16),
        compiler_params=_cparams(("parallel", "parallel")),
        name="conv_silu",
    )(proj, proj, proj, conv_w, conv_b)


def _split3(v):
    hi = v.astype(BF16)
    r1 = v - hi.astype(F32)
    mid = r1.astype(BF16)
    lo = (r1 - mid.astype(F32)).astype(BF16)
    return [hi, mid, lo]


def _masked_sums(tri, vals):
    parts = []
    for v in vals:
        parts += _split3(v)
    out = jnp.dot(tri, jnp.concatenate(parts, axis=1), preferred_element_type=F32)
    return [out[:, (3 * i) * LANES:(3 * i + 1) * LANES] + out[:, (3 * i + 1) * LANES:(3 * i + 2) * LANES]
            + out[:, (3 * i + 2) * LANES:(3 * i + 3) * LANES] for i in range(len(vals))]


def _ssd_kernel(xs_ref, b_ref, c_ref, dtf_ref, dtb_ref, xs2_ref, b2_ref, c2_ref, dtb2_ref,
                biasf_ref, biasb_ref, alogf_ref, alogb_ref, exp_ref, dskip_ref,
                y1_ref, y2_ref, statef_ref, stateb_ref):
    @pl.when(pl.program_id(1) == 0)
    def _():
        statef_ref[...] = jnp.zeros(statef_ref.shape, F32)
        stateb_ref[...] = jnp.zeros(stateb_ref.shape, F32)

    row = lax.broadcasted_iota(jnp.int32, (CHUNK, CHUNK), 0)
    col = lax.broadcasted_iota(jnp.int32, (CHUNK, CHUNK), 1)
    lower = row >= col
    diag = row == col
    tri_f = jnp.where(lower, 1.0, 0.0).astype(BF16)
    tri_b = jnp.where(row <= col, 1.0, 0.0).astype(BF16)

    af_neg = -jnp.exp(alogf_ref[...])
    ab_neg = -jnp.exp(alogb_ref[...])
    dt_f = jax.nn.softplus(dtf_ref[...] + biasf_ref[...])
    dt_b = jax.nn.softplus(dtb_ref[...] + biasb_ref[...])
    dt_b2 = jax.nn.softplus(dtb2_ref[...] + biasb_ref[...])
    (cf,) = _masked_sums(tri_f, [dt_f * af_neg])
    cb, cb2 = _masked_sums(tri_b, [dt_b * ab_neg, dt_b2 * ab_neg])
    tot_f = cf[CHUNK - 1:CHUNK, :]
    tot_b2 = cb2[0:1, :]
    dtb_t = dt_b.T
    srcf_t = (cf - jnp.log(dt_f)).T
    srcb_t = cb.T - jnp.log(dtb_t)

    dec = jnp.concatenate(_split3(jnp.exp(tot_f)) + _split3(jnp.exp(tot_b2))
                          + [jnp.zeros((PIECE - 6, LANES), BF16)], axis=0)
    stack = jnp.concatenate([(dt_f * jnp.exp(tot_f - cf)).astype(BF16), jnp.exp(cf).astype(BF16),
                             (dt_b2 * jnp.exp(tot_b2 - cb2)).astype(BF16), jnp.exp(cb2).astype(BF16), dec], axis=0)
    lane = lax.broadcasted_iota(jnp.int32, (CHUNK, LANES), 1)
    lo = lane < HEADDIM
    gw = HEADS_PER_GROUP * HEADDIM
    tn = (((0,), (0,)), ((), ()))
    d0 = 4 * CHUNK
    for g in range(N_GROUPS):
        gs = slice(g * gw, (g + 1) * gw)
        ns = slice(g * D_STATE, (g + 1) * D_STATE)
        ex = jnp.dot(stack, exp_ref[:, gs], preferred_element_type=F32)
        decf_e = ex[d0:d0 + 1] + ex[d0 + 1:d0 + 2] + ex[d0 + 2:d0 + 3]
        decb_e = ex[d0 + 3:d0 + 4] + ex[d0 + 4:d0 + 5] + ex[d0 + 5:d0 + 6]
        xd_f = (xs_ref[:, gs].astype(F32) * ex[0:CHUNK]).astype(BF16)
        xd_b = (xs2_ref[:, gs].astype(F32) * ex[2 * CHUNK:3 * CHUNK]).astype(BF16)
        bg, cg = b_ref[:, ns], c_ref[:, ns]
        cbm = lax.dot_general(cg, bg, (((1,), (1,)), ((), ())), preferred_element_type=F32)
        prev_f = statef_ref[g]
        y_off = jnp.dot(cg, prev_f.astype(BF16), preferred_element_type=F32) * ex[CHUNK:2 * CHUNK]
        statef_ref[g] = prev_f * decf_e + lax.dot_general(bg, xd_f, tn, preferred_element_type=F32)
        prev_b = stateb_ref[g]
        y2_ref[:, gs] = (jnp.dot(c2_ref[:, ns], prev_b.astype(BF16), preferred_element_type=F32)
                         * ex[3 * CHUNK:4 * CHUNK]).astype(BF16)
        stateb_ref[g] = prev_b * decb_e + lax.dot_general(b2_ref[:, ns], xd_b, tn, preferred_element_type=F32)
        for q in range(HEADS_PER_GROUP // 2):
            ms = []
            for hh in range(2):
                h = g * HEADS_PER_GROUP + 2 * q + hh
                seg = jnp.where(lower, cf[:, h:h + 1] - srcf_t[h:h + 1, :], cb[:, h:h + 1] - srcb_t[h:h + 1, :])
                wgt = jnp.exp(seg) + jnp.where(diag, dtb_t[h:h + 1, :], 0.0)
                ms.append((wgt * cbm).astype(BF16))
            lhs = jnp.concatenate(ms, axis=1)
            c0 = g * gw + 2 * q * HEADDIM
            xp = xs_ref[:, c0:c0 + LANES]
            zero = jnp.zeros_like(xp)
            rhs = jnp.concatenate([jnp.where(lo, xp, zero), jnp.where(lo, zero, xp)], axis=0)
            y = (jnp.dot(lhs, rhs, preferred_element_type=F32) + y_off[:, 2 * q * HEADDIM:2 * q * HEADDIM + LANES]
                 + dskip_ref[:, c0:c0 + LANES] * xp.astype(F32))
            y1_ref[:, c0:c0 + LANES] = y.astype(BF16)


def _ssd_scan(xbc, dt, p, seq):
    t = xbc.shape[0]
    nc = seq // CHUNK
    near = lambda col: (lambda b, c: (b * nc + c, col))
    far = lambda col: (lambda b, c: (b * nc + nc - 1 - c, col))
    const = lambda b, c: (0, 0)
    gn = N_GROUPS * D_STATE
    state = pltpu.VMEM((N_GROUPS, D_STATE, HEADS_PER_GROUP * HEADDIM), F32)
    return pl.pallas_call(
        _ssd_kernel,
        grid=(t // seq, nc),
        in_specs=[
            pl.BlockSpec((CHUNK, D_INNER), near(0)),
            pl.BlockSpec((CHUNK, gn), near(D_INNER // gn)),
            pl.BlockSpec((CHUNK, gn), near(D_INNER // gn + 1)),
            pl.BlockSpec((CHUNK, LANES), near(0)),
            pl.BlockSpec((CHUNK, LANES), near(1)),
            pl.BlockSpec((CHUNK, D_INNER), far(0)),
            pl.BlockSpec((CHUNK, gn), far(D_INNER // gn)),
            pl.BlockSpec((CHUNK, gn), far(D_INNER // gn + 1)),
            pl.BlockSpec((CHUNK, LANES), far(1)),
            pl.BlockSpec((1, LANES), const), pl.BlockSpec((1, LANES), const),
            pl.BlockSpec((1, LANES), const), pl.BlockSpec((1, LANES), const),
            pl.BlockSpec((LANES, D_INNER), const),
            pl.BlockSpec((1, D_INNER), const),
        ],
        out_specs=[pl.BlockSpec((CHUNK, D_INNER), near(0)), pl.BlockSpec((CHUNK, D_INNER), far(0))],
        out_shape=[jax.ShapeDtypeStruct((t, D_INNER), BF16), jax.ShapeDtypeStruct((t, D_INNER), BF16)],
        scratch_shapes=[state, state],
        compiler_params=_cparams(("parallel", "arbitrary")),
        name="ssd_scan",
    )(xbc, xbc, xbc, dt, dt, xbc, xbc, xbc, dt,
      p["dtb_f"], p["dtb_b"], p["alog_f"], p["alog_b"], p["expand"], p["d_skip"])


def _merge_kernel(y1_ref, y2_ref, z_ref, gate_ref, fm_ref, x_ref, snw_ref, wf_ref, ws_ref, wo_ref, fnw_ref,
                  wr_ref, br_ref, x1_ref, hn_ref, gates_ref, blk_ref):
    acc = None
    ssq = None
    for c0 in range(0, D_INNER, MERGE_CHUNK):
        cs = slice(c0, c0 + MERGE_CHUNK)
        z = z_ref[:, cs].astype(F32)
        yg = (y1_ref[:, cs].astype(F32) + y2_ref[:, cs].astype(F32)) * (z * jax.nn.sigmoid(z))
        sq = jnp.sum(yg * yg, axis=-1, keepdims=True)
        part = jnp.dot((yg * snw_ref[:, cs]).astype(BF16), ws_ref[cs, :], preferred_element_type=F32)
        acc = part if acc is None else acc + part
        ssq = sq if ssq is None else ssq + sq
    u_s = acc * lax.rsqrt(ssq * (1.0 / D_INNER) + EPS)
    x1 = x_ref[...]
    fm = fm_ref[...]
    for c0 in range(0, D_MODEL, MERGE_CHUNK):
        cs = slice(c0, c0 + MERGE_CHUNK)
        u_f = jnp.dot(fm, wf_ref[:, cs], preferred_element_type=F32)
        g_f = jax.nn.sigmoid(gate_ref[:, cs].astype(F32))
        g_s = jax.nn.sigmoid(gate_ref[:, D_MODEL + c0:D_MODEL + c0 + MERGE_CHUNK].astype(F32))
        merged = (g_f * u_f + g_s * u_s[:, cs]).astype(BF16)
        x1 = x1 + jnp.dot(merged, wo_ref[cs, :], preferred_element_type=F32)
    x1_ref[...] = x1
    ms1 = jnp.mean(x1 * x1, axis=-1, keepdims=True)
    hn = (x1 * lax.rsqrt(ms1 + EPS) * fnw_ref[...]).astype(BF16)
    hn_ref[...] = hn
    logits = jnp.dot(hn, wr_ref[...], preferred_element_type=F32) + br_ref[...]
    lane = lax.broadcasted_iota(jnp.int32, logits.shape, 1).astype(F32)
    tm = logits.shape[0]
    work = logits
    top = None
    denom = jnp.zeros((tm, 1), F32)
    route = jnp.zeros(logits.shape, F32)
    probs, onehots = [], []
    for k in range(TOP_K):
        m = jnp.max(work, axis=-1, keepdims=True)
        if k == 0:
            top = m
        first = jnp.min(jnp.where(work == m, lane, float(LANES)), axis=-1, keepdims=True)
        sel = lane == first
        onehots.append(jnp.where(sel, 1.0, 0.0))
        work = jnp.where(sel, NEG_BIG * 2, work)
        pk = jnp.exp(m - top)
        denom = denom + pk
        probs.append(pk)
        route = jnp.where(lane == float(k), first, route)
    inv = 1.0 / denom
    for k in range(TOP_K):
        route = jnp.where(lane == float(TOP_K + k), probs[k] * inv, route)
    ti = lax.broadcasted_iota(jnp.int32, (tm, tm), 0)
    tj = lax.broadcasted_iota(jnp.int32, (tm, tm), 1)
    earlier = jnp.where(tj < ti, 1.0, 0.0).astype(BF16)
    prefix = jnp.dot(earlier, jnp.concatenate(onehots, axis=1).astype(BF16), preferred_element_type=F32)
    cnts = [prefix[tm - 1:tm, k * LANES:(k + 1) * LANES] + onehots[k][tm - 1:tm, :] for k in range(TOP_K)]
    pieces = jnp.floor((cnts[0] + cnts[1] + cnts[2] + cnts[3] + (PIECE - 1.0)) * (1.0 / PIECE))
    ei = lax.broadcasted_iota(jnp.int32, (LANES, LANES), 0)
    ej = lax.broadcasted_iota(jnp.int32, (LANES, LANES), 1)
    lower_experts = jnp.where(ei < ej, 1.0, 0.0).astype(BF16)
    seg_start = jnp.dot(jnp.broadcast_to(pieces, (8, LANES)).astype(BF16), lower_experts,
                        preferred_element_type=F32)[0:1, :]
    base = seg_start * float(PIECE)
    for k in range(TOP_K):
        row_of = base + prefix[:, k * LANES:(k + 1) * LANES]
        pos = jnp.sum(onehots[k] * row_of, axis=-1, keepdims=True)
        route = jnp.where(lane == float(2 * TOP_K + k), pos, route)
        base = base + cnts[k]
    gates_ref[...] = route
    blk_ref[0] = jnp.concatenate([pieces, seg_start, jnp.zeros((6, LANES), F32)], axis=0)


def _merge_route(y1, y2, proj, f_mix, x, ssm_norm, w_fourier, w_ssm_out, w_out, norm_ffn, w_router, b_router):
    t = x.shape[0]
    tm = TOK_BLOCK
    assert t % tm == 0
    full = lambda r, c: pl.BlockSpec((r, c), lambda i: (0, 0))
    return pl.pallas_call(
        _merge_kernel,
        grid=(t // tm,),
        in_specs=[
            pl.BlockSpec((tm, D_INNER), lambda i: (i, 0)),
            pl.BlockSpec((tm, D_INNER), lambda i: (i, 0)),
            pl.BlockSpec((tm, D_INNER), lambda i: (i, COL_Z // D_INNER)),
            pl.BlockSpec((tm, 2 * D_MODEL), lambda i: (i, COL_GATE // (2 * D_MODEL))),
            pl.BlockSpec((tm, D_F), lambda i: (i, 0)),
            pl.BlockSpec((tm, D_MODEL), lambda i: (i, 0)),
            full(1, D_INNER), full(D_F, D_MODEL), full(D_INNER, D_MODEL), full(D_MODEL, D_MODEL),
            full(1, D_MODEL), full(D_MODEL, LANES), full(1, LANES),
        ],
        out_specs=[
            pl.BlockSpec((tm, D_MODEL), lambda i: (i, 0)),
            pl.BlockSpec((tm, D_MODEL), lambda i: (i, 0)),
            pl.BlockSpec((tm, LANES), lambda i: (i, 0)),
            pl.BlockSpec((1, 8, LANES), lambda i: (i, 0, 0)),
        ],
        out_shape=[
            jax.ShapeDtypeStruct((t, D_MODEL), F32),
            jax.ShapeDtypeStruct((t, D_MODEL), BF16),
            jax.ShapeDtypeStruct((t, LANES), F32),
            jax.ShapeDtypeStruct((t // tm, 8, LANES), F32),
        ],
        compiler_params=_cparams(("parallel",)),
        name="merge_route",
    )(y1, y2, proj, proj, f_mix, x, ssm_norm, w_fourier, w_ssm_out, w_out, norm_ffn, w_router, b_router)


def _route_tables(route, blk, t):
    i32 = jnp.int32
    nb = t // TOK_BLOCK
    pcs = blk[:, 0, :N_EXPERTS].astype(i32)
    seg_start = blk[:, 1, :N_EXPERTS].astype(i32)
    seg_end = seg_start + pcs
    used = seg_end[:, -1]
    per_e = jnp.sum(pcs, axis=0)
    g_end = jnp.cumsum(per_e)
    g_start = g_end - per_e
    b_prefix = jnp.cumsum(pcs, axis=0) - pcs
    j = jnp.arange(PIECES_PER_BLOCK, dtype=i32)[None, :, None]
    in_seg = jnp.logical_and(j >= seg_start[:, None, :], j < seg_end[:, None, :])
    shift = (g_start[None, :] + b_prefix - seg_start)[:, None, :]
    j2 = j[:, :, 0]
    dst_used = jnp.sum(jnp.where(in_seg, shift, 0), axis=2) + j2
    free = PIECES_PER_BLOCK - used
    dst_unused = g_end[-1] + (jnp.cumsum(free) - free)[:, None] + j2 - used[:, None]
    is_used = j2 < used[:, None]
    dst = jnp.where(is_used, dst_used, dst_unused).reshape(-1)
    src = jnp.where(is_used, dst_used, 0).reshape(-1)
    n_tiles = nb * ROWS_PER_BLOCK // FFN_TM
    n_items_max = n_tiles + N_EXPERTS
    start_rows, end_rows = g_start * PIECE, g_end * PIECE
    t0 = start_rows // FFN_TM
    n_it = jnp.where(end_rows > start_rows, (end_rows - 1) // FFN_TM - t0 + 1, 0)
    it_end = jnp.cumsum(n_it)
    it_start = it_end - n_it
    n_exp = it_end[-1]
    tiles_used = (end_rows[-1] + FFN_TM - 1) // FFN_TM
    q = jnp.arange(n_items_max, dtype=i32)
    qc = jnp.minimum(q, n_exp - 1)[:, None]
    mine = jnp.logical_and(qc >= it_start[None, :], qc < it_end[None, :])
    pick = lambda v: jnp.sum(jnp.where(mine, v[None, :], 0), axis=1)
    e_q = pick(jnp.arange(N_EXPERTS, dtype=i32))
    tile_exp = pick(t0 - it_start) + qc[:, 0]
    tile_q = jnp.where(q < n_exp, tile_exp, jnp.minimum(tiles_used + q - n_exp, n_tiles - 1))
    lo = jnp.maximum(pick(start_rows), tile_q * FFN_TM) - tile_q * FFN_TM
    hi = jnp.minimum(pick(end_rows), (tile_q + 1) * FFN_TM) - tile_q * FFN_TM
    new_tile = jnp.concatenate([jnp.ones((1,), i32), (tile_q[1:] != tile_q[:-1]).astype(i32)])
    kind = jnp.where(q < n_exp, jnp.where(new_tile == 1, ITEM_WRITE, ITEM_MERGE),
                     jnp.where(new_tile == 1, ITEM_ZERO, ITEM_NONE))
    pos_t = (route[:, 2 * TOP_K:3 * TOP_K].astype(i32).reshape(nb, TOK_BLOCK, TOP_K).transpose(0, 2, 1))
    return dict(pos_t=pos_t, dst=dst, src=src,
                item_e=e_q, item_tile=tile_q, item_lo=lo, item_hi=hi, item_kind=kind)


def _piece_copy_out(buf_ref, hbm_ref, sem, slot, j, dst_piece):
    return pltpu.make_async_copy(buf_ref.at[slot, j], hbm_ref.at[dst_piece], sem.at[slot])


def _piece_copy_in(hbm_ref, buf_ref, sem, slot, j, src_piece):
    return pltpu.make_async_copy(hbm_ref.at[src_piece], buf_ref.at[slot, j], sem.at[slot])


def _dispatch_kernel(dst_ref, hn_ref, post_ref, xs_hbm, buf_ref, sem, *, nb):
    b = pl.program_id(0)
    slot = b % 2

    def wait_slot(s):
        pltpu.make_async_copy(buf_ref.at[s], xs_hbm.at[pl.ds(0, PIECES_PER_BLOCK)], sem.at[s]).wait()

    @pl.when(b >= 2)
    def _():
        wait_slot(slot)

    r = lax.broadcasted_iota(jnp.int32, (ROWS_PER_BLOCK, TOK_BLOCK), 0)
    onehot = jnp.zeros((ROWS_PER_BLOCK, TOK_BLOCK), F32)
    for k in range(TOP_K):
        onehot = jnp.where(r == post_ref[0, k:k + 1, :], 1.0, onehot)
    sorted_rows = jnp.dot(onehot.astype(BF16), hn_ref[...], preferred_element_type=F32).astype(BF16)
    buf_ref[slot] = sorted_rows.reshape(PIECES_PER_BLOCK, PIECE, D_MODEL)

    def start(j, c):
        _piece_copy_out(buf_ref, xs_hbm, sem, slot, j, dst_ref[b * PIECES_PER_BLOCK + j]).start()
        return c
    lax.fori_loop(0, PIECES_PER_BLOCK, start, 0, unroll=DMA_UNROLL)

    @pl.when(b == nb - 1)
    def _():
        wait_slot(slot)
        if nb >= 2:
            wait_slot(1 - slot)


def _ffn_kernel(e_ref, tile_ref, lo_ref, hi_ref, kind_ref, x_ref, w1_ref, b1_ref, w2_ref, b2_ref, o_ref):
    q = pl.program_id(0)
    lo, hi, kind = lo_ref[q], hi_ref[q], kind_ref[q]

    def ffn():
        hu = jnp.dot(x_ref[...], w1_ref[0], preferred_element_type=F32) + b1_ref[0]
        glu = jnp.minimum(hu[:, :D_FF], SWIGLU_LIMIT)
        lin = jnp.clip(hu[:, D_FF:], -SWIGLU_LIMIT, SWIGLU_LIMIT)
        act = (glu * jax.nn.sigmoid(SWIGLU_ALPHA * glu) * (lin + 1.0)).astype(BF16)
        return (jnp.dot(act, w2_ref[0], preferred_element_type=F32) + b2_ref[0]).astype(BF16)

    @pl.when(kind == ITEM_WRITE)
    def _():
        o_ref[...] = ffn()

    @pl.when(kind == ITEM_MERGE)
    def _():
        row = lax.broadcasted_iota(jnp.int32, (FFN_TM, 1), 0)
        mine = jnp.logical_and(row >= lo, row < hi)
        o_ref[...] = jnp.where(mine, ffn(), o_ref[...])

    @pl.when(kind == ITEM_ZERO)
    def _():
        o_ref[...] = jnp.zeros(o_ref.shape, BF16)


def _combine_kernel(src_ref, route_ref, x1_ref, nf_ref, os_hbm, o_ref, buf_ref, sem, *, nb):
    b = pl.program_id(0)
    slot = b % 2

    def fetch(bb, s):
        def body(j, c):
            _piece_copy_in(os_hbm, buf_ref, sem, s, j, src_ref[bb * PIECES_PER_BLOCK + j]).start()
            return c
        lax.fori_loop(0, PIECES_PER_BLOCK, body, 0, unroll=DMA_UNROLL)

    @pl.when(b == 0)
    def _():
        fetch(0, 0)

    @pl.when(b + 1 < nb)
    def _():
        fetch(b + 1, 1 - slot)

    pltpu.make_async_copy(os_hbm.at[pl.ds(0, PIECES_PER_BLOCK)], buf_ref.at[slot], sem.at[slot]).wait()

    r = lax.broadcasted_iota(jnp.int32, (TOK_BLOCK, ROWS_PER_BLOCK), 1).astype(F32)
    route = route_ref[...]
    wmat = jnp.zeros((TOK_BLOCK, ROWS_PER_BLOCK), F32)
    for k in range(TOP_K):
        wmat = jnp.where(r == route[:, 2 * TOP_K + k:2 * TOP_K + k + 1], route[:, TOP_K + k:TOP_K + k + 1], wmat)
    y = jnp.dot(wmat.astype(BF16), buf_ref[slot].reshape(ROWS_PER_BLOCK, D_MODEL), preferred_element_type=F32)
    x2 = x1_ref[...] + y
    ms = jnp.mean(x2 * x2, axis=-1, keepdims=True)
    o_ref[...] = x2 * lax.rsqrt(ms + EPS) * nf_ref[...]


def _moe_routed(hn, route, blk, x1, w1, b1, w2, b2, norm_final):
    t = hn.shape[0]
    assert t % TOK_BLOCK == 0 and ROWS_PER_BLOCK % FFN_TM == 0
    nb = t // TOK_BLOCK
    rows = nb * ROWS_PER_BLOCK
    tb = _route_tables(route, blk, t)
    sorted_x = pl.pallas_call(
        functools.partial(_dispatch_kernel, nb=nb),
        grid_spec=pltpu.PrefetchScalarGridSpec(
            num_scalar_prefetch=1,
            grid=(nb,),
            in_specs=[
                pl.BlockSpec((TOK_BLOCK, D_MODEL), lambda b, d: (b, 0)),
                pl.BlockSpec((1, TOP_K, TOK_BLOCK), lambda b, d: (b, 0, 0)),
            ],
            out_specs=pl.BlockSpec(memory_space=pl.ANY),
            scratch_shapes=[pltpu.VMEM((2, PIECES_PER_BLOCK, PIECE, D_MODEL), BF16),
                            pltpu.SemaphoreType.DMA((2,))],
        ),
        out_shape=jax.ShapeDtypeStruct((rows // PIECE, PIECE, D_MODEL), BF16),
        compiler_params=_cparams(("arbitrary",)),
        name="moe_dispatch",
    )(tb["dst"], hn, tb["pos_t"]).reshape(rows, D_MODEL)
    n_items = rows // FFN_TM + N_EXPERTS
    sorted_o = pl.pallas_call(
        _ffn_kernel,
        grid_spec=pltpu.PrefetchScalarGridSpec(
            num_scalar_prefetch=5,
            grid=(n_items,),
            in_specs=[
                pl.BlockSpec((FFN_TM, D_MODEL), lambda q, e, ti, lo, hi, fi: (ti[q], 0)),
                pl.BlockSpec((1, D_MODEL, 2 * D_FF), lambda q, e, ti, lo, hi, fi: (e[q], 0, 0)),
                pl.BlockSpec((1, 1, 2 * D_FF), lambda q, e, ti, lo, hi, fi: (e[q], 0, 0)),
                pl.BlockSpec((1, D_FF, D_MODEL), lambda q, e, ti, lo, hi, fi: (e[q], 0, 0)),
                pl.BlockSpec((1, 1, D_MODEL), lambda q, e, ti, lo, hi, fi: (e[q], 0, 0)),
            ],
            out_specs=pl.BlockSpec((FFN_TM, D_MODEL), lambda q, e, ti, lo, hi, fi: (ti[q], 0)),
        ),
        out_shape=jax.ShapeDtypeStruct((rows, D_MODEL), BF16),
        compiler_params=_cparams(("arbitrary",)),
        name="moe_ffn",
    )(tb["item_e"], tb["item_tile"], tb["item_lo"], tb["item_hi"], tb["item_kind"], sorted_x, w1, b1, w2, b2)
    return pl.pallas_call(
        functools.partial(_combine_kernel, nb=nb),
        grid_spec=pltpu.PrefetchScalarGridSpec(
            num_scalar_prefetch=1,
            grid=(nb,),
            in_specs=[
                pl.BlockSpec((TOK_BLOCK, LANES), lambda b, s: (b, 0)),
                pl.BlockSpec((TOK_BLOCK, D_MODEL), lambda b, s: (b, 0)),
                pl.BlockSpec((1, D_MODEL), lambda b, s: (0, 0)),
                pl.BlockSpec(memory_space=pl.ANY),
            ],
            out_specs=pl.BlockSpec((TOK_BLOCK, D_MODEL), lambda b, s: (b, 0)),
            scratch_shapes=[pltpu.VMEM((2, PIECES_PER_BLOCK, PIECE, D_MODEL), BF16),
                            pltpu.SemaphoreType.DMA((2,))],
        ),
        out_shape=jax.ShapeDtypeStruct((t, D_MODEL), F32),
        compiler_params=_cparams(("arbitrary",)),
        name="moe_combine",
    )(tb["src"], route, x1, norm_final, sorted_o.reshape(rows // PIECE, PIECE, D_MODEL))


def _pad_lanes(v, fill=0.0):
    v = v.reshape(1, -1).astype(F32)
    return jnp.pad(v, ((0, 0), (0, LANES - v.shape[1])), constant_values=fill)


def _stream(x3, p):
    bsz, seq, _ = x3.shape
    x = x3.reshape(bsz * seq, D_MODEL)
    proj, f_in, dt = _inproj(x, p["norm_mix"], p["w_main"], p["w_dt"])
    f_mix = _fourier_mix(f_in.reshape(bsz, seq, D_F), _dft_tables(seq)).reshape(bsz * seq, D_F)
    xbc = _conv_silu(proj, p["conv_w"], p["conv_b"], seq)
    y1, y2 = _ssd_scan(xbc, dt, p, seq)
    x1, hn, route, blk = _merge_route(y1, y2, proj, f_mix, x, p["ssm_norm"], p["w_fourier"], p["w_ssm_out"],
                                 p["w_out"], p["norm_ffn"], p["w_router"], p["b_router"])
    out = _moe_routed(hn, route, blk, x1, p["w1"], p["b1"], p["w2"], p["b2"], p["norm_final"])
    return out.reshape(bsz, seq, D_MODEL)


def kernel(x_prompt, x_sample, norm_mix, w_in, conv_w, conv_b, dt_bias_fwd, dt_bias_bwd, a_log_fwd, a_log_bwd, d_skip, ssm_norm, w_fourier, w_ssm_out, w_out, norm_ffn, w_router, b_router, w_gate_up, b_gate_up, w_down, b_down, norm_final):
    assert norm_mix.shape[0] == 1, "single-layer block"
    w = w_in[0]
    o_z, o_xbc, o_dt, o_gate = D_F, D_F + D_INNER, D_F + D_INNER + CONV_DIM, D_F + D_INNER + CONV_DIM + 2 * N_HEADS
    w_main = jnp.concatenate([w[:, o_z:o_xbc], w[:, o_xbc:o_dt], w[:, o_gate:], w[:, :D_F]], axis=1).astype(BF16)
    pad_dt = lambda m: jnp.pad(m, ((0, 0), (0, LANES - N_HEADS))).astype(BF16)
    head_of_chan = jnp.arange(D_INNER, dtype=jnp.int32) // HEADDIM
    expand = (jnp.arange(LANES, dtype=jnp.int32)[:, None] == head_of_chan[None, :]).astype(BF16)
    p = dict(
        norm_mix=norm_mix[0].reshape(1, D_MODEL),
        w_main=w_main,
        w_dt=jnp.concatenate([pad_dt(w[:, o_dt:o_dt + N_HEADS]), pad_dt(w[:, o_dt + N_HEADS:o_gate])], axis=1),
        conv_w=conv_w[0], conv_b=conv_b[0].reshape(1, CONV_DIM),
        dtb_f=_pad_lanes(dt_bias_fwd[0]), dtb_b=_pad_lanes(dt_bias_bwd[0]),
        alog_f=_pad_lanes(a_log_fwd[0], NEG_BIG), alog_b=_pad_lanes(a_log_bwd[0], NEG_BIG),
        expand=expand,
        d_skip=jnp.repeat(d_skip[0].astype(F32), HEADDIM).reshape(1, D_INNER),
        ssm_norm=ssm_norm[0].reshape(1, D_INNER),
        w_fourier=w_fourier[0].astype(BF16), w_ssm_out=w_ssm_out[0].astype(BF16), w_out=w_out[0].astype(BF16),
        norm_ffn=norm_ffn[0].reshape(1, D_MODEL),
        w_router=jnp.pad(w_router[0], ((0, 0), (0, LANES - N_EXPERTS))).astype(BF16),
        b_router=_pad_lanes(b_router[0], NEG_BIG),
        w1=w_gate_up[0].astype(BF16), b1=b_gate_up[0].reshape(N_EXPERTS, 1, 2 * D_FF),
        w2=w_down[0].astype(BF16), b2=b_down[0].reshape(N_EXPERTS, 1, D_MODEL),
        norm_final=norm_final.reshape(1, D_MODEL),
    )
    return (_stream(x_prompt, p), _stream(x_sample, p))
```

```python
import functools
import math

import jax
import jax.numpy as jnp
import numpy as np
from jax import lax
from jax.experimental import pallas as pl
from jax.experimental.pallas import tpu as pltpu

F32 = jnp.float32
BF16 = jnp.bfloat16

D_MODEL = 1024
D_F = 1024
FGROUP = 256
D_INNER = 2048
HEADDIM = 64
N_HEADS = 32
N_GROUPS = 8
HEADS_PER_GROUP = N_HEADS // N_GROUPS
D_STATE = 128
D_CONV = 5
CHUNK = 128
CONV_DIM = D_INNER + 2 * N_GROUPS * D_STATE
N_EXPERTS = 32
TOP_K = 4
D_FF = 1024
SWIGLU_ALPHA = 1.702
SWIGLU_LIMIT = 7.0
EPS = 1e-5
NEG_BIG = -1e30

LANES = 128
HALO = 16
CONV_SUB = 128
SSD_CHUNKS_PER_STEP = 2
MERGE_CHUNK = 512
VMEM_LIMIT = 56 * 1024 * 1024

TOK_BLOCK = 512
PIECE = 16
FFN_TM = 512
_WORST_PIECES = (TOK_BLOCK * TOP_K + N_EXPERTS * (PIECE - 1) + PIECE - 1) // PIECE
_PIECES_PER_TILE = FFN_TM // PIECE
PIECES_PER_BLOCK = (_WORST_PIECES + _PIECES_PER_TILE - 1) // _PIECES_PER_TILE * _PIECES_PER_TILE
ROWS_PER_BLOCK = PIECES_PER_BLOCK * PIECE
DMA_UNROLL = 8
ITEM_NONE, ITEM_WRITE, ITEM_MERGE, ITEM_ZERO = 0, 1, 2, 3

COL_Z = 0
COL_XBC = D_INNER
COL_GATE = COL_XBC + CONV_DIM
COL_F = COL_GATE + 2 * D_MODEL
PROJ_MAIN = COL_F + D_F


def _cparams(sem):
    return pltpu.CompilerParams(dimension_semantics=sem, vmem_limit_bytes=VMEM_LIMIT)


def _inproj_kernel(x_ref, nw_ref, w_ref, wdt_ref, proj_ref, f_ref, dt_ref, hn_ref, *, n_main):
    j = pl.program_id(1)

    @pl.when(j == 0)
    def _():
        x = x_ref[...]
        ms = jnp.mean(x * x, axis=-1, keepdims=True)
        hn = (x * lax.rsqrt(ms + EPS) * nw_ref[...]).astype(BF16)
        hn_ref[...] = hn
        dt_ref[...] = jnp.dot(hn, wdt_ref[...], preferred_element_type=F32)

    def tile():
        return jnp.dot(hn_ref[...], w_ref[...], preferred_element_type=F32).astype(BF16)

    @pl.when(j < n_main)
    def _():
        proj_ref[...] = tile()

    @pl.when(j == n_main)
    def _():
        f_ref[...] = tile()


def _inproj(x, norm_w, w_main, w_dt, tm=2048, tn=1024):
    t = x.shape[0]
    tm = min(tm, t)
    assert tn == D_F and COL_F % tn == 0
    n_main = COL_F // tn
    return pl.pallas_call(
        functools.partial(_inproj_kernel, n_main=n_main),
        grid=(t // tm, n_main + 1),
        in_specs=[
            pl.BlockSpec((tm, D_MODEL), lambda i, j: (i, 0)),
            pl.BlockSpec((1, D_MODEL), lambda i, j: (0, 0)),
            pl.BlockSpec((D_MODEL, tn), lambda i, j: (0, j)),
            pl.BlockSpec((D_MODEL, 2 * LANES), lambda i, j: (0, 0)),
        ],
        out_specs=[
            pl.BlockSpec((tm, tn), lambda i, j: (i, jnp.minimum(j, n_main - 1))),
            pl.BlockSpec((tm, D_F), lambda i, j: (i, 0)),
            pl.BlockSpec((tm, 2 * LANES), lambda i, j: (i, 0)),
        ],
        out_shape=[
            jax.ShapeDtypeStruct((t, COL_F), BF16),
            jax.ShapeDtypeStruct((t, D_F), BF16),
            jax.ShapeDtypeStruct((t, 2 * LANES), F32),
        ],
        scratch_shapes=[pltpu.VMEM((tm, D_MODEL), BF16)],
        compiler_params=_cparams(("parallel", "arbitrary")),
        name="inproj",
    )(x, norm_w, w_main, w_dt)


def _dft_factors(seq):
    n1 = 1 << ((int(math.log2(seq)) + 1) // 2)
    return n1, seq // n1


def _dft_tables(seq):
    n1, n2 = _dft_factors(seq)
    two_pi = 2.0 * math.pi
    c = jnp.arange(FGROUP, dtype=jnp.int32)
    ph = ((c[:, None] * c[None, :]) % FGROUP).astype(F32) * (two_pi / FGROUP)
    cs_chan = jnp.concatenate([jnp.cos(ph), -jnp.sin(ph)], axis=1).astype(BF16)
    k1 = jnp.arange(n1, dtype=jnp.int32)
    nn = (n2 * jnp.arange(n1, dtype=jnp.int32)[None, :] + jnp.arange(n2, dtype=jnp.int32)[:, None])
    al = ((k1[None, :, None] * nn[:, None, :]) % seq).astype(F32) * (two_pi / seq)
    ca, sa = jnp.cos(al), jnp.sin(al)
    g1 = jnp.concatenate([jnp.concatenate([ca, sa], axis=2),
                          jnp.concatenate([-sa, ca], axis=2)], axis=1).astype(BF16)
    k2 = jnp.arange(n2, dtype=jnp.int32)
    be = ((k2[:, None] * k2[None, :]) % n2).astype(F32) * (two_pi / n2)
    g2 = jnp.concatenate([jnp.cos(be), jnp.sin(be)], axis=1).astype(BF16)
    return cs_chan, g1, g2


def _dft1_kernel(x_ref, cs_ref, g_ref, o_ref, *, tn2, n1):
    for j in range(tn2):
        x = x_ref[0, j]
        parts = []
        for q in range(D_F // FGROUP):
            uv = jnp.dot(x[:, q * FGROUP:(q + 1) * FGROUP], cs_ref[...],
                         preferred_element_type=F32).astype(BF16)
            parts.append(jnp.concatenate([uv[:, :FGROUP], uv[:, FGROUP:]], axis=0))
        z = jnp.concatenate(parts, axis=1)
        o_ref[0, j] = jnp.dot(g_ref[j], z, preferred_element_type=F32).astype(BF16)


def _dft2_kernel(a_ref, g_ref, o_ref, *, tk1, scale):
    for j in range(tk1):
        o_ref[0, j] = (jnp.dot(g_ref[...], a_ref[0, j], preferred_element_type=F32) * scale).astype(BF16)


def _fourier_mix(f_in, tables):
    bsz, seq, _ = f_in.shape
    n1, n2 = _dft_factors(seq)
    cs_chan, g1, g2 = tables
    xt = f_in.reshape(bsz, n1, n2, D_F).transpose(0, 2, 1, 3)
    tn2 = min(8, n2)
    stage1 = pl.pallas_call(
        functools.partial(_dft1_kernel, tn2=tn2, n1=n1),
        grid=(bsz, n2 // tn2),
        in_specs=[
            pl.BlockSpec((1, tn2, n1, D_F), lambda b, i: (b, i, 0, 0)),
            pl.BlockSpec((FGROUP, 2 * FGROUP), lambda b, i: (0, 0)),
            pl.BlockSpec((tn2, 2 * n1, 2 * n1), lambda b, i: (i, 0, 0)),
        ],
        out_specs=pl.BlockSpec((1, tn2, 2 * n1, D_F), lambda b, i: (b, i, 0, 0)),
        out_shape=jax.ShapeDtypeStruct((bsz, n2, 2 * n1, D_F), BF16),
        compiler_params=_cparams(("parallel", "parallel")),
        name="dft_stage1",
    )(xt, cs_chan, g1)
    a2 = (stage1.reshape(bsz, n2, 2, n1, D_F).transpose(0, 3, 2, 1, 4)
          .reshape(bsz, n1, 2 * n2, D_F))
    tk1 = min(8, n1)
    scale = 1.0 / math.sqrt(seq * FGROUP)
    stage2 = pl.pallas_call(
        functools.partial(_dft2_kernel, tk1=tk1, scale=scale),
        grid=(bsz, n1 // tk1),
        in_specs=[
            pl.BlockSpec((1, tk1, 2 * n2, D_F), lambda b, i: (b, i, 0, 0)),
            pl.BlockSpec((n2, 2 * n2), lambda b, i: (0, 0)),
        ],
        out_specs=pl.BlockSpec((1, tk1, n2, D_F), lambda b, i: (b, i, 0, 0)),
        out_shape=jax.ShapeDtypeStruct((bsz, n1, n2, D_F), BF16),
        compiler_params=_cparams(("parallel", "parallel")),
        name="dft_stage2",
    )(a2, g2)
    return stage2.transpose(0, 2, 1, 3).reshape(bsz, seq, D_F)


def _conv_kernel(prev_ref, main_ref, next_ref, w_ref, b_ref, o_ref, *, tl, tiles_per_seq):
    i = pl.program_id(0) % tiles_per_seq
    halo_zero = jnp.zeros(prev_ref.shape, BF16)
    prev = jnp.where(i == 0, halo_zero, prev_ref[...])
    nxt = jnp.where(i == tiles_per_seq - 1, halo_zero, next_ref[...])
    full = jnp.concatenate([prev, main_ref[...], nxt], axis=0)
    pad = D_CONV // 2
    win = CONV_SUB + 2 * HALO
    r = lax.broadcasted_iota(jnp.int32, (CONV_SUB, win), 0)
    c = lax.broadcasted_iota(jnp.int32, (CONV_SUB, win), 1)
    taps = [k for k in range(D_CONV) if k != pad]
    shift = jnp.concatenate([jnp.where(c == r + (HALO + k - pad), 1.0, 0.0) for k in taps], axis=1).astype(BF16)
    w_bf = w_ref[...].astype(BF16)
    for j in range(tl // CONV_SUB):
        window = full[j * CONV_SUB:j * CONV_SUB + win]
        stacked = jnp.concatenate([window * w_bf[k:k + 1, :] for k in taps], axis=0)
        acc = (b_ref[...] + w_ref[pad:pad + 1, :] * window[HALO:HALO + CONV_SUB].astype(F32)
               + jnp.dot(shift, stacked, preferred_element_type=F32))
        o_ref[j * CONV_SUB:(j + 1) * CONV_SUB, :] = (acc * jax.nn.sigmoid(acc)).astype(BF16)


def _conv_silu(proj, conv_w, conv_b, seq, tl=512, tc=2048):
    t = proj.shape[0]
    tl = min(tl, seq)
    tiles_per_seq = seq // tl
    cb0 = COL_XBC // tc
    hb = tl // HALO
    last_hb = t // HALO - 1
    return pl.pallas_call(
        functools.partial(_conv_kernel, tl=tl, tiles_per_seq=tiles_per_seq),
        grid=(t // tl, CONV_DIM // tc),
        in_specs=[
            pl.BlockSpec((HALO, tc), lambda i, c: (jnp.maximum(i * hb - 1, 0), cb0 + c)),
            pl.BlockSpec((tl, tc), lambda i, c: (i, cb0 + c)),
            pl.BlockSpec((HALO, tc), lambda i, c: (jnp.minimum((i + 1) * hb, last_hb), cb0 + c)),
            pl.BlockSpec((D_CONV, tc), lambda i, c: (0, c)),
            pl.BlockSpec((1, tc), lambda i, c: (0, c)),
        ],
        out_specs=pl.BlockSpec((tl, tc), lambda i, c: (i, c)),
        out_shape=jax.ShapeDtypeStruct((t, CONV_DIM), BF16),
        compiler_params=_cparams(("parallel", "parallel")),
        name="conv_silu",
    )(proj, proj, proj, conv_w, conv_b)


def _split3(v):
    hi = v.astype(BF16)
    r1 = v - hi.astype(F32)
    mid = r1.astype(BF16)
    lo = (r1 - mid.astype(F32)).astype(BF16)
    return [hi, mid, lo]


def _masked_sums(tri, vals):
    parts = []
    for v in vals:
        parts += _split3(v)
    out = jnp.dot(tri, jnp.concatenate(parts, axis=1), preferred_element_type=F32)
    return [out[:, (3 * i) * LANES:(3 * i + 1) * LANES] + out[:, (3 * i + 1) * LANES:(3 * i + 2) * LANES]
            + out[:, (3 * i + 2) * LANES:(3 * i + 3) * LANES] for i in range(len(vals))]


def _ssd_kernel(xs_ref, b_ref, c_ref, dtf_ref, dtb_ref, xs2_ref, b2_ref, c2_ref, dtb2_ref,
                biasf_ref, biasb_ref, alogf_ref, alogb_ref, exp_ref, dskip_ref,
                y1_ref, y2_ref, statef_ref, stateb_ref):
    @pl.when(pl.program_id(1) == 0)
    def _():
        statef_ref[...] = jnp.zeros(statef_ref.shape, F32)
        stateb_ref[...] = jnp.zeros(stateb_ref.shape, F32)

    for u in range(SSD_CHUNKS_PER_STEP):
        near = pl.ds(u * CHUNK, CHUNK)
        far = pl.ds((SSD_CHUNKS_PER_STEP - 1 - u) * CHUNK, CHUNK)
        _ssd_chunk(xs_ref.at[near], b_ref.at[near], c_ref.at[near], dtf_ref.at[near], dtb_ref.at[near],
                   xs2_ref.at[far], b2_ref.at[far], c2_ref.at[far], dtb2_ref.at[far],
                   biasf_ref, biasb_ref, alogf_ref, alogb_ref, exp_ref, dskip_ref,
                   y1_ref.at[near], y2_ref.at[far], statef_ref, stateb_ref)


def _ssd_chunk(xs_ref, b_ref, c_ref, dtf_ref, dtb_ref, xs2_ref, b2_ref, c2_ref, dtb2_ref,
               biasf_ref, biasb_ref, alogf_ref, alogb_ref, exp_ref, dskip_ref,
               y1_ref, y2_ref, statef_ref, stateb_ref):
    row = lax.broadcasted_iota(jnp.int32, (CHUNK, CHUNK), 0)
    col = lax.broadcasted_iota(jnp.int32, (CHUNK, CHUNK), 1)
    lower = row >= col
    diag = row == col
    tri_f = jnp.where(lower, 1.0, 0.0).astype(BF16)
    tri_b = jnp.where(row <= col, 1.0, 0.0).astype(BF16)

    af_neg = -jnp.exp(alogf_ref[...])
    ab_neg = -jnp.exp(alogb_ref[...])
    dt_f = jax.nn.softplus(dtf_ref[...] + biasf_ref[...])
    dt_b = jax.nn.softplus(dtb_ref[...] + biasb_ref[...])
    dt_b2 = jax.nn.softplus(dtb2_ref[...] + biasb_ref[...])
    (cf,) = _masked_sums(tri_f, [dt_f * af_neg])
    cb, cb2 = _masked_sums(tri_b, [dt_b * ab_neg, dt_b2 * ab_neg])
    tot_f = cf[CHUNK - 1:CHUNK, :]
    tot_b2 = cb2[0:1, :]
    dtb_t = dt_b.T
    srcf_t = (cf - jnp.log(dt_f)).T
    srcb_t = cb.T - jnp.log(dtb_t)

    dec = jnp.concatenate(_split3(jnp.exp(tot_f)) + _split3(jnp.exp(tot_b2))
                          + [jnp.zeros((PIECE - 6, LANES), BF16)], axis=0)
    stack = jnp.concatenate([(dt_f * jnp.exp(tot_f - cf)).astype(BF16), jnp.exp(cf).astype(BF16),
                             (dt_b2 * jnp.exp(tot_b2 - cb2)).astype(BF16), jnp.exp(cb2).astype(BF16), dec], axis=0)
    lane = lax.broadcasted_iota(jnp.int32, (CHUNK, LANES), 1)
    lo = lane < HEADDIM
    gw = HEADS_PER_GROUP * HEADDIM
    tn = (((0,), (0,)), ((), ()))
    d0 = 4 * CHUNK
    for g in range(N_GROUPS):
        gs = slice(g * gw, (g + 1) * gw)
        ns = slice(g * D_STATE, (g + 1) * D_STATE)
        ex = jnp.dot(stack, exp_ref[:, gs], preferred_element_type=F32)
        decf_e = ex[d0:d0 + 1] + ex[d0 + 1:d0 + 2] + ex[d0 + 2:d0 + 3]
        decb_e = ex[d0 + 3:d0 + 4] + ex[d0 + 4:d0 + 5] + ex[d0 + 5:d0 + 6]
        xd_f = (xs_ref[:, gs].astype(F32) * ex[0:CHUNK]).astype(BF16)
        xd_b = (xs2_ref[:, gs].astype(F32) * ex[2 * CHUNK:3 * CHUNK]).astype(BF16)
        bg, cg = b_ref[:, ns], c_ref[:, ns]
        cbm = lax.dot_general(cg, bg, (((1,), (1,)), ((), ())), preferred_element_type=F32)
        prev_f = statef_ref[g]
        y_off = jnp.dot(cg, prev_f.astype(BF16), preferred_element_type=F32) * ex[CHUNK:2 * CHUNK]
        statef_ref[g] = prev_f * decf_e + lax.dot_general(bg, xd_f, tn, preferred_element_type=F32)
        prev_b = stateb_ref[g]
        y2_ref[:, gs] = (jnp.dot(c2_ref[:, ns], prev_b.astype(BF16), preferred_element_type=F32)
                         * ex[3 * CHUNK:4 * CHUNK]).astype(BF16)
        stateb_ref[g] = prev_b * decb_e + lax.dot_general(b2_ref[:, ns], xd_b, tn, preferred_element_type=F32)
        for q in range(HEADS_PER_GROUP // 2):
            ms = []
            for hh in range(2):
                h = g * HEADS_PER_GROUP + 2 * q + hh
                seg = jnp.where(lower, cf[:, h:h + 1] - srcf_t[h:h + 1, :], cb[:, h:h + 1] - srcb_t[h:h + 1, :])
                wgt = jnp.exp(seg) + jnp.where(diag, dtb_t[h:h + 1, :], 0.0)
                ms.append((wgt * cbm).astype(BF16))
            lhs = jnp.concatenate(ms, axis=1)
            c0 = g * gw + 2 * q * HEADDIM
            xp = xs_ref[:, c0:c0 + LANES]
            zero = jnp.zeros_like(xp)
            rhs = jnp.concatenate([jnp.where(lo, xp, zero), jnp.where(lo, zero, xp)], axis=0)
            y = (jnp.dot(lhs, rhs, preferred_element_type=F32) + y_off[:, 2 * q * HEADDIM:2 * q * HEADDIM + LANES]
                 + dskip_ref[:, c0:c0 + LANES] * xp.astype(F32))
            y1_ref[:, c0:c0 + LANES] = y.astype(BF16)


def _ssd_scan(xbc, dt, p, seq):
    t = xbc.shape[0]
    rows = SSD_CHUNKS_PER_STEP * CHUNK
    assert seq % rows == 0
    ns = seq // rows
    near = lambda col: (lambda b, c: (b * ns + c, col))
    far = lambda col: (lambda b, c: (b * ns + ns - 1 - c, col))
    const = lambda b, c: (0, 0)
    gn = N_GROUPS * D_STATE
    state = pltpu.VMEM((N_GROUPS, D_STATE, HEADS_PER_GROUP * HEADDIM), F32)
    return pl.pallas_call(
        _ssd_kernel,
        grid=(t // seq, ns),
        in_specs=[
            pl.BlockSpec((rows, D_INNER), near(0)),
            pl.BlockSpec((rows, gn), near(D_INNER // gn)),
            pl.BlockSpec((rows, gn), near(D_INNER // gn + 1)),
            pl.BlockSpec((rows, LANES), near(0)),
            pl.BlockSpec((rows, LANES), near(1)),
            pl.BlockSpec((rows, D_INNER), far(0)),
            pl.BlockSpec((rows, gn), far(D_INNER // gn)),
            pl.BlockSpec((rows, gn), far(D_INNER // gn + 1)),
            pl.BlockSpec((rows, LANES), far(1)),
            pl.BlockSpec((1, LANES), const), pl.BlockSpec((1, LANES), const),
            pl.BlockSpec((1, LANES), const), pl.BlockSpec((1, LANES), const),
            pl.BlockSpec((LANES, D_INNER), const),
            pl.BlockSpec((1, D_INNER), const),
        ],
        out_specs=[pl.BlockSpec((rows, D_INNER), near(0)), pl.BlockSpec((rows, D_INNER), far(0))],
        out_shape=[jax.ShapeDtypeStruct((t, D_INNER), BF16), jax.ShapeDtypeStruct((t, D_INNER), BF16)],
        scratch_shapes=[state, state],
        compiler_params=_cparams(("parallel", "arbitrary")),
        name="ssd_scan",
    )(xbc, xbc, xbc, dt, dt, xbc, xbc, xbc, dt,
      p["dtb_f"], p["dtb_b"], p["alog_f"], p["alog_b"], p["expand"], p["d_skip"])


def _merge_kernel(y1_ref, y2_ref, z_ref, gate_ref, fm_ref, x_ref, snw_ref, wf_ref, ws_ref, wo_ref, fnw_ref,
                  wr_ref, br_ref, x1_ref, hn_ref, gates_ref, blk_ref):
    acc = None
    ssq = None
    for c0 in range(0, D_INNER, MERGE_CHUNK):
        cs = slice(c0, c0 + MERGE_CHUNK)
        z = z_ref[:, cs].astype(F32)
        yg = (y1_ref[:, cs].astype(F32) + y2_ref[:, cs].astype(F32)) * (z * jax.nn.sigmoid(z))
        sq = jnp.sum(yg * yg, axis=-1, keepdims=True)
        part = jnp.dot((yg * snw_ref[:, cs]).astype(BF16), ws_ref[cs, :], preferred_element_type=F32)
        acc = part if acc is None else acc + part
        ssq = sq if ssq is None else ssq + sq
    u_s = acc * lax.rsqrt(ssq * (1.0 / D_INNER) + EPS)
    x1 = x_ref[...]
    fm = fm_ref[...]
    for c0 in range(0, D_MODEL, MERGE_CHUNK):
        cs = slice(c0, c0 + MERGE_CHUNK)
        u_f = jnp.dot(fm, wf_ref[:, cs], preferred_element_type=F32)
        g_f = jax.nn.sigmoid(gate_ref[:, cs].astype(F32))
        g_s = jax.nn.sigmoid(gate_ref[:, D_MODEL + c0:D_MODEL + c0 + MERGE_CHUNK].astype(F32))
        merged = (g_f * u_f + g_s * u_s[:, cs]).astype(BF16)
        x1 = x1 + jnp.dot(merged, wo_ref[cs, :], preferred_element_type=F32)
    x1_ref[...] = x1
    ms1 = jnp.mean(x1 * x1, axis=-1, keepdims=True)
    hn = (x1 * lax.rsqrt(ms1 + EPS) * fnw_ref[...]).astype(BF16)
    hn_ref[...] = hn
    logits = jnp.dot(hn, wr_ref[...], preferred_element_type=F32) + br_ref[...]
    lane = lax.broadcasted_iota(jnp.int32, logits.shape, 1).astype(F32)
    tm = logits.shape[0]
    work = logits
    top = None
    denom = jnp.zeros((tm, 1), F32)
    route = jnp.zeros(logits.shape, F32)
    probs, onehots = [], []
    for k in range(TOP_K):
        m = jnp.max(work, axis=-1, keepdims=True)
        if k == 0:
            top = m
        first = jnp.min(jnp.where(work == m, lane, float(LANES)), axis=-1, keepdims=True)
        sel = lane == first
        onehots.append(jnp.where(sel, 1.0, 0.0))
        work = jnp.where(sel, NEG_BIG * 2, work)
        pk = jnp.exp(m - top)
        denom = denom + pk
        probs.append(pk)
        route = jnp.where(lane == float(k), first, route)
    inv = 1.0 / denom
    for k in range(TOP_K):
        route = jnp.where(lane == float(TOP_K + k), probs[k] * inv, route)
    ti = lax.broadcasted_iota(jnp.int32, (tm, tm), 0)
    tj = lax.broadcasted_iota(jnp.int32, (tm, tm), 1)
    earlier = jnp.where(tj < ti, 1.0, 0.0).astype(BF16)
    prefix = jnp.dot(earlier, jnp.concatenate(onehots, axis=1).astype(BF16), preferred_element_type=F32)
    cnts = [prefix[tm - 1:tm, k * LANES:(k + 1) * LANES] + onehots[k][tm - 1:tm, :] for k in range(TOP_K)]
    pieces = jnp.floor((cnts[0] + cnts[1] + cnts[2] + cnts[3] + (PIECE - 1.0)) * (1.0 / PIECE))
    ei = lax.broadcasted_iota(jnp.int32, (LANES, LANES), 0)
    ej = lax.broadcasted_iota(jnp.int32, (LANES, LANES), 1)
    lower_experts = jnp.where(ei < ej, 1.0, 0.0).astype(BF16)
    seg_start = jnp.dot(jnp.broadcast_to(pieces, (8, LANES)).astype(BF16), lower_experts,
                        preferred_element_type=F32)[0:1, :]
    base = seg_start * float(PIECE)
    for k in range(TOP_K):
        row_of = base + prefix[:, k * LANES:(k + 1) * LANES]
        pos = jnp.sum(onehots[k] * row_of, axis=-1, keepdims=True)
        route = jnp.where(lane == float(2 * TOP_K + k), pos, route)
        base = base + cnts[k]
    gates_ref[...] = route
    blk_ref[0] = jnp.concatenate([pieces, seg_start, jnp.zeros((6, LANES), F32)], axis=0)


def _merge_route(y1, y2, proj, f_mix, x, ssm_norm, w_fourier, w_ssm_out, w_out, norm_ffn, w_router, b_router):
    t = x.shape[0]
    tm = TOK_BLOCK
    assert t % tm == 0
    full = lambda r, c: pl.BlockSpec((r, c), lambda i: (0, 0))
    return pl.pallas_call(
        _merge_kernel,
        grid=(t // tm,),
        in_specs=[
            pl.BlockSpec((tm, D_INNER), lambda i: (i, 0)),
            pl.BlockSpec((tm, D_INNER), lambda i: (i, 0)),
            pl.BlockSpec((tm, D_INNER), lambda i: (i, COL_Z // D_INNER)),
            pl.BlockSpec((tm, 2 * D_MODEL), lambda i: (i, COL_GATE // (2 * D_MODEL))),
            pl.BlockSpec((tm, D_F), lambda i: (i, 0)),
            pl.BlockSpec((tm, D_MODEL), lambda i: (i, 0)),
            full(1, D_INNER), full(D_F, D_MODEL), full(D_INNER, D_MODEL), full(D_MODEL, D_MODEL),
            full(1, D_MODEL), full(D_MODEL, LANES), full(1, LANES),
        ],
        out_specs=[
            pl.BlockSpec((tm, D_MODEL), lambda i: (i, 0)),
            pl.BlockSpec((tm, D_MODEL), lambda i: (i, 0)),
            pl.BlockSpec((tm, LANES), lambda i: (i, 0)),
            pl.BlockSpec((1, 8, LANES), lambda i: (i, 0, 0)),
        ],
        out_shape=[
            jax.ShapeDtypeStruct((t, D_MODEL), F32),
            jax.ShapeDtypeStruct((t, D_MODEL), BF16),
            jax.ShapeDtypeStruct((t, LANES), F32),
            jax.ShapeDtypeStruct((t // tm, 8, LANES), F32),
        ],
        compiler_params=_cparams(("parallel",)),
        name="merge_route",
    )(y1, y2, proj, proj, f_mix, x, ssm_norm, w_fourier, w_ssm_out, w_out, norm_ffn, w_router, b_router)


def _route_tables(route, blk, t):
    i32 = jnp.int32
    nb = t // TOK_BLOCK
    pcs = blk[:, 0, :N_EXPERTS].astype(i32)
    seg_start = blk[:, 1, :N_EXPERTS].astype(i32)
    seg_end = seg_start + pcs
    used = seg_end[:, -1]
    per_e = jnp.sum(pcs, axis=0)
    g_end = jnp.cumsum(per_e)
    g_start = g_end - per_e
    b_prefix = jnp.cumsum(pcs, axis=0) - pcs
    j = jnp.arange(PIECES_PER_BLOCK, dtype=i32)[None, :, None]
    in_seg = jnp.logical_and(j >= seg_start[:, None, :], j < seg_end[:, None, :])
    shift = (g_start[None, :] + b_prefix - seg_start)[:, None, :]
    j2 = j[:, :, 0]
    dst_used = jnp.sum(jnp.where(in_seg, shift, 0), axis=2) + j2
    free = PIECES_PER_BLOCK - used
    dst_unused = g_end[-1] + (jnp.cumsum(free) - free)[:, None] + j2 - used[:, None]
    is_used = j2 < used[:, None]
    dst = jnp.where(is_used, dst_used, dst_unused).reshape(-1)
    src = jnp.where(is_used, dst_used, 0).reshape(-1)
    n_tiles = nb * ROWS_PER_BLOCK // FFN_TM
    n_items_max = n_tiles + N_EXPERTS
    start_rows, end_rows = g_start * PIECE, g_end * PIECE
    t0 = start_rows // FFN_TM
    n_it = jnp.where(end_rows > start_rows, (end_rows - 1) // FFN_TM - t0 + 1, 0)
    it_end = jnp.cumsum(n_it)
    it_start = it_end - n_it
    n_exp = it_end[-1]
    tiles_used = (end_rows[-1] + FFN_TM - 1) // FFN_TM
    q = jnp.arange(n_items_max, dtype=i32)
    qc = jnp.minimum(q, n_exp - 1)[:, None]
    mine = jnp.logical_and(qc >= it_start[None, :], qc < it_end[None, :])
    pick = lambda v: jnp.sum(jnp.where(mine, v[None, :], 0), axis=1)
    e_q = pick(jnp.arange(N_EXPERTS, dtype=i32))
    tile_exp = pick(t0 - it_start) + qc[:, 0]
    tile_q = jnp.where(q < n_exp, tile_exp, jnp.minimum(tiles_used + q - n_exp, n_tiles - 1))
    lo = jnp.maximum(pick(start_rows), tile_q * FFN_TM) - tile_q * FFN_TM
    hi = jnp.minimum(pick(end_rows), (tile_q + 1) * FFN_TM) - tile_q * FFN_TM
    new_tile = jnp.concatenate([jnp.ones((1,), i32), (tile_q[1:] != tile_q[:-1]).astype(i32)])
    kind = jnp.where(q < n_exp, jnp.where(new_tile == 1, ITEM_WRITE, ITEM_MERGE),
                     jnp.where(new_tile == 1, ITEM_ZERO, ITEM_NONE))
    pos_t = (route[:, 2 * TOP_K:3 * TOP_K].astype(i32).reshape(nb, TOK_BLOCK, TOP_K).transpose(0, 2, 1))
    return dict(pos_t=pos_t, dst=dst, src=src,
                item_e=e_q, item_tile=tile_q, item_lo=lo, item_hi=hi, item_kind=kind)


def _piece_copy_out(buf_ref, hbm_ref, sem, slot, j, dst_piece):
    return pltpu.make_async_copy(buf_ref.at[slot, j], hbm_ref.at[dst_piece], sem.at[slot])


def _piece_copy_in(hbm_ref, buf_ref, sem, slot, j, src_piece):
    return pltpu.make_async_copy(hbm_ref.at[src_piece], buf_ref.at[slot, j], sem.at[slot])


def _dispatch_kernel(dst_ref, hn_ref, post_ref, xs_hbm, buf_ref, sem, *, nb):
    b = pl.program_id(0)
    slot = b % 2

    def wait_slot(s):
        pltpu.make_async_copy(buf_ref.at[s], xs_hbm.at[pl.ds(0, PIECES_PER_BLOCK)], sem.at[s]).wait()

    @pl.when(b >= 2)
    def _():
        wait_slot(slot)

    r = lax.broadcasted_iota(jnp.int32, (ROWS_PER_BLOCK, TOK_BLOCK), 0)
    onehot = jnp.zeros((ROWS_PER_BLOCK, TOK_BLOCK), F32)
    for k in range(TOP_K):
        onehot = jnp.where(r == post_ref[0, k:k + 1, :], 1.0, onehot)
    sorted_rows = jnp.dot(onehot.astype(BF16), hn_ref[...], preferred_element_type=F32).astype(BF16)
    buf_ref[slot] = sorted_rows.reshape(PIECES_PER_BLOCK, PIECE, D_MODEL)

    def start(j, c):
        _piece_copy_out(buf_ref, xs_hbm, sem, slot, j, dst_ref[b * PIECES_PER_BLOCK + j]).start()
        return c
    lax.fori_loop(0, PIECES_PER_BLOCK, start, 0, unroll=DMA_UNROLL)

    @pl.when(b == nb - 1)
    def _():
        wait_slot(slot)
        if nb >= 2:
            wait_slot(1 - slot)


def _ffn_kernel(e_ref, tile_ref, lo_ref, hi_ref, kind_ref, x_ref, w1_ref, b1_ref, w2_ref, b2_ref, o_ref):
    q = pl.program_id(0)
    lo, hi, kind = lo_ref[q], hi_ref[q], kind_ref[q]

    def ffn():
        hu = jnp.dot(x_ref[...], w1_ref[0].astype(BF16), preferred_element_type=F32) + b1_ref[0]
        glu = jnp.minimum(hu[:, :D_FF], SWIGLU_LIMIT)
        lin = jnp.clip(hu[:, D_FF:], -SWIGLU_LIMIT, SWIGLU_LIMIT)
        act = (glu * jax.nn.sigmoid(SWIGLU_ALPHA * glu) * (lin + 1.0)).astype(BF16)
        return (jnp.dot(act, w2_ref[0].astype(BF16), preferred_element_type=F32) + b2_ref[0]).astype(BF16)

    @pl.when(kind == ITEM_WRITE)
    def _():
        o_ref[...] = ffn()

    @pl.when(kind == ITEM_MERGE)
    def _():
        row = lax.broadcasted_iota(jnp.int32, (FFN_TM, 1), 0)
        mine = jnp.logical_and(row >= lo, row < hi)
        o_ref[...] = jnp.where(mine, ffn(), o_ref[...])

    @pl.when(kind == ITEM_ZERO)
    def _():
        o_ref[...] = jnp.zeros(o_ref.shape, BF16)


def _combine_kernel(src_ref, route_ref, x1_ref, nf_ref, os_hbm, o_ref, buf_ref, sem, *, nb):
    b = pl.program_id(0)
    slot = b % 2

    def fetch(bb, s):
        def body(j, c):
            _piece_copy_in(os_hbm, buf_ref, sem, s, j, src_ref[bb * PIECES_PER_BLOCK + j]).start()
            return c
        lax.fori_loop(0, PIECES_PER_BLOCK, body, 0, unroll=DMA_UNROLL)

    @pl.when(b == 0)
    def _():
        fetch(0, 0)

    @pl.when(b + 1 < nb)
    def _():
        fetch(b + 1, 1 - slot)

    pltpu.make_async_copy(os_hbm.at[pl.ds(0, PIECES_PER_BLOCK)], buf_ref.at[slot], sem.at[slot]).wait()

    r = lax.broadcasted_iota(jnp.int32, (TOK_BLOCK, ROWS_PER_BLOCK), 1).astype(F32)
    route = route_ref[...]
    wmat = jnp.zeros((TOK_BLOCK, ROWS_PER_BLOCK), F32)
    for k in range(TOP_K):
        wmat = jnp.where(r == route[:, 2 * TOP_K + k:2 * TOP_K + k + 1], route[:, TOP_K + k:TOP_K + k + 1], wmat)
    y = jnp.dot(wmat.astype(BF16), buf_ref[slot].reshape(ROWS_PER_BLOCK, D_MODEL), preferred_element_type=F32)
    x2 = x1_ref[...] + y
    ms = jnp.mean(x2 * x2, axis=-1, keepdims=True)
    o_ref[...] = x2 * lax.rsqrt(ms + EPS) * nf_ref[...]


def _moe_routed(hn, route, blk, x1, w1, b1, w2, b2, norm_final):
    t = hn.shape[0]
    assert t % TOK_BLOCK == 0 and ROWS_PER_BLOCK % FFN_TM == 0
    nb = t // TOK_BLOCK
    rows = nb * ROWS_PER_BLOCK
    tb = _route_tables(route, blk, t)
    sorted_x = pl.pallas_call(
        functools.partial(_dispatch_kernel, nb=nb),
        grid_spec=pltpu.PrefetchScalarGridSpec(
            num_scalar_prefetch=1,
            grid=(nb,),
            in_specs=[
                pl.BlockSpec((TOK_BLOCK, D_MODEL), lambda b, d: (b, 0)),
                pl.BlockSpec((1, TOP_K, TOK_BLOCK), lambda b, d: (b, 0, 0)),
            ],
            out_specs=pl.BlockSpec(memory_space=pl.ANY),
            scratch_shapes=[pltpu.VMEM((2, PIECES_PER_BLOCK, PIECE, D_MODEL), BF16),
                            pltpu.SemaphoreType.DMA((2,))],
        ),
        out_shape=jax.ShapeDtypeStruct((rows // PIECE, PIECE, D_MODEL), BF16),
        compiler_params=_cparams(("arbitrary",)),
        name="moe_dispatch",
    )(tb["dst"], hn, tb["pos_t"]).reshape(rows, D_MODEL)
    n_items = rows // FFN_TM + N_EXPERTS
    sorted_o = pl.pallas_call(
        _ffn_kernel,
        grid_spec=pltpu.PrefetchScalarGridSpec(
            num_scalar_prefetch=5,
            grid=(n_items,),
            in_specs=[
                pl.BlockSpec((FFN_TM, D_MODEL), lambda q, e, ti, lo, hi, fi: (ti[q], 0)),
                pl.BlockSpec((1, D_MODEL, 2 * D_FF), lambda q, e, ti, lo, hi, fi: (e[q], 0, 0)),
                pl.BlockSpec((1, 1, 2 * D_FF), lambda q, e, ti, lo, hi, fi: (e[q], 0, 0)),
                pl.BlockSpec((1, D_FF, D_MODEL), lambda q, e, ti, lo, hi, fi: (e[q], 0, 0)),
                pl.BlockSpec((1, 1, D_MODEL), lambda q, e, ti, lo, hi, fi: (e[q], 0, 0)),
            ],
            out_specs=pl.BlockSpec((FFN_TM, D_MODEL), lambda q, e, ti, lo, hi, fi: (ti[q], 0)),
        ),
        out_shape=jax.ShapeDtypeStruct((rows, D_MODEL), BF16),
        compiler_params=_cparams(("arbitrary",)),
        name="moe_ffn",
    )(tb["item_e"], tb["item_tile"], tb["item_lo"], tb["item_hi"], tb["item_kind"], sorted_x, w1, b1, w2, b2)
    return pl.pallas_call(
        functools.partial(_combine_kernel, nb=nb),
        grid_spec=pltpu.PrefetchScalarGridSpec(
            num_scalar_prefetch=1,
            grid=(nb,),
            in_specs=[
                pl.BlockSpec((TOK_BLOCK, LANES), lambda b, s: (b, 0)),
                pl.BlockSpec((TOK_BLOCK, D_MODEL), lambda b, s: (b, 0)),
                pl.BlockSpec((1, D_MODEL), lambda b, s: (0, 0)),
                pl.BlockSpec(memory_space=pl.ANY),
            ],
            out_specs=pl.BlockSpec((TOK_BLOCK, D_MODEL), lambda b, s: (b, 0)),
            scratch_shapes=[pltpu.VMEM((2, PIECES_PER_BLOCK, PIECE, D_MODEL), BF16),
                            pltpu.SemaphoreType.DMA((2,))],
        ),
        out_shape=jax.ShapeDtypeStruct((t, D_MODEL), F32),
        compiler_params=_cparams(("arbitrary",)),
        name="moe_combine",
    )(tb["src"], route, x1, norm_final, sorted_o.reshape(rows // PIECE, PIECE, D_MODEL))


def _pad_lanes(v, fill=0.0):
    v = v.reshape(1, -1).astype(F32)
    return jnp.pad(v, ((0, 0), (0, LANES - v.shape[1])), constant_values=fill)


def _stream(x3, p):
    bsz, seq, _ = x3.shape
    x = x3.reshape(bsz * seq, D_MODEL)
    proj, f_in, dt = _inproj(x, p["norm_mix"], p["w_main"], p["w_dt"])
    f_mix = _fourier_mix(f_in.reshape(bsz, seq, D_F), _dft_tables(seq)).reshape(bsz * seq, D_F)
    xbc = _conv_silu(proj, p["conv_w"], p["conv_b"], seq)
    y1, y2 = _ssd_scan(xbc, dt, p, seq)
    x1, hn, route, blk = _merge_route(y1, y2, proj, f_mix, x, p["ssm_norm"], p["w_fourier"], p["w_ssm_out"],
                                 p["w_out"], p["norm_ffn"], p["w_router"], p["b_router"])
    out = _moe_routed(hn, route, blk, x1, p["w1"], p["b1"], p["w2"], p["b2"], p["norm_final"])
    return out.reshape(bsz, seq, D_MODEL)


def kernel(x_prompt, x_sample, norm_mix, w_in, conv_w, conv_b, dt_bias_fwd, dt_bias_bwd, a_log_fwd, a_log_bwd, d_skip, ssm_norm, w_fourier, w_ssm_out, w_out, norm_ffn, w_router, b_router, w_gate_up, b_gate_up, w_down, b_down, norm_final):
    assert norm_mix.shape[0] == 1, "single-layer block"
    w = w_in[0]
    o_z, o_xbc, o_dt, o_gate = D_F, D_F + D_INNER, D_F + D_INNER + CONV_DIM, D_F + D_INNER + CONV_DIM + 2 * N_HEADS
    w_main = jnp.concatenate([w[:, o_z:o_xbc], w[:, o_xbc:o_dt], w[:, o_gate:], w[:, :D_F]], axis=1).astype(BF16)
    pad_dt = lambda m: jnp.pad(m, ((0, 0), (0, LANES - N_HEADS))).astype(BF16)
    head_of_chan = jnp.arange(D_INNER, dtype=jnp.int32) // HEADDIM
    expand = (jnp.arange(LANES, dtype=jnp.int32)[:, None] == head_of_chan[None, :]).astype(BF16)
    p = dict(
        norm_mix=norm_mix[0].reshape(1, D_MODEL),
        w_main=w_main,
        w_dt=jnp.concatenate([pad_dt(w[:, o_dt:o_dt + N_HEADS]), pad_dt(w[:, o_dt + N_HEADS:o_gate])], axis=1),
        conv_w=conv_w[0], conv_b=conv_b[0].reshape(1, CONV_DIM),
        dtb_f=_pad_lanes(dt_bias_fwd[0]), dtb_b=_pad_lanes(dt_bias_bwd[0]),
        alog_f=_pad_lanes(a_log_fwd[0], NEG_BIG), alog_b=_pad_lanes(a_log_bwd[0], NEG_BIG),
        expand=expand,
        d_skip=jnp.repeat(d_skip[0].astype(F32), HEADDIM).reshape(1, D_INNER),
        ssm_norm=ssm_norm[0].reshape(1, D_INNER),
        w_fourier=w_fourier[0].astype(BF16), w_ssm_out=w_ssm_out[0].astype(BF16), w_out=w_out[0].astype(BF16),
        norm_ffn=norm_ffn[0].reshape(1, D_MODEL),
        w_router=jnp.pad(w_router[0], ((0, 0), (0, LANES - N_EXPERTS))).astype(BF16),
        b_router=_pad_lanes(b_router[0], NEG_BIG),
        w1=w_gate_up[0], b1=b_gate_up[0].reshape(N_EXPERTS, 1, 2 * D_FF),
        w2=w_down[0], b2=b_down[0].reshape(N_EXPERTS, 1, D_MODEL),
        norm_final=norm_final.reshape(1, D_MODEL),
    )
    return (_stream(x_prompt, p), _stream(x_sample, p))
```

```python
import functools
import math

import jax
import jax.numpy as jnp
import numpy as np
from jax import lax
from jax.experimental import pallas as pl
from jax.experimental.pallas import tpu as pltpu

F32 = jnp.float32
BF16 = jnp.bfloat16

D_MODEL = 1024
D_F = 1024
FGROUP = 256
D_INNER = 2048
HEADDIM = 64
N_HEADS = 32
N_GROUPS = 8
HEADS_PER_GROUP = N_HEADS // N_GROUPS
D_STATE = 128
D_CONV = 5
CHUNK = 128
CONV_DIM = D_INNER + 2 * N_GROUPS * D_STATE
N_EXPERTS = 32
TOP_K = 4
D_FF = 1024
SWIGLU_ALPHA = 1.702
SWIGLU_LIMIT = 7.0
EPS = 1e-5
NEG_BIG = -1e30

LANES = 128
HALO = 16
CONV_SUB = 128
SSD_CHUNKS_PER_STEP = 2
MERGE_CHUNK = 512
VMEM_LIMIT = 56 * 1024 * 1024

TOK_BLOCK = 512
PIECE = 16
FFN_TM = 512
_WORST_PIECES = (TOK_BLOCK * TOP_K + N_EXPERTS * (PIECE - 1) + PIECE - 1) // PIECE
_PIECES_PER_TILE = FFN_TM // PIECE
PIECES_PER_BLOCK = (_WORST_PIECES + _PIECES_PER_TILE - 1) // _PIECES_PER_TILE * _PIECES_PER_TILE
ROWS_PER_BLOCK = PIECES_PER_BLOCK * PIECE
DMA_UNROLL = 8
ITEM_NONE, ITEM_WRITE, ITEM_MERGE, ITEM_ZERO = 0, 1, 2, 3

COL_Z = 0
COL_XBC = D_INNER
COL_GATE = COL_XBC + CONV_DIM
COL_F = COL_GATE + 2 * D_MODEL
PROJ_MAIN = COL_F + D_F


def _cparams(sem):
    return pltpu.CompilerParams(dimension_semantics=sem, vmem_limit_bytes=VMEM_LIMIT)


def _inproj_kernel(x_ref, nw_ref, w_ref, wdt_ref, proj_ref, f_ref, dt_ref, hn_ref, *, n_main):
    j = pl.program_id(1)

    @pl.when(j == 0)
    def _():
        x = x_ref[...]
        ms = jnp.mean(x * x, axis=-1, keepdims=True)
        hn = (x * lax.rsqrt(ms + EPS) * nw_ref[...]).astype(BF16)
        hn_ref[...] = hn
        dt_ref[...] = jnp.dot(hn, wdt_ref[...], preferred_element_type=F32)

    def tile():
        return jnp.dot(hn_ref[...], w_ref[...], preferred_element_type=F32).astype(BF16)

    @pl.when(j < n_main)
    def _():
        proj_ref[...] = tile()

    @pl.when(j == n_main)
    def _():
        f_ref[...] = tile()


def _inproj(x, norm_w, w_main, w_dt, tm=2048, tn=1024):
    t = x.shape[0]
    tm = min(tm, t)
    assert tn == D_F and COL_F % tn == 0
    n_main = COL_F // tn
    return pl.pallas_call(
        functools.partial(_inproj_kernel, n_main=n_main),
        grid=(t // tm, n_main + 1),
        in_specs=[
            pl.BlockSpec((tm, D_MODEL), lambda i, j: (i, 0)),
            pl.BlockSpec((1, D_MODEL), lambda i, j: (0, 0)),
            pl.BlockSpec((D_MODEL, tn), lambda i, j: (0, j)),
            pl.BlockSpec((D_MODEL, 2 * LANES), lambda i, j: (0, 0)),
        ],
        out_specs=[
            pl.BlockSpec((tm, tn), lambda i, j: (i, jnp.minimum(j, n_main - 1))),
            pl.BlockSpec((tm, D_F), lambda i, j: (i, 0)),
            pl.BlockSpec((tm, 2 * LANES), lambda i, j: (i, 0)),
        ],
        out_shape=[
            jax.ShapeDtypeStruct((t, COL_F), BF16),
            jax.ShapeDtypeStruct((t, D_F), BF16),
            jax.ShapeDtypeStruct((t, 2 * LANES), F32),
        ],
        scratch_shapes=[pltpu.VMEM((tm, D_MODEL), BF16)],
        compiler_params=_cparams(("parallel", "arbitrary")),
        name="inproj",
    )(x, norm_w, w_main, w_dt)


def _dft_factors(seq):
    n1 = 1 << ((int(math.log2(seq)) + 1) // 2)
    return n1, seq // n1


def _dft_tables(seq):
    n1, n2 = _dft_factors(seq)
    two_pi = 2.0 * math.pi
    c = jnp.arange(FGROUP, dtype=jnp.int32)
    ph = ((c[:, None] * c[None, :]) % FGROUP).astype(F32) * (two_pi / FGROUP)
    cs_chan = jnp.concatenate([jnp.cos(ph), -jnp.sin(ph)], axis=1).astype(BF16)
    k1 = jnp.arange(n1, dtype=jnp.int32)
    nn = (n2 * jnp.arange(n1, dtype=jnp.int32)[None, :] + jnp.arange(n2, dtype=jnp.int32)[:, None])
    al = ((k1[None, :, None] * nn[:, None, :]) % seq).astype(F32) * (two_pi / seq)
    ca, sa = jnp.cos(al), jnp.sin(al)
    g1 = jnp.concatenate([jnp.concatenate([ca, sa], axis=2),
                          jnp.concatenate([-sa, ca], axis=2)], axis=1).astype(BF16)
    k2 = jnp.arange(n2, dtype=jnp.int32)
    be = ((k2[:, None] * k2[None, :]) % n2).astype(F32) * (two_pi / n2)
    g2 = jnp.concatenate([jnp.cos(be), jnp.sin(be)], axis=1).astype(BF16)
    return cs_chan, g1, g2


def _dft1_kernel(x_ref, cs_ref, g_ref, o_ref, *, tn2, n1):
    for j in range(tn2):
        x = x_ref[0, j]
        parts = []
        for q in range(D_F // FGROUP):
            uv = jnp.dot(x[:, q * FGROUP:(q + 1) * FGROUP], cs_ref[...],
                         preferred_element_type=F32).astype(BF16)
            parts.append(jnp.concatenate([uv[:, :FGROUP], uv[:, FGROUP:]], axis=0))
        z = jnp.concatenate(parts, axis=1)
        o_ref[0, j] = jnp.dot(g_ref[j], z, preferred_element_type=F32).astype(BF16)


def _dft2_kernel(a_ref, g_ref, o_ref, *, tk1, scale):
    for j in range(tk1):
        o_ref[0, j] = (jnp.dot(g_ref[...], a_ref[0, j], preferred_element_type=F32) * scale).astype(BF16)


def _fourier_mix(f_in, tables):
    bsz, seq, _ = f_in.shape
    n1, n2 = _dft_factors(seq)
    cs_chan, g1, g2 = tables
    xt = f_in.reshape(bsz, n1, n2, D_F).transpose(0, 2, 1, 3)
    tn2 = min(8, n2)
    stage1 = pl.pallas_call(
        functools.partial(_dft1_kernel, tn2=tn2, n1=n1),
        grid=(bsz, n2 // tn2),
        in_specs=[
            pl.BlockSpec((1, tn2, n1, D_F), lambda b, i: (b, i, 0, 0)),
            pl.BlockSpec((FGROUP, 2 * FGROUP), lambda b, i: (0, 0)),
            pl.BlockSpec((tn2, 2 * n1, 2 * n1), lambda b, i: (i, 0, 0)),
        ],
        out_specs=pl.BlockSpec((1, tn2, 2 * n1, D_F), lambda b, i: (b, i, 0, 0)),
        out_shape=jax.ShapeDtypeStruct((bsz, n2, 2 * n1, D_F), BF16),
        compiler_params=_cparams(("parallel", "parallel")),
        name="dft_stage1",
    )(xt, cs_chan, g1)
    a2 = (stage1.reshape(bsz, n2, 2, n1, D_F).transpose(0, 3, 2, 1, 4)
          .reshape(bsz, n1, 2 * n2, D_F))
    tk1 = min(8, n1)
    scale = 1.0 / math.sqrt(seq * FGROUP)
    stage2 = pl.pallas_call(
        functools.partial(_dft2_kernel, tk1=tk1, scale=scale),
        grid=(bsz, n1 // tk1),
        in_specs=[
            pl.BlockSpec((1, tk1, 2 * n2, D_F), lambda b, i: (b, i, 0, 0)),
            pl.BlockSpec((n2, 2 * n2), lambda b, i: (0, 0)),
        ],
        out_specs=pl.BlockSpec((1, tk1, n2, D_F), lambda b, i: (b, i, 0, 0)),
        out_shape=jax.ShapeDtypeStruct((bsz, n1, n2, D_F), BF16),
        compiler_params=_cparams(("parallel", "parallel")),
        name="dft_stage2",
    )(a2, g2)
    return stage2.transpose(0, 2, 1, 3).reshape(bsz, seq, D_F)


def _conv_kernel(prev_ref, main_ref, next_ref, w_ref, b_ref, o_ref, *, tl, tiles_per_seq):
    i = pl.program_id(0) % tiles_per_seq
    halo_zero = jnp.zeros(prev_ref.shape, BF16)
    prev = jnp.where(i == 0, halo_zero, prev_ref[...])
    nxt = jnp.where(i == tiles_per_seq - 1, halo_zero, next_ref[...])
    full = jnp.concatenate([prev, main_ref[...], nxt], axis=0)
    pad = D_CONV // 2
    win = CONV_SUB + 2 * HALO
    r = lax.broadcasted_iota(jnp.int32, (CONV_SUB, win), 0)
    c = lax.broadcasted_iota(jnp.int32, (CONV_SUB, win), 1)
    taps = [k for k in range(D_CONV) if k != pad]
    shift = jnp.concatenate([jnp.where(c == r + (HALO + k - pad), 1.0, 0.0) for k in taps], axis=1).astype(BF16)
    w_bf = w_ref[...].astype(BF16)
    for j in range(tl // CONV_SUB):
        window = full[j * CONV_SUB:j * CONV_SUB + win]
        stacked = jnp.concatenate([window * w_bf[k:k + 1, :] for k in taps], axis=0)
        acc = (b_ref[...] + w_ref[pad:pad + 1, :] * window[HALO:HALO + CONV_SUB].astype(F32)
               + jnp.dot(shift, stacked, preferred_element_type=F32))
        o_ref[j * CONV_SUB:(j + 1) * CONV_SUB, :] = (acc * jax.nn.sigmoid(acc)).astype(BF16)


def _conv_silu(proj, conv_w, conv_b, seq, tl=512, tc=2048):
    t = proj.shape[0]
    tl = min(tl, seq)
    tiles_per_seq = seq // tl
    cb0 = COL_XBC // tc
    hb = tl // HALO
    last_hb = t // HALO - 1
    return pl.pallas_call(
        functools.partial(_conv_kernel, tl=tl, tiles_per_seq=tiles_per_seq),
        grid=(t // tl, CONV_DIM // tc),
        in_specs=[
            pl.BlockSpec((HALO, tc), lambda i, c: (jnp.maximum(i * hb - 1, 0), cb0 + c)),
            pl.BlockSpec((tl, tc), lambda i, c: (i, cb0 + c)),
            pl.BlockSpec((HALO, tc), lambda i, c: (jnp.minimum((i + 1) * hb, last_hb), cb0 + c)),
            pl.BlockSpec((D_CONV, tc), lambda i, c: (0, c)),
            pl.BlockSpec((1, tc), lambda i, c: (0, c)),
        ],
        out_specs=pl.BlockSpec((tl, tc), lambda i, c: (i, c)),
        out_shape=jax.ShapeDtypeStruct((t, CONV_DIM), BF16),
        compiler_params=_cparams(("parallel", "parallel")),
        name="conv_silu",
    )(proj, proj, proj, conv_w, conv_b)


def _split3(v):
    hi = v.astype(BF16)
    r1 = v - hi.astype(F32)
    mid = r1.astype(BF16)
    lo = (r1 - mid.astype(F32)).astype(BF16)
    return [hi, mid, lo]


def _masked_sums(tri, vals):
    parts = []
    for v in vals:
        parts += _split3(v)
    out = jnp.dot(tri, jnp.concatenate(parts, axis=1), preferred_element_type=F32)
    return [out[:, (3 * i) * LANES:(3 * i + 1) * LANES] + out[:, (3 * i + 1) * LANES:(3 * i + 2) * LANES]
            + out[:, (3 * i + 2) * LANES:(3 * i + 3) * LANES] for i in range(len(vals))]


def _ssd_kernel(xs_ref, b_ref, c_ref, dtf_ref, dtb_ref, xs2_ref, b2_ref, c2_ref, dtb2_ref,
                biasf_ref, biasb_ref, alogf_ref, alogb_ref, exp_ref, dskip_ref,
                y1_ref, y2_ref, statef_ref, stateb_ref):
    @pl.when(pl.program_id(1) == 0)
    def _():
        statef_ref[...] = jnp.zeros(statef_ref.shape, F32)
        stateb_ref[...] = jnp.zeros(stateb_ref.shape, F32)

    for u in range(SSD_CHUNKS_PER_STEP):
        near = pl.ds(u * CHUNK, CHUNK)
        far = pl.ds((SSD_CHUNKS_PER_STEP - 1 - u) * CHUNK, CHUNK)
        _ssd_chunk(xs_ref.at[near], b_ref.at[near], c_ref.at[near], dtf_ref.at[near], dtb_ref.at[near],
                   xs2_ref.at[far], b2_ref.at[far], c2_ref.at[far], dtb2_ref.at[far],
                   biasf_ref, biasb_ref, alogf_ref, alogb_ref, exp_ref, dskip_ref,
                   y1_ref.at[near], y2_ref.at[far], statef_ref, stateb_ref)


def _ssd_chunk(xs_ref, b_ref, c_ref, dtf_ref, dtb_ref, xs2_ref, b2_ref, c2_ref, dtb2_ref,
               biasf_ref, biasb_ref, alogf_ref, alogb_ref, exp_ref, dskip_ref,
               y1_ref, y2_ref, statef_ref, stateb_ref):
    row = lax.broadcasted_iota(jnp.int32, (CHUNK, CHUNK), 0)
    col = lax.broadcasted_iota(jnp.int32, (CHUNK, CHUNK), 1)
    lower = row >= col
    diag = row == col
    tri_f = jnp.where(lower, 1.0, 0.0).astype(BF16)
    tri_b = jnp.where(row <= col, 1.0, 0.0).astype(BF16)

    af_neg = -jnp.exp(alogf_ref[...])
    ab_neg = -jnp.exp(alogb_ref[...])
    dt_f = jax.nn.softplus(dtf_ref[...] + biasf_ref[...])
    dt_b = jax.nn.softplus(dtb_ref[...] + biasb_ref[...])
    dt_b2 = jax.nn.softplus(dtb2_ref[...] + biasb_ref[...])
    (cf,) = _masked_sums(tri_f, [dt_f * af_neg])
    cb, cb2 = _masked_sums(tri_b, [dt_b * ab_neg, dt_b2 * ab_neg])
    tot_f = cf[CHUNK - 1:CHUNK, :]
    tot_b2 = cb2[0:1, :]
    dtb_t = dt_b.T
    srcf_t = (cf - jnp.log(dt_f)).T
    srcb_t = cb.T - jnp.log(dtb_t)

    dec = jnp.concatenate(_split3(jnp.exp(tot_f)) + _split3(jnp.exp(tot_b2))
                          + [jnp.zeros((PIECE - 6, LANES), BF16)], axis=0)
    stack = jnp.concatenate([(dt_f * jnp.exp(tot_f - cf)).astype(BF16), jnp.exp(cf).astype(BF16),
                             (dt_b2 * jnp.exp(tot_b2 - cb2)).astype(BF16), jnp.exp(cb2).astype(BF16), dec], axis=0)
    lane = lax.broadcasted_iota(jnp.int32, (CHUNK, LANES), 1)
    lo = lane < HEADDIM
    gw = HEADS_PER_GROUP * HEADDIM
    tn = (((0,), (0,)), ((), ()))
    d0 = 4 * CHUNK
    for g in range(N_GROUPS):
        gs = slice(g * gw, (g + 1) * gw)
        ns = slice(g * D_STATE, (g + 1) * D_STATE)
        ex = jnp.dot(stack, exp_ref[:, gs], preferred_element_type=F32)
        decf_e = ex[d0:d0 + 1] + ex[d0 + 1:d0 + 2] + ex[d0 + 2:d0 + 3]
        decb_e = ex[d0 + 3:d0 + 4] + ex[d0 + 4:d0 + 5] + ex[d0 + 5:d0 + 6]
        xd_f = (xs_ref[:, gs].astype(F32) * ex[0:CHUNK]).astype(BF16)
        xd_b = (xs2_ref[:, gs].astype(F32) * ex[2 * CHUNK:3 * CHUNK]).astype(BF16)
        bg, cg = b_ref[:, ns], c_ref[:, ns]
        cbm = lax.dot_general(cg, bg, (((1,), (1,)), ((), ())), preferred_element_type=F32)
        prev_f = statef_ref[g]
        y_off = jnp.dot(cg, prev_f.astype(BF16), preferred_element_type=F32) * ex[CHUNK:2 * CHUNK]
        statef_ref[g] = prev_f * decf_e + lax.dot_general(bg, xd_f, tn, preferred_element_type=F32)
        prev_b = stateb_ref[g]
        y2_ref[:, gs] = (jnp.dot(c2_ref[:, ns], prev_b.astype(BF16), preferred_element_type=F32)
                         * ex[3 * CHUNK:4 * CHUNK]).astype(BF16)
        stateb_ref[g] = prev_b * decb_e + lax.dot_general(b2_ref[:, ns], xd_b, tn, preferred_element_type=F32)
        for q in range(HEADS_PER_GROUP // 2):
            ms = []
            for hh in range(2):
                h = g * HEADS_PER_GROUP + 2 * q + hh
                seg = jnp.where(lower, cf[:, h:h + 1] - srcf_t[h:h + 1, :], cb[:, h:h + 1] - srcb_t[h:h + 1, :])
                wgt = jnp.exp(seg) + jnp.where(diag, dtb_t[h:h + 1, :], 0.0)
                ms.append((wgt * cbm).astype(BF16))
            lhs = jnp.concatenate(ms, axis=1)
            c0 = g * gw + 2 * q * HEADDIM
            xp = xs_ref[:, c0:c0 + LANES]
            zero = jnp.zeros_like(xp)
            rhs = jnp.concatenate([jnp.where(lo, xp, zero), jnp.where(lo, zero, xp)], axis=0)
            y = (jnp.dot(lhs, rhs, preferred_element_type=F32) + y_off[:, 2 * q * HEADDIM:2 * q * HEADDIM + LANES]
                 + dskip_ref[:, c0:c0 + LANES] * xp.astype(F32))
            y1_ref[:, c0:c0 + LANES] = y.astype(BF16)


def _ssd_scan(xbc, dt, p, seq):
    t = xbc.shape[0]
    rows = SSD_CHUNKS_PER_STEP * CHUNK
    assert seq % rows == 0
    ns = seq // rows
    near = lambda col: (lambda b, c: (b * ns + c, col))
    far = lambda col: (lambda b, c: (b * ns + ns - 1 - c, col))
    const = lambda b, c: (0, 0)
    gn = N_GROUPS * D_STATE
    state = pltpu.VMEM((N_GROUPS, D_STATE, HEADS_PER_GROUP * HEADDIM), F32)
    return pl.pallas_call(
        _ssd_kernel,
        grid=(t // seq, ns),
        in_specs=[
            pl.BlockSpec((rows, D_INNER), near(0)),
            pl.BlockSpec((rows, gn), near(D_INNER // gn)),
            pl.BlockSpec((rows, gn), near(D_INNER // gn + 1)),
            pl.BlockSpec((rows, LANES), near(0)),
            pl.BlockSpec((rows, LANES), near(1)),
            pl.BlockSpec((rows, D_INNER), far(0)),
            pl.BlockSpec((rows, gn), far(D_INNER // gn)),
            pl.BlockSpec((rows, gn), far(D_INNER // gn + 1)),
            pl.BlockSpec((rows, LANES), far(1)),
            pl.BlockSpec((1, LANES), const), pl.BlockSpec((1, LANES), const),
            pl.BlockSpec((1, LANES), const), pl.BlockSpec((1, LANES), const),
            pl.BlockSpec((LANES, D_INNER), const),
            pl.BlockSpec((1, D_INNER), const),
        ],
        out_specs=[pl.BlockSpec((rows, D_INNER), near(0)), pl.BlockSpec((rows, D_INNER), far(0))],
        out_shape=[jax.ShapeDtypeStruct((t, D_INNER), BF16), jax.ShapeDtypeStruct((t, D_INNER), BF16)],
        scratch_shapes=[state, state],
        compiler_params=_cparams(("parallel", "arbitrary")),
        name="ssd_scan",
    )(xbc, xbc, xbc, dt, dt, xbc, xbc, xbc, dt,
      p["dtb_f"], p["dtb_b"], p["alog_f"], p["alog_b"], p["expand"], p["d_skip"])


def _merge_kernel(y1_ref, y2_ref, z_ref, gate_ref, fm_ref, x_ref, snw_ref, wf_ref, ws_ref, wo_ref, fnw_ref,
                  wr_ref, br_ref, x1_ref, hn_ref, gates_ref, blk_ref):
    acc = None
    ssq = None
    for c0 in range(0, D_INNER, MERGE_CHUNK):
        cs = slice(c0, c0 + MERGE_CHUNK)
        z = z_ref[:, cs].astype(F32)
        yg = (y1_ref[:, cs].astype(F32) + y2_ref[:, cs].astype(F32)) * (z * jax.nn.sigmoid(z))
        sq = jnp.sum(yg * yg, axis=-1, keepdims=True)
        part = jnp.dot((yg * snw_ref[:, cs]).astype(BF16), ws_ref[cs, :], preferred_element_type=F32)
        acc = part if acc is None else acc + part
        ssq = sq if ssq is None else ssq + sq
    u_s = acc * lax.rsqrt(ssq * (1.0 / D_INNER) + EPS)
    x1 = x_ref[...]
    fm = fm_ref[...]
    for c0 in range(0, D_MODEL, MERGE_CHUNK):
        cs = slice(c0, c0 + MERGE_CHUNK)
        u_f = jnp.dot(fm, wf_ref[:, cs], preferred_element_type=F32)
        g_f = jax.nn.sigmoid(gate_ref[:, cs].astype(F32))
        g_s = jax.nn.sigmoid(gate_ref[:, D_MODEL + c0:D_MODEL + c0 + MERGE_CHUNK].astype(F32))
        merged = (g_f * u_f + g_s * u_s[:, cs]).astype(BF16)
        x1 = x1 + jnp.dot(merged, wo_ref[cs, :], preferred_element_type=F32)
    x1_ref[...] = x1
    ms1 = jnp.mean(x1 * x1, axis=-1, keepdims=True)
    hn = (x1 * lax.rsqrt(ms1 + EPS) * fnw_ref[...]).astype(BF16)
    hn_ref[...] = hn
    logits = jnp.dot(hn, wr_ref[...], preferred_element_type=F32) + br_ref[...]
    lane = lax.broadcasted_iota(jnp.int32, logits.shape, 1).astype(F32)
    tm = logits.shape[0]
    work = logits
    top = None
    denom = jnp.zeros((tm, 1), F32)
    route = jnp.zeros(logits.shape, F32)
    probs, onehots = [], []
    for k in range(TOP_K):
        m = jnp.max(work, axis=-1, keepdims=True)
        if k == 0:
            top = m
        first = jnp.min(jnp.where(work == m, lane, float(LANES)), axis=-1, keepdims=True)
        sel = lane == first
        onehots.append(jnp.where(sel, 1.0, 0.0))
        work = jnp.where(sel, NEG_BIG * 2, work)
        pk = jnp.exp(m - top)
        denom = denom + pk
        probs.append(pk)
        route = jnp.where(lane == float(k), first, route)
    inv = 1.0 / denom
    for k in range(TOP_K):
        route = jnp.where(lane == float(TOP_K + k), probs[k] * inv, route)
    ti = lax.broadcasted_iota(jnp.int32, (tm, tm), 0)
    tj = lax.broadcasted_iota(jnp.int32, (tm, tm), 1)
    earlier = jnp.where(tj < ti, 1.0, 0.0).astype(BF16)
    prefix = jnp.dot(earlier, jnp.concatenate(onehots, axis=1).astype(BF16), preferred_element_type=F32)
    cnts = [prefix[tm - 1:tm, k * LANES:(k + 1) * LANES] + onehots[k][tm - 1:tm, :] for k in range(TOP_K)]
    pieces = jnp.floor((cnts[0] + cnts[1] + cnts[2] + cnts[3] + (PIECE - 1.0)) * (1.0 / PIECE))
    ei = lax.broadcasted_iota(jnp.int32, (LANES, LANES), 0)
    ej = lax.broadcasted_iota(jnp.int32, (LANES, LANES), 1)
    lower_experts = jnp.where(ei < ej, 1.0, 0.0).astype(BF16)
    seg_start = jnp.dot(jnp.broadcast_to(pieces, (8, LANES)).astype(BF16), lower_experts,
                        preferred_element_type=F32)[0:1, :]
    base = seg_start * float(PIECE)
    for k in range(TOP_K):
        row_of = base + prefix[:, k * LANES:(k + 1) * LANES]
        pos = jnp.sum(onehots[k] * row_of, axis=-1, keepdims=True)
        route = jnp.where(lane == float(2 * TOP_K + k), pos, route)
        base = base + cnts[k]
    gates_ref[...] = route
    blk_ref[0] = jnp.concatenate([pieces, seg_start, jnp.zeros((6, LANES), F32)], axis=0)


def _merge_route(y1, y2, proj, f_mix, x, ssm_norm, w_fourier, w_ssm_out, w_out, norm_ffn, w_router, b_router):
    t = x.shape[0]
    tm = TOK_BLOCK
    assert t % tm == 0
    full = lambda r, c: pl.BlockSpec((r, c), lambda i: (0, 0))
    return pl.pallas_call(
        _merge_kernel,
        grid=(t // tm,),
        in_specs=[
            pl.BlockSpec((tm, D_INNER), lambda i: (i, 0)),
            pl.BlockSpec((tm, D_INNER), lambda i: (i, 0)),
            pl.BlockSpec((tm, D_INNER), lambda i: (i, COL_Z // D_INNER)),
            pl.BlockSpec((tm, 2 * D_MODEL), lambda i: (i, COL_GATE // (2 * D_MODEL))),
            pl.BlockSpec((tm, D_F), lambda i: (i, 0)),
            pl.BlockSpec((tm, D_MODEL), lambda i: (i, 0)),
            full(1, D_INNER), full(D_F, D_MODEL), full(D_INNER, D_MODEL), full(D_MODEL, D_MODEL),
            full(1, D_MODEL), full(D_MODEL, LANES), full(1, LANES),
        ],
        out_specs=[
            pl.BlockSpec((tm, D_MODEL), lambda i: (i, 0)),
            pl.BlockSpec((tm, D_MODEL), lambda i: (i, 0)),
            pl.BlockSpec((tm, LANES), lambda i: (i, 0)),
            pl.BlockSpec((1, 8, LANES), lambda i: (i, 0, 0)),
        ],
        out_shape=[
            jax.ShapeDtypeStruct((t, D_MODEL), F32),
            jax.ShapeDtypeStruct((t, D_MODEL), BF16),
            jax.ShapeDtypeStruct((t, LANES), F32),
            jax.ShapeDtypeStruct((t // tm, 8, LANES), F32),
        ],
        compiler_params=_cparams(("parallel",)),
        name="merge_route",
    )(y1, y2, proj, proj, f_mix, x, ssm_norm, w_fourier, w_ssm_out, w_out, norm_ffn, w_router, b_router)


def _route_tables(pos, blk):
    i32 = jnp.int32
    nb = blk.shape[0]
    pcs = blk[:, 0, :N_EXPERTS].astype(i32)
    seg_start = blk[:, 1, :N_EXPERTS].astype(i32)
    seg_end = seg_start + pcs
    used = seg_end[:, -1]
    per_e = jnp.sum(pcs, axis=0)
    g_end = jnp.cumsum(per_e)
    g_start = g_end - per_e
    b_prefix = jnp.cumsum(pcs, axis=0) - pcs
    j = jnp.arange(PIECES_PER_BLOCK, dtype=i32)[None, :, None]
    in_seg = jnp.logical_and(j >= seg_start[:, None, :], j < seg_end[:, None, :])
    shift = (g_start[None, :] + b_prefix - seg_start)[:, None, :]
    j2 = j[:, :, 0]
    dst_used = jnp.sum(jnp.where(in_seg, shift, 0), axis=2) + j2
    free = PIECES_PER_BLOCK - used
    dst_unused = g_end[-1] + (jnp.cumsum(free) - free)[:, None] + j2 - used[:, None]
    is_used = j2 < used[:, None]
    dst = jnp.where(is_used, dst_used, dst_unused).reshape(-1)
    src = jnp.where(is_used, dst_used, 0).reshape(-1)
    n_tiles = nb * ROWS_PER_BLOCK // FFN_TM
    n_items_max = n_tiles + N_EXPERTS
    start_rows, end_rows = g_start * PIECE, g_end * PIECE
    t0 = start_rows // FFN_TM
    n_it = jnp.where(end_rows > start_rows, (end_rows - 1) // FFN_TM - t0 + 1, 0)
    it_end = jnp.cumsum(n_it)
    it_start = it_end - n_it
    n_exp = it_end[-1]
    tiles_used = (end_rows[-1] + FFN_TM - 1) // FFN_TM
    q = jnp.arange(n_items_max, dtype=i32)
    qc = jnp.minimum(q, n_exp - 1)[:, None]
    mine = jnp.logical_and(qc >= it_start[None, :], qc < it_end[None, :])
    pick = lambda v: jnp.sum(jnp.where(mine, v[None, :], 0), axis=1)
    e_q = pick(jnp.arange(N_EXPERTS, dtype=i32))
    tile_exp = pick(t0 - it_start) + qc[:, 0]
    tile_q = jnp.where(q < n_exp, tile_exp, jnp.minimum(tiles_used + q - n_exp, n_tiles - 1))
    lo = jnp.maximum(pick(start_rows), tile_q * FFN_TM) - tile_q * FFN_TM
    hi = jnp.minimum(pick(end_rows), (tile_q + 1) * FFN_TM) - tile_q * FFN_TM
    new_tile = jnp.concatenate([jnp.ones((1,), i32), (tile_q[1:] != tile_q[:-1]).astype(i32)])
    kind = jnp.where(q < n_exp, jnp.where(new_tile == 1, ITEM_WRITE, ITEM_MERGE),
                     jnp.where(new_tile == 1, ITEM_ZERO, ITEM_NONE))
    pos_t = pos.astype(i32).reshape(nb, TOK_BLOCK, TOP_K).transpose(0, 2, 1)
    return dict(pos_t=pos_t, dst=dst, src=src,
                item_e=e_q, item_tile=tile_q, item_lo=lo, item_hi=hi, item_kind=kind)


def _piece_copy_out(buf_ref, hbm_ref, sem, slot, j, dst_piece):
    return pltpu.make_async_copy(buf_ref.at[slot, j], hbm_ref.at[dst_piece], sem.at[slot])


def _piece_copy_in(hbm_ref, buf_ref, sem, slot, j, src_piece):
    return pltpu.make_async_copy(hbm_ref.at[src_piece], buf_ref.at[slot, j], sem.at[slot])


def _dispatch_kernel(dst_ref, hna_ref, hnb_ref, post_ref, xs_hbm, buf_ref, sem, *, nb, nb_a):
    b = pl.program_id(0)
    slot = b % 2
    hn = jnp.where(b < nb_a, hna_ref[...], hnb_ref[...])

    def wait_slot(s):
        pltpu.make_async_copy(buf_ref.at[s], xs_hbm.at[pl.ds(0, PIECES_PER_BLOCK)], sem.at[s]).wait()

    @pl.when(b >= 2)
    def _():
        wait_slot(slot)

    r = lax.broadcasted_iota(jnp.int32, (ROWS_PER_BLOCK, TOK_BLOCK), 0)
    onehot = jnp.zeros((ROWS_PER_BLOCK, TOK_BLOCK), F32)
    for k in range(TOP_K):
        onehot = jnp.where(r == post_ref[0, k:k + 1, :], 1.0, onehot)
    sorted_rows = jnp.dot(onehot.astype(BF16), hn, preferred_element_type=F32).astype(BF16)
    buf_ref[slot] = sorted_rows.reshape(PIECES_PER_BLOCK, PIECE, D_MODEL)

    def start(j, c):
        _piece_copy_out(buf_ref, xs_hbm, sem, slot, j, dst_ref[b * PIECES_PER_BLOCK + j]).start()
        return c
    lax.fori_loop(0, PIECES_PER_BLOCK, start, 0, unroll=DMA_UNROLL)

    @pl.when(b == nb - 1)
    def _():
        wait_slot(slot)
        if nb >= 2:
            wait_slot(1 - slot)


def _ffn_kernel(e_ref, tile_ref, lo_ref, hi_ref, kind_ref, x_ref, w1_ref, b1_ref, w2_ref, b2_ref, o_ref):
    q = pl.program_id(0)
    lo, hi, kind = lo_ref[q], hi_ref[q], kind_ref[q]

    def ffn():
        hu = jnp.dot(x_ref[...], w1_ref[0].astype(BF16), preferred_element_type=F32) + b1_ref[0]
        glu = jnp.minimum(hu[:, :D_FF], SWIGLU_LIMIT)
        lin = jnp.clip(hu[:, D_FF:], -SWIGLU_LIMIT, SWIGLU_LIMIT)
        act = (glu * jax.nn.sigmoid(SWIGLU_ALPHA * glu) * (lin + 1.0)).astype(BF16)
        return (jnp.dot(act, w2_ref[0].astype(BF16), preferred_element_type=F32) + b2_ref[0]).astype(BF16)

    @pl.when(kind == ITEM_WRITE)
    def _():
        o_ref[...] = ffn()

    @pl.when(kind == ITEM_MERGE)
    def _():
        row = lax.broadcasted_iota(jnp.int32, (FFN_TM, 1), 0)
        mine = jnp.logical_and(row >= lo, row < hi)
        o_ref[...] = jnp.where(mine, ffn(), o_ref[...])

    @pl.when(kind == ITEM_ZERO)
    def _():
        o_ref[...] = jnp.zeros(o_ref.shape, BF16)


def _combine_kernel(src_ref, route_ref, x1_ref, nf_ref, os_hbm, o_ref, buf_ref, sem, *, nb, blk0):
    b = pl.program_id(0)
    slot = b % 2

    def fetch(bb, s):
        def body(j, c):
            _piece_copy_in(os_hbm, buf_ref, sem, s, j, src_ref[(blk0 + bb) * PIECES_PER_BLOCK + j]).start()
            return c
        lax.fori_loop(0, PIECES_PER_BLOCK, body, 0, unroll=DMA_UNROLL)

    @pl.when(b == 0)
    def _():
        fetch(0, 0)

    @pl.when(b + 1 < nb)
    def _():
        fetch(b + 1, 1 - slot)

    pltpu.make_async_copy(os_hbm.at[pl.ds(0, PIECES_PER_BLOCK)], buf_ref.at[slot], sem.at[slot]).wait()

    r = lax.broadcasted_iota(jnp.int32, (TOK_BLOCK, ROWS_PER_BLOCK), 1).astype(F32)
    route = route_ref[...]
    wmat = jnp.zeros((TOK_BLOCK, ROWS_PER_BLOCK), F32)
    for k in range(TOP_K):
        wmat = jnp.where(r == route[:, 2 * TOP_K + k:2 * TOP_K + k + 1], route[:, TOP_K + k:TOP_K + k + 1], wmat)
    y = jnp.dot(wmat.astype(BF16), buf_ref[slot].reshape(ROWS_PER_BLOCK, D_MODEL), preferred_element_type=F32)
    x2 = x1_ref[...] + y
    ms = jnp.mean(x2 * x2, axis=-1, keepdims=True)
    o_ref[...] = x2 * lax.rsqrt(ms + EPS) * nf_ref[...]


def _moe_routed(streams, w1, b1, w2, b2, norm_final):
    (hn_a, route_a, blk_a, _), (hn_b, route_b, blk_b, _) = streams
    assert ROWS_PER_BLOCK % FFN_TM == 0
    nb_a, nb_b = blk_a.shape[0], blk_b.shape[0]
    nb = nb_a + nb_b
    rows = nb * ROWS_PER_BLOCK
    pos = jnp.concatenate([route_a[:, 2 * TOP_K:3 * TOP_K], route_b[:, 2 * TOP_K:3 * TOP_K]], axis=0)
    tb = _route_tables(pos, jnp.concatenate([blk_a, blk_b], axis=0))
    sorted_x = pl.pallas_call(
        functools.partial(_dispatch_kernel, nb=nb, nb_a=nb_a),
        grid_spec=pltpu.PrefetchScalarGridSpec(
            num_scalar_prefetch=1,
            grid=(nb,),
            in_specs=[
                pl.BlockSpec((TOK_BLOCK, D_MODEL), lambda b, d: (jnp.minimum(b, nb_a - 1), 0)),
                pl.BlockSpec((TOK_BLOCK, D_MODEL), lambda b, d: (jnp.maximum(b - nb_a, 0), 0)),
                pl.BlockSpec((1, TOP_K, TOK_BLOCK), lambda b, d: (b, 0, 0)),
            ],
            out_specs=pl.BlockSpec(memory_space=pl.ANY),
            scratch_shapes=[pltpu.VMEM((2, PIECES_PER_BLOCK, PIECE, D_MODEL), BF16),
                            pltpu.SemaphoreType.DMA((2,))],
        ),
        out_shape=jax.ShapeDtypeStruct((rows // PIECE, PIECE, D_MODEL), BF16),
        compiler_params=_cparams(("arbitrary",)),
        name="moe_dispatch",
    )(tb["dst"], hn_a, hn_b, tb["pos_t"]).reshape(rows, D_MODEL)
    n_items = rows // FFN_TM + N_EXPERTS
    sorted_o = pl.pallas_call(
        _ffn_kernel,
        grid_spec=pltpu.PrefetchScalarGridSpec(
            num_scalar_prefetch=5,
            grid=(n_items,),
            in_specs=[
                pl.BlockSpec((FFN_TM, D_MODEL), lambda q, e, ti, lo, hi, fi: (ti[q], 0)),
                pl.BlockSpec((1, D_MODEL, 2 * D_FF), lambda q, e, ti, lo, hi, fi: (e[q], 0, 0)),
                pl.BlockSpec((1, 1, 2 * D_FF), lambda q, e, ti, lo, hi, fi: (e[q], 0, 0)),
                pl.BlockSpec((1, D_FF, D_MODEL), lambda q, e, ti, lo, hi, fi: (e[q], 0, 0)),
                pl.BlockSpec((1, 1, D_MODEL), lambda q, e, ti, lo, hi, fi: (e[q], 0, 0)),
            ],
            out_specs=pl.BlockSpec((FFN_TM, D_MODEL), lambda q, e, ti, lo, hi, fi: (ti[q], 0)),
        ),
        out_shape=jax.ShapeDtypeStruct((rows, D_MODEL), BF16),
        compiler_params=_cparams(("arbitrary",)),
        name="moe_ffn",
    )(tb["item_e"], tb["item_tile"], tb["item_lo"], tb["item_hi"], tb["item_kind"], sorted_x, w1, b1, w2, b2)
    sorted_o = sorted_o.reshape(rows // PIECE, PIECE, D_MODEL)
    outs = []
    blk0 = 0
    for _, route, blk, x1 in streams:
        nb_s = blk.shape[0]
        outs.append(pl.pallas_call(
            functools.partial(_combine_kernel, nb=nb_s, blk0=blk0),
            grid_spec=pltpu.PrefetchScalarGridSpec(
                num_scalar_prefetch=1,
                grid=(nb_s,),
                in_specs=[
                    pl.BlockSpec((TOK_BLOCK, LANES), lambda b, s: (b, 0)),
                    pl.BlockSpec((TOK_BLOCK, D_MODEL), lambda b, s: (b, 0)),
                    pl.BlockSpec((1, D_MODEL), lambda b, s: (0, 0)),
                    pl.BlockSpec(memory_space=pl.ANY),
                ],
                out_specs=pl.BlockSpec((TOK_BLOCK, D_MODEL), lambda b, s: (b, 0)),
                scratch_shapes=[pltpu.VMEM((2, PIECES_PER_BLOCK, PIECE, D_MODEL), BF16),
                                pltpu.SemaphoreType.DMA((2,))],
            ),
            out_shape=jax.ShapeDtypeStruct((nb_s * TOK_BLOCK, D_MODEL), F32),
            compiler_params=_cparams(("arbitrary",)),
            name="moe_combine",
        )(tb["src"], route, x1, norm_final, sorted_o))
        blk0 += nb_s
    return outs


def _pad_lanes(v, fill=0.0):
    v = v.reshape(1, -1).astype(F32)
    return jnp.pad(v, ((0, 0), (0, LANES - v.shape[1])), constant_values=fill)


def _mixer(x3, p):
    bsz, seq, _ = x3.shape
    x = x3.reshape(bsz * seq, D_MODEL)
    proj, f_in, dt = _inproj(x, p["norm_mix"], p["w_main"], p["w_dt"])
    f_mix = _fourier_mix(f_in.reshape(bsz, seq, D_F), _dft_tables(seq)).reshape(bsz * seq, D_F)
    xbc = _conv_silu(proj, p["conv_w"], p["conv_b"], seq)
    y1, y2 = _ssd_scan(xbc, dt, p, seq)
    x1, hn, route, blk = _merge_route(y1, y2, proj, f_mix, x, p["ssm_norm"], p["w_fourier"], p["w_ssm_out"],
                                 p["w_out"], p["norm_ffn"], p["w_router"], p["b_router"])
    return hn, route, blk, x1


def kernel(x_prompt, x_sample, norm_mix, w_in, conv_w, conv_b, dt_bias_fwd, dt_bias_bwd, a_log_fwd, a_log_bwd, d_skip, ssm_norm, w_fourier, w_ssm_out, w_out, norm_ffn, w_router, b_router, w_gate_up, b_gate_up, w_down, b_down, norm_final):
    assert norm_mix.shape[0] == 1, "single-layer block"
    w = w_in[0]
    o_z, o_xbc, o_dt, o_gate = D_F, D_F + D_INNER, D_F + D_INNER + CONV_DIM, D_F + D_INNER + CONV_DIM + 2 * N_HEADS
    w_main = jnp.concatenate([w[:, o_z:o_xbc], w[:, o_xbc:o_dt], w[:, o_gate:], w[:, :D_F]], axis=1).astype(BF16)
    pad_dt = lambda m: jnp.pad(m, ((0, 0), (0, LANES - N_HEADS))).astype(BF16)
    head_of_chan = jnp.arange(D_INNER, dtype=jnp.int32) // HEADDIM
    expand = (jnp.arange(LANES, dtype=jnp.int32)[:, None] == head_of_chan[None, :]).astype(BF16)
    p = dict(
        norm_mix=norm_mix[0].reshape(1, D_MODEL),
        w_main=w_main,
        w_dt=jnp.concatenate([pad_dt(w[:, o_dt:o_dt + N_HEADS]), pad_dt(w[:, o_dt + N_HEADS:o_gate])], axis=1),
        conv_w=conv_w[0], conv_b=conv_b[0].reshape(1, CONV_DIM),
        dtb_f=_pad_lanes(dt_bias_fwd[0]), dtb_b=_pad_lanes(dt_bias_bwd[0]),
        alog_f=_pad_lanes(a_log_fwd[0], NEG_BIG), alog_b=_pad_lanes(a_log_bwd[0], NEG_BIG),
        expand=expand,
        d_skip=jnp.repeat(d_skip[0].astype(F32), HEADDIM).reshape(1, D_INNER),
        ssm_norm=ssm_norm[0].reshape(1, D_INNER),
        w_fourier=w_fourier[0].astype(BF16), w_ssm_out=w_ssm_out[0].astype(BF16), w_out=w_out[0].astype(BF16),
        norm_ffn=norm_ffn[0].reshape(1, D_MODEL),
        w_router=jnp.pad(w_router[0], ((0, 0), (0, LANES - N_EXPERTS))).astype(BF16),
        b_router=_pad_lanes(b_router[0], NEG_BIG),
        w1=w_gate_up[0], b1=b_gate_up[0].reshape(N_EXPERTS, 1, 2 * D_FF),
        w2=w_down[0], b2=b_down[0].reshape(N_EXPERTS, 1, D_MODEL),
        norm_final=norm_final.reshape(1, D_MODEL),
    )
    streams = [_mixer(x_prompt, p), _mixer(x_sample, p)]
    y_prompt, y_sample = _moe_routed(streams, p["w1"], p["b1"], p["w2"], p["b2"], p["norm_final"])
    return (y_prompt.reshape(x_prompt.shape), y_sample.reshape(x_sample.shape))
```

```python
import functools
import math

import jax
import jax.numpy as jnp
import numpy as np
from jax import lax
from jax.experimental import pallas as pl
from jax.experimental.pallas import tpu as pltpu

F32 = jnp.float32
BF16 = jnp.bfloat16

D_MODEL = 1024
D_F = 1024
FGROUP = 256
D_INNER = 2048
HEADDIM = 64
N_HEADS = 32
N_GROUPS = 8
HEADS_PER_GROUP = N_HEADS // N_GROUPS
D_STATE = 128
D_CONV = 5
CHUNK = 128
CONV_DIM = D_INNER + 2 * N_GROUPS * D_STATE
N_EXPERTS = 32
TOP_K = 4
D_FF = 1024
SWIGLU_ALPHA = 1.702
SWIGLU_LIMIT = 7.0
EPS = 1e-5
NEG_BIG = -1e30

LANES = 128
HALO = 16
CONV_SUB = 128
DFT_N1_MAX = 128
SSD_CHUNKS_PER_STEP = 2
MERGE_CHUNK = 512
VMEM_LIMIT = 56 * 1024 * 1024

TOK_BLOCK = 512
PIECE = 16
FFN_TM = 512
_WORST_PIECES = (TOK_BLOCK * TOP_K + N_EXPERTS * (PIECE - 1) + PIECE - 1) // PIECE
_PIECES_PER_TILE = FFN_TM // PIECE
PIECES_PER_BLOCK = (_WORST_PIECES + _PIECES_PER_TILE - 1) // _PIECES_PER_TILE * _PIECES_PER_TILE
ROWS_PER_BLOCK = PIECES_PER_BLOCK * PIECE
DMA_UNROLL = 8
ITEM_NONE, ITEM_WRITE, ITEM_MERGE, ITEM_ZERO = 0, 1, 2, 3

COL_Z = 0
COL_XBC = D_INNER
COL_GATE = COL_XBC + CONV_DIM
COL_F = COL_GATE + 2 * D_MODEL
PROJ_MAIN = COL_F + D_F


def _cparams(sem):
    return pltpu.CompilerParams(dimension_semantics=sem, vmem_limit_bytes=VMEM_LIMIT)


def _inproj_kernel(x_ref, nw_ref, w_ref, wdt_ref, proj_ref, f_ref, dt_ref, hn_ref, *, n_main):
    j = pl.program_id(1)

    @pl.when(j == 0)
    def _():
        x = x_ref[...]
        ms = jnp.mean(x * x, axis=-1, keepdims=True)
        hn = (x * lax.rsqrt(ms + EPS) * nw_ref[...]).astype(BF16)
        hn_ref[...] = hn
        dt_ref[...] = jnp.dot(hn, wdt_ref[...], preferred_element_type=F32)

    def tile():
        return jnp.dot(hn_ref[...], w_ref[...], preferred_element_type=F32).astype(BF16)

    @pl.when(j < n_main)
    def _():
        proj_ref[...] = tile()

    @pl.when(j == n_main)
    def _():
        f_ref[...] = tile()


def _inproj(x, norm_w, w_main, w_dt, tm=2048, tn=1024):
    t = x.shape[0]
    tm = min(tm, t)
    assert tn == D_F and COL_F % tn == 0
    n_main = COL_F // tn
    return pl.pallas_call(
        functools.partial(_inproj_kernel, n_main=n_main),
        grid=(t // tm, n_main + 1),
        in_specs=[
            pl.BlockSpec((tm, D_MODEL), lambda i, j: (i, 0)),
            pl.BlockSpec((1, D_MODEL), lambda i, j: (0, 0)),
            pl.BlockSpec((D_MODEL, tn), lambda i, j: (0, j)),
            pl.BlockSpec((D_MODEL, 2 * LANES), lambda i, j: (0, 0)),
        ],
        out_specs=[
            pl.BlockSpec((tm, tn), lambda i, j: (i, jnp.minimum(j, n_main - 1))),
            pl.BlockSpec((tm, D_F), lambda i, j: (i, 0)),
            pl.BlockSpec((tm, 2 * LANES), lambda i, j: (i, 0)),
        ],
        out_shape=[
            jax.ShapeDtypeStruct((t, COL_F), BF16),
            jax.ShapeDtypeStruct((t, D_F), BF16),
            jax.ShapeDtypeStruct((t, 2 * LANES), F32),
        ],
        scratch_shapes=[pltpu.VMEM((tm, D_MODEL), BF16)],
        compiler_params=_cparams(("parallel", "arbitrary")),
        name="inproj",
    )(x, norm_w, w_main, w_dt)


def _dft_factors(seq):
    assert seq & (seq - 1) == 0 and seq >= 256, "power-of-two sequence lengths"
    n1 = min(DFT_N1_MAX, seq // HALO)
    return n1, seq // n1


def _dft_tables(seq):
    n1, n2 = _dft_factors(seq)
    two_pi = 2.0 * math.pi
    c = jnp.arange(FGROUP, dtype=jnp.int32)
    ph = ((c[:, None] * c[None, :]) % FGROUP).astype(F32) * (two_pi / FGROUP)
    cs_chan = jnp.concatenate([jnp.cos(ph), -jnp.sin(ph)], axis=1).astype(BF16)
    k1 = jnp.arange(n1, dtype=jnp.int32)
    nn = (n2 * jnp.arange(n1, dtype=jnp.int32)[None, :] + jnp.arange(n2, dtype=jnp.int32)[:, None])
    al = ((k1[None, :, None] * nn[:, None, :]) % seq).astype(F32) * (two_pi / seq)
    ca, sa = jnp.cos(al), jnp.sin(al)
    g1 = jnp.concatenate([jnp.concatenate([ca, sa], axis=2),
                          jnp.concatenate([-sa, ca], axis=2)], axis=1).astype(BF16)
    k2 = jnp.arange(n2, dtype=jnp.int32)
    be = ((k2[:, None] * k2[None, :]) % n2).astype(F32) * (two_pi / n2)
    g2 = jnp.concatenate([jnp.cos(be), jnp.sin(be)], axis=1).astype(BF16)
    return cs_chan, g1, g2


def _dft1_kernel(x_ref, cs_ref, g_ref, o_ref, *, tn2, n1):
    for j in range(tn2):
        x = x_ref[0, j]
        parts = []
        for q in range(D_F // FGROUP):
            uv = jnp.dot(x[:, q * FGROUP:(q + 1) * FGROUP], cs_ref[...],
                         preferred_element_type=F32).astype(BF16)
            parts.append(jnp.concatenate([uv[:, :FGROUP], uv[:, FGROUP:]], axis=0))
        z = jnp.concatenate(parts, axis=1)
        o_ref[0, j] = jnp.dot(g_ref[j], z, preferred_element_type=F32).astype(BF16)


def _dft2_kernel(a_ref, g_ref, o_ref, *, tk1, scale):
    for j in range(tk1):
        o_ref[0, j] = (jnp.dot(g_ref[...], a_ref[0, j], preferred_element_type=F32) * scale).astype(BF16)


def _fourier_mix(f_in, tables):
    bsz, seq, _ = f_in.shape
    n1, n2 = _dft_factors(seq)
    cs_chan, g1, g2 = tables
    xt = f_in.reshape(bsz, n1, n2, D_F).transpose(0, 2, 1, 3)
    tn2 = min(8, n2)
    stage1 = pl.pallas_call(
        functools.partial(_dft1_kernel, tn2=tn2, n1=n1),
        grid=(bsz, n2 // tn2),
        in_specs=[
            pl.BlockSpec((1, tn2, n1, D_F), lambda b, i: (b, i, 0, 0)),
            pl.BlockSpec((FGROUP, 2 * FGROUP), lambda b, i: (0, 0)),
            pl.BlockSpec((tn2, 2 * n1, 2 * n1), lambda b, i: (i, 0, 0)),
        ],
        out_specs=pl.BlockSpec((1, tn2, 2 * n1, D_F), lambda b, i: (b, i, 0, 0)),
        out_shape=jax.ShapeDtypeStruct((bsz, n2, 2 * n1, D_F), BF16),
        compiler_params=_cparams(("parallel", "parallel")),
        name="dft_stage1",
    )(xt, cs_chan, g1)
    a2 = (stage1.reshape(bsz, n2, 2, n1, D_F).transpose(0, 3, 2, 1, 4)
          .reshape(bsz, n1, 2 * n2, D_F))
    tk1 = min(8, n1)
    scale = 1.0 / math.sqrt(seq * FGROUP)
    stage2 = pl.pallas_call(
        functools.partial(_dft2_kernel, tk1=tk1, scale=scale),
        grid=(bsz, n1 // tk1),
        in_specs=[
            pl.BlockSpec((1, tk1, 2 * n2, D_F), lambda b, i: (b, i, 0, 0)),
            pl.BlockSpec((n2, 2 * n2), lambda b, i: (0, 0)),
        ],
        out_specs=pl.BlockSpec((1, tk1, n2, D_F), lambda b, i: (b, i, 0, 0)),
        out_shape=jax.ShapeDtypeStruct((bsz, n1, n2, D_F), BF16),
        compiler_params=_cparams(("parallel", "parallel")),
        name="dft_stage2",
    )(a2, g2)
    return stage2.transpose(0, 2, 1, 3).reshape(bsz, seq, D_F)


def _conv_kernel(prev_ref, main_ref, next_ref, w_ref, b_ref, o_ref, *, tl, tiles_per_seq):
    i = pl.program_id(0) % tiles_per_seq
    halo_zero = jnp.zeros(prev_ref.shape, BF16)
    prev = jnp.where(i == 0, halo_zero, prev_ref[...])
    nxt = jnp.where(i == tiles_per_seq - 1, halo_zero, next_ref[...])
    full = jnp.concatenate([prev, main_ref[...], nxt], axis=0)
    pad = D_CONV // 2
    win = CONV_SUB + 2 * HALO
    r = lax.broadcasted_iota(jnp.int32, (CONV_SUB, win), 0)
    c = lax.broadcasted_iota(jnp.int32, (CONV_SUB, win), 1)
    taps = [k for k in range(D_CONV) if k != pad]
    shift = jnp.concatenate([jnp.where(c == r + (HALO + k - pad), 1.0, 0.0) for k in taps], axis=1).astype(BF16)
    w_bf = w_ref[...].astype(BF16)
    for j in range(tl // CONV_SUB):
        window = full[j * CONV_SUB:j * CONV_SUB + win]
        stacked = jnp.concatenate([window * w_bf[k:k + 1, :] for k in taps], axis=0)
        acc = (b_ref[...] + w_ref[pad:pad + 1, :] * window[HALO:HALO + CONV_SUB].astype(F32)
               + jnp.dot(shift, stacked, preferred_element_type=F32))
        o_ref[j * CONV_SUB:(j + 1) * CONV_SUB, :] = (acc * jax.nn.sigmoid(acc)).astype(BF16)


def _conv_silu(proj, conv_w, conv_b, seq, tl=512, tc=2048):
    t = proj.shape[0]
    tl = min(tl, seq)
    tiles_per_seq = seq // tl
    cb0 = COL_XBC // tc
    hb = tl // HALO
    last_hb = t // HALO - 1
    return pl.pallas_call(
        functools.partial(_conv_kernel, tl=tl, tiles_per_seq=tiles_per_seq),
        grid=(t // tl, CONV_DIM // tc),
        in_specs=[
            pl.BlockSpec((HALO, tc), lambda i, c: (jnp.maximum(i * hb - 1, 0), cb0 + c)),
            pl.BlockSpec((tl, tc), lambda i, c: (i, cb0 + c)),
            pl.BlockSpec((HALO, tc), lambda i, c: (jnp.minimum((i + 1) * hb, last_hb), cb0 + c)),
            pl.BlockSpec((D_CONV, tc), lambda i, c: (0, c)),
            pl.BlockSpec((1, tc), lambda i, c: (0, c)),
        ],
        out_specs=pl.BlockSpec((tl, tc), lambda i, c: (i, c)),
        out_shape=jax.ShapeDtypeStruct((t, CONV_DIM), BF16),
        compiler_params=_cparams(("parallel", "parallel")),
        name="conv_silu",
    )(proj, proj, proj, conv_w, conv_b)


def _split3(v):
    hi = v.astype(BF16)
    r1 = v - hi.astype(F32)
    mid = r1.astype(BF16)
    lo = (r1 - mid.astype(F32)).astype(BF16)
    return [hi, mid, lo]


def _masked_sums(tri, vals):
    parts = []
    for v in vals:
        parts += _split3(v)
    out = jnp.dot(tri, jnp.concatenate(parts, axis=1), preferred_element_type=F32)
    return [out[:, (3 * i) * LANES:(3 * i + 1) * LANES] + out[:, (3 * i + 1) * LANES:(3 * i + 2) * LANES]
            + out[:, (3 * i + 2) * LANES:(3 * i + 3) * LANES] for i in range(len(vals))]


def _ssd_kernel(xs_ref, b_ref, c_ref, dtf_ref, dtb_ref, xs2_ref, b2_ref, c2_ref, dtb2_ref,
                biasf_ref, biasb_ref, alogf_ref, alogb_ref, exp_ref, dskip_ref,
                y1_ref, y2_ref, statef_ref, stateb_ref):
    @pl.when(pl.program_id(1) == 0)
    def _():
        statef_ref[...] = jnp.zeros(statef_ref.shape, F32)
        stateb_ref[...] = jnp.zeros(stateb_ref.shape, F32)

    for u in range(SSD_CHUNKS_PER_STEP):
        near = pl.ds(u * CHUNK, CHUNK)
        far = pl.ds((SSD_CHUNKS_PER_STEP - 1 - u) * CHUNK, CHUNK)
        _ssd_chunk(xs_ref.at[near], b_ref.at[near], c_ref.at[near], dtf_ref.at[near], dtb_ref.at[near],
                   xs2_ref.at[far], b2_ref.at[far], c2_ref.at[far], dtb2_ref.at[far],
                   biasf_ref, biasb_ref, alogf_ref, alogb_ref, exp_ref, dskip_ref,
                   y1_ref.at[near], y2_ref.at[far], statef_ref, stateb_ref)


def _ssd_chunk(xs_ref, b_ref, c_ref, dtf_ref, dtb_ref, xs2_ref, b2_ref, c2_ref, dtb2_ref,
               biasf_ref, biasb_ref, alogf_ref, alogb_ref, exp_ref, dskip_ref,
               y1_ref, y2_ref, statef_ref, stateb_ref):
    row = lax.broadcasted_iota(jnp.int32, (CHUNK, CHUNK), 0)
    col = lax.broadcasted_iota(jnp.int32, (CHUNK, CHUNK), 1)
    lower = row >= col
    diag = row == col
    tri_f = jnp.where(lower, 1.0, 0.0).astype(BF16)
    tri_b = jnp.where(row <= col, 1.0, 0.0).astype(BF16)

    af_neg = -jnp.exp(alogf_ref[...])
    ab_neg = -jnp.exp(alogb_ref[...])
    dt_f = jax.nn.softplus(dtf_ref[...] + biasf_ref[...])
    dt_b = jax.nn.softplus(dtb_ref[...] + biasb_ref[...])
    dt_b2 = jax.nn.softplus(dtb2_ref[...] + biasb_ref[...])
    (cf,) = _masked_sums(tri_f, [dt_f * af_neg])
    cb, cb2 = _masked_sums(tri_b, [dt_b * ab_neg, dt_b2 * ab_neg])
    tot_f = cf[CHUNK - 1:CHUNK, :]
    tot_b2 = cb2[0:1, :]
    dtb_t = dt_b.T
    srcf_t = (cf - jnp.log(dt_f)).T
    srcb_t = cb.T - jnp.log(dtb_t)

    dec = jnp.concatenate(_split3(jnp.exp(tot_f)) + _split3(jnp.exp(tot_b2))
                          + [jnp.zeros((PIECE - 6, LANES), BF16)], axis=0)
    stack = jnp.concatenate([(dt_f * jnp.exp(tot_f - cf)).astype(BF16), jnp.exp(cf).astype(BF16),
                             (dt_b2 * jnp.exp(tot_b2 - cb2)).astype(BF16), jnp.exp(cb2).astype(BF16), dec], axis=0)
    lane = lax.broadcasted_iota(jnp.int32, (CHUNK, LANES), 1)
    lo = lane < HEADDIM
    gw = HEADS_PER_GROUP * HEADDIM
    tn = (((0,), (0,)), ((), ()))
    d0 = 4 * CHUNK
    for g in range(N_GROUPS):
        gs = slice(g * gw, (g + 1) * gw)
        ns = slice(g * D_STATE, (g + 1) * D_STATE)
        ex = jnp.dot(stack, exp_ref[:, gs], preferred_element_type=F32)
        decf_e = ex[d0:d0 + 1] + ex[d0 + 1:d0 + 2] + ex[d0 + 2:d0 + 3]
        decb_e = ex[d0 + 3:d0 + 4] + ex[d0 + 4:d0 + 5] + ex[d0 + 5:d0 + 6]
        xd_f = (xs_ref[:, gs].astype(F32) * ex[0:CHUNK]).astype(BF16)
        xd_b = (xs2_ref[:, gs].astype(F32) * ex[2 * CHUNK:3 * CHUNK]).astype(BF16)
        bg, cg = b_ref[:, ns], c_ref[:, ns]
        cbm = lax.dot_general(cg, bg, (((1,), (1,)), ((), ())), preferred_element_type=F32)
        prev_f = statef_ref[g]
        y_off = jnp.dot(cg, prev_f.astype(BF16), preferred_element_type=F32) * ex[CHUNK:2 * CHUNK]
        statef_ref[g] = prev_f * decf_e + lax.dot_general(bg, xd_f, tn, preferred_element_type=F32)
        prev_b = stateb_ref[g]
        y2_ref[:, gs] = (jnp.dot(c2_ref[:, ns], prev_b.astype(BF16), preferred_element_type=F32)
                         * ex[3 * CHUNK:4 * CHUNK]).astype(BF16)
        stateb_ref[g] = prev_b * decb_e + lax.dot_general(b2_ref[:, ns], xd_b, tn, preferred_element_type=F32)
        for q in range(HEADS_PER_GROUP // 2):
            ms = []
            for hh in range(2):
                h = g * HEADS_PER_GROUP + 2 * q + hh
                seg = jnp.where(lower, cf[:, h:h + 1] - srcf_t[h:h + 1, :], cb[:, h:h + 1] - srcb_t[h:h + 1, :])
                wgt = jnp.exp(seg) + jnp.where(diag, dtb_t[h:h + 1, :], 0.0)
                ms.append((wgt * cbm).astype(BF16))
            lhs = jnp.concatenate(ms, axis=1)
            c0 = g * gw + 2 * q * HEADDIM
            xp = xs_ref[:, c0:c0 + LANES]
            zero = jnp.zeros_like(xp)
            rhs = jnp.concatenate([jnp.where(lo, xp, zero), jnp.where(lo, zero, xp)], axis=0)
            y = (jnp.dot(lhs, rhs, preferred_element_type=F32) + y_off[:, 2 * q * HEADDIM:2 * q * HEADDIM + LANES]
                 + dskip_ref[:, c0:c0 + LANES] * xp.astype(F32))
            y1_ref[:, c0:c0 + LANES] = y.astype(BF16)


def _ssd_scan(xbc, dt, p, seq):
    t = xbc.shape[0]
    rows = SSD_CHUNKS_PER_STEP * CHUNK
    assert seq % rows == 0
    ns = seq // rows
    near = lambda col: (lambda b, c: (b * ns + c, col))
    far = lambda col: (lambda b, c: (b * ns + ns - 1 - c, col))
    const = lambda b, c: (0, 0)
    gn = N_GROUPS * D_STATE
    state = pltpu.VMEM((N_GROUPS, D_STATE, HEADS_PER_GROUP * HEADDIM), F32)
    return pl.pallas_call(
        _ssd_kernel,
        grid=(t // seq, ns),
        in_specs=[
            pl.BlockSpec((rows, D_INNER), near(0)),
            pl.BlockSpec((rows, gn), near(D_INNER // gn)),
            pl.BlockSpec((rows, gn), near(D_INNER // gn + 1)),
            pl.BlockSpec((rows, LANES), near(0)),
            pl.BlockSpec((rows, LANES), near(1)),
            pl.BlockSpec((rows, D_INNER), far(0)),
            pl.BlockSpec((rows, gn), far(D_INNER // gn)),
            pl.BlockSpec((rows, gn), far(D_INNER // gn + 1)),
            pl.BlockSpec((rows, LANES), far(1)),
            pl.BlockSpec((1, LANES), const), pl.BlockSpec((1, LANES), const),
            pl.BlockSpec((1, LANES), const), pl.BlockSpec((1, LANES), const),
            pl.BlockSpec((LANES, D_INNER), const),
            pl.BlockSpec((1, D_INNER), const),
        ],
        out_specs=[pl.BlockSpec((rows, D_INNER), near(0)), pl.BlockSpec((rows, D_INNER), far(0))],
        out_shape=[jax.ShapeDtypeStruct((t, D_INNER), BF16), jax.ShapeDtypeStruct((t, D_INNER), BF16)],
        scratch_shapes=[state, state],
        compiler_params=_cparams(("parallel", "arbitrary")),
        name="ssd_scan",
    )(xbc, xbc, xbc, dt, dt, xbc, xbc, xbc, dt,
      p["dtb_f"], p["dtb_b"], p["alog_f"], p["alog_b"], p["expand"], p["d_skip"])


def _merge_kernel(y1_ref, y2_ref, z_ref, gate_ref, fm_ref, x_ref, snw_ref, wf_ref, ws_ref, wo_ref, fnw_ref,
                  wr_ref, br_ref, x1_ref, hn_ref, gates_ref, blk_ref):
    acc = None
    ssq = None
    for c0 in range(0, D_INNER, MERGE_CHUNK):
        cs = slice(c0, c0 + MERGE_CHUNK)
        z = z_ref[:, cs].astype(F32)
        yg = (y1_ref[:, cs].astype(F32) + y2_ref[:, cs].astype(F32)) * (z * jax.nn.sigmoid(z))
        sq = jnp.sum(yg * yg, axis=-1, keepdims=True)
        part = jnp.dot((yg * snw_ref[:, cs]).astype(BF16), ws_ref[cs, :], preferred_element_type=F32)
        acc = part if acc is None else acc + part
        ssq = sq if ssq is None else ssq + sq
    u_s = acc * lax.rsqrt(ssq * (1.0 / D_INNER) + EPS)
    x1 = x_ref[...]
    fm = fm_ref[...]
    for c0 in range(0, D_MODEL, MERGE_CHUNK):
        cs = slice(c0, c0 + MERGE_CHUNK)
        u_f = jnp.dot(fm, wf_ref[:, cs], preferred_element_type=F32)
        g_f = jax.nn.sigmoid(gate_ref[:, cs].astype(F32))
        g_s = jax.nn.sigmoid(gate_ref[:, D_MODEL + c0:D_MODEL + c0 + MERGE_CHUNK].astype(F32))
        merged = (g_f * u_f + g_s * u_s[:, cs]).astype(BF16)
        x1 = x1 + jnp.dot(merged, wo_ref[cs, :], preferred_element_type=F32)
    x1_ref[...] = x1
    ms1 = jnp.mean(x1 * x1, axis=-1, keepdims=True)
    hn = (x1 * lax.rsqrt(ms1 + EPS) * fnw_ref[...]).astype(BF16)
    hn_ref[...] = hn
    logits = jnp.dot(hn, wr_ref[...], preferred_element_type=F32) + br_ref[...]
    lane = lax.broadcasted_iota(jnp.int32, logits.shape, 1).astype(F32)
    tm = logits.shape[0]
    work = logits
    top = None
    denom = jnp.zeros((tm, 1), F32)
    route = jnp.zeros(logits.shape, F32)
    probs, onehots = [], []
    for k in range(TOP_K):
        m = jnp.max(work, axis=-1, keepdims=True)
        if k == 0:
            top = m
        first = jnp.min(jnp.where(work == m, lane, float(LANES)), axis=-1, keepdims=True)
        sel = lane == first
        onehots.append(jnp.where(sel, 1.0, 0.0))
        work = jnp.where(sel, NEG_BIG * 2, work)
        pk = jnp.exp(m - top)
        denom = denom + pk
        probs.append(pk)
        route = jnp.where(lane == float(k), first, route)
    inv = 1.0 / denom
    for k in range(TOP_K):
        route = jnp.where(lane == float(TOP_K + k), probs[k] * inv, route)
    ti = lax.broadcasted_iota(jnp.int32, (tm, tm), 0)
    tj = lax.broadcasted_iota(jnp.int32, (tm, tm), 1)
    earlier = jnp.where(tj < ti, 1.0, 0.0).astype(BF16)
    prefix = jnp.dot(earlier, jnp.concatenate(onehots, axis=1).astype(BF16), preferred_element_type=F32)
    cnts = [prefix[tm - 1:tm, k * LANES:(k + 1) * LANES] + onehots[k][tm - 1:tm, :] for k in range(TOP_K)]
    pieces = jnp.floor((cnts[0] + cnts[1] + cnts[2] + cnts[3] + (PIECE - 1.0)) * (1.0 / PIECE))
    ei = lax.broadcasted_iota(jnp.int32, (LANES, LANES), 0)
    ej = lax.broadcasted_iota(jnp.int32, (LANES, LANES), 1)
    lower_experts = jnp.where(ei < ej, 1.0, 0.0).astype(BF16)
    seg_start = jnp.dot(jnp.broadcast_to(pieces, (8, LANES)).astype(BF16), lower_experts,
                        preferred_element_type=F32)[0:1, :]
    base = seg_start * float(PIECE)
    for k in range(TOP_K):
        row_of = base + prefix[:, k * LANES:(k + 1) * LANES]
        pos = jnp.sum(onehots[k] * row_of, axis=-1, keepdims=True)
        route = jnp.where(lane == float(2 * TOP_K + k), pos, route)
        base = base + cnts[k]
    gates_ref[...] = route
    blk_ref[0] = jnp.concatenate([pieces, seg_start, jnp.zeros((6, LANES), F32)], axis=0)


def _merge_route(y1, y2, proj, f_mix, x, ssm_norm, w_fourier, w_ssm_out, w_out, norm_ffn, w_router, b_router):
    t = x.shape[0]
    tm = TOK_BLOCK
    assert t % tm == 0
    full = lambda r, c: pl.BlockSpec((r, c), lambda i: (0, 0))
    return pl.pallas_call(
        _merge_kernel,
        grid=(t // tm,),
        in_specs=[
            pl.BlockSpec((tm, D_INNER), lambda i: (i, 0)),
            pl.BlockSpec((tm, D_INNER), lambda i: (i, 0)),
            pl.BlockSpec((tm, D_INNER), lambda i: (i, COL_Z // D_INNER)),
            pl.BlockSpec((tm, 2 * D_MODEL), lambda i: (i, COL_GATE // (2 * D_MODEL))),
            pl.BlockSpec((tm, D_F), lambda i: (i, 0)),
            pl.BlockSpec((tm, D_MODEL), lambda i: (i, 0)),
            full(1, D_INNER), full(D_F, D_MODEL), full(D_INNER, D_MODEL), full(D_MODEL, D_MODEL),
            full(1, D_MODEL), full(D_MODEL, LANES), full(1, LANES),
        ],
        out_specs=[
            pl.BlockSpec((tm, D_MODEL), lambda i: (i, 0)),
            pl.BlockSpec((tm, D_MODEL), lambda i: (i, 0)),
            pl.BlockSpec((tm, LANES), lambda i: (i, 0)),
            pl.BlockSpec((1, 8, LANES), lambda i: (i, 0, 0)),
        ],
        out_shape=[
            jax.ShapeDtypeStruct((t, D_MODEL), F32),
            jax.ShapeDtypeStruct((t, D_MODEL), BF16),
            jax.ShapeDtypeStruct((t, LANES), F32),
            jax.ShapeDtypeStruct((t // tm, 8, LANES), F32),
        ],
        compiler_params=_cparams(("parallel",)),
        name="merge_route",
    )(y1, y2, proj, proj, f_mix, x, ssm_norm, w_fourier, w_ssm_out, w_out, norm_ffn, w_router, b_router)


def _route_tables(pos, blk):
    i32 = jnp.int32
    nb = blk.shape[0]
    pcs = blk[:, 0, :N_EXPERTS].astype(i32)
    seg_start = blk[:, 1, :N_EXPERTS].astype(i32)
    seg_end = seg_start + pcs
    used = seg_end[:, -1]
    per_e = jnp.sum(pcs, axis=0)
    g_end = jnp.cumsum(per_e)
    g_start = g_end - per_e
    b_prefix = jnp.cumsum(pcs, axis=0) - pcs
    j = jnp.arange(PIECES_PER_BLOCK, dtype=i32)[None, :, None]
    in_seg = jnp.logical_and(j >= seg_start[:, None, :], j < seg_end[:, None, :])
    shift = (g_start[None, :] + b_prefix - seg_start)[:, None, :]
    j2 = j[:, :, 0]
    dst_used = jnp.sum(jnp.where(in_seg, shift, 0), axis=2) + j2
    free = PIECES_PER_BLOCK - used
    dst_unused = g_end[-1] + (jnp.cumsum(free) - free)[:, None] + j2 - used[:, None]
    is_used = j2 < used[:, None]
    dst = jnp.where(is_used, dst_used, dst_unused).reshape(-1)
    src = jnp.where(is_used, dst_used, 0).reshape(-1)
    n_tiles = nb * ROWS_PER_BLOCK // FFN_TM
    n_items_max = n_tiles + N_EXPERTS
    start_rows, end_rows = g_start * PIECE, g_end * PIECE
    t0 = start_rows // FFN_TM
    n_it = jnp.where(end_rows > start_rows, (end_rows - 1) // FFN_TM - t0 + 1, 0)
    it_end = jnp.cumsum(n_it)
    it_start = it_end - n_it
    n_exp = it_end[-1]
    tiles_used = (end_rows[-1] + FFN_TM - 1) // FFN_TM
    q = jnp.arange(n_items_max, dtype=i32)
    qc = jnp.minimum(q, n_exp - 1)[:, None]
    mine = jnp.logical_and(qc >= it_start[None, :], qc < it_end[None, :])
    pick = lambda v: jnp.sum(jnp.where(mine, v[None, :], 0), axis=1)
    e_q = pick(jnp.arange(N_EXPERTS, dtype=i32))
    tile_exp = pick(t0 - it_start) + qc[:, 0]
    tile_q = jnp.where(q < n_exp, tile_exp, jnp.minimum(tiles_used + q - n_exp, n_tiles - 1))
    lo = jnp.maximum(pick(start_rows), tile_q * FFN_TM) - tile_q * FFN_TM
    hi = jnp.minimum(pick(end_rows), (tile_q + 1) * FFN_TM) - tile_q * FFN_TM
    new_tile = jnp.concatenate([jnp.ones((1,), i32), (tile_q[1:] != tile_q[:-1]).astype(i32)])
    kind = jnp.where(q < n_exp, jnp.where(new_tile == 1, ITEM_WRITE, ITEM_MERGE),
                     jnp.where(new_tile == 1, ITEM_ZERO, ITEM_NONE))
    pos_t = pos.astype(i32).reshape(nb, TOK_BLOCK, TOP_K).transpose(0, 2, 1)
    return dict(pos_t=pos_t, dst=dst, src=src,
                item_e=e_q, item_tile=tile_q, item_lo=lo, item_hi=hi, item_kind=kind)


def _piece_copy_out(buf_ref, hbm_ref, sem, slot, j, dst_piece):
    return pltpu.make_async_copy(buf_ref.at[slot, j], hbm_ref.at[dst_piece], sem.at[slot])


def _piece_copy_in(hbm_ref, buf_ref, sem, slot, j, src_piece):
    return pltpu.make_async_copy(hbm_ref.at[src_piece], buf_ref.at[slot, j], sem.at[slot])


def _dispatch_kernel(dst_ref, hna_ref, hnb_ref, post_ref, xs_hbm, buf_ref, sem, *, nb, nb_a):
    b = pl.program_id(0)
    slot = b % 2
    hn = jnp.where(b < nb_a, hna_ref[...], hnb_ref[...])

    def wait_slot(s):
        pltpu.make_async_copy(buf_ref.at[s], xs_hbm.at[pl.ds(0, PIECES_PER_BLOCK)], sem.at[s]).wait()

    @pl.when(b >= 2)
    def _():
        wait_slot(slot)

    r = lax.broadcasted_iota(jnp.int32, (ROWS_PER_BLOCK, TOK_BLOCK), 0)
    onehot = jnp.zeros((ROWS_PER_BLOCK, TOK_BLOCK), F32)
    for k in range(TOP_K):
        onehot = jnp.where(r == post_ref[0, k:k + 1, :], 1.0, onehot)
    sorted_rows = jnp.dot(onehot.astype(BF16), hn, preferred_element_type=F32).astype(BF16)
    buf_ref[slot] = sorted_rows.reshape(PIECES_PER_BLOCK, PIECE, D_MODEL)

    def start(j, c):
        _piece_copy_out(buf_ref, xs_hbm, sem, slot, j, dst_ref[b * PIECES_PER_BLOCK + j]).start()
        return c
    lax.fori_loop(0, PIECES_PER_BLOCK, start, 0, unroll=DMA_UNROLL)

    @pl.when(b == nb - 1)
    def _():
        wait_slot(slot)
        if nb >= 2:
            wait_slot(1 - slot)


def _ffn_kernel(e_ref, tile_ref, lo_ref, hi_ref, kind_ref, x_ref, w1_ref, b1_ref, w2_ref, b2_ref, o_ref):
    q = pl.program_id(0)
    lo, hi, kind = lo_ref[q], hi_ref[q], kind_ref[q]

    def ffn():
        hu = jnp.dot(x_ref[...], w1_ref[0].astype(BF16), preferred_element_type=F32) + b1_ref[0]
        glu = jnp.minimum(hu[:, :D_FF], SWIGLU_LIMIT)
        lin = jnp.clip(hu[:, D_FF:], -SWIGLU_LIMIT, SWIGLU_LIMIT)
        act = (glu * jax.nn.sigmoid(SWIGLU_ALPHA * glu) * (lin + 1.0)).astype(BF16)
        return (jnp.dot(act, w2_ref[0].astype(BF16), preferred_element_type=F32) + b2_ref[0]).astype(BF16)

    @pl.when(kind == ITEM_WRITE)
    def _():
        o_ref[...] = ffn()

    @pl.when(kind == ITEM_MERGE)
    def _():
        row = lax.broadcasted_iota(jnp.int32, (FFN_TM, 1), 0)
        mine = jnp.logical_and(row >= lo, row < hi)
        o_ref[...] = jnp.where(mine, ffn(), o_ref[...])

    @pl.when(kind == ITEM_ZERO)
    def _():
        o_ref[...] = jnp.zeros(o_ref.shape, BF16)


def _combine_kernel(src_ref, route_ref, x1_ref, nf_ref, os_hbm, o_ref, buf_ref, sem, *, nb, blk0):
    b = pl.program_id(0)
    slot = b % 2

    def fetch(bb, s):
        def body(j, c):
            _piece_copy_in(os_hbm, buf_ref, sem, s, j, src_ref[(blk0 + bb) * PIECES_PER_BLOCK + j]).start()
            return c
        lax.fori_loop(0, PIECES_PER_BLOCK, body, 0, unroll=DMA_UNROLL)

    @pl.when(b == 0)
    def _():
        fetch(0, 0)

    @pl.when(b + 1 < nb)
    def _():
        fetch(b + 1, 1 - slot)

    pltpu.make_async_copy(os_hbm.at[pl.ds(0, PIECES_PER_BLOCK)], buf_ref.at[slot], sem.at[slot]).wait()

    r = lax.broadcasted_iota(jnp.int32, (TOK_BLOCK, ROWS_PER_BLOCK), 1).astype(F32)
    route = route_ref[...]
    wmat = jnp.zeros((TOK_BLOCK, ROWS_PER_BLOCK), F32)
    for k in range(TOP_K):
        wmat = jnp.where(r == route[:, 2 * TOP_K + k:2 * TOP_K + k + 1], route[:, TOP_K + k:TOP_K + k + 1], wmat)
    y = jnp.dot(wmat.astype(BF16), buf_ref[slot].reshape(ROWS_PER_BLOCK, D_MODEL), preferred_element_type=F32)
    x2 = x1_ref[...] + y
    ms = jnp.mean(x2 * x2, axis=-1, keepdims=True)
    o_ref[...] = x2 * lax.rsqrt(ms + EPS) * nf_ref[...]


def _moe_routed(streams, w1, b1, w2, b2, norm_final):
    (hn_a, route_a, blk_a, _), (hn_b, route_b, blk_b, _) = streams
    assert ROWS_PER_BLOCK % FFN_TM == 0
    nb_a, nb_b = blk_a.shape[0], blk_b.shape[0]
    nb = nb_a + nb_b
    rows = nb * ROWS_PER_BLOCK
    pos = jnp.concatenate([route_a[:, 2 * TOP_K:3 * TOP_K], route_b[:, 2 * TOP_K:3 * TOP_K]], axis=0)
    tb = _route_tables(pos, jnp.concatenate([blk_a, blk_b], axis=0))
    sorted_x = pl.pallas_call(
        functools.partial(_dispatch_kernel, nb=nb, nb_a=nb_a),
        grid_spec=pltpu.PrefetchScalarGridSpec(
            num_scalar_prefetch=1,
            grid=(nb,),
            in_specs=[
                pl.BlockSpec((TOK_BLOCK, D_MODEL), lambda b, d: (jnp.minimum(b, nb_a - 1), 0)),
                pl.BlockSpec((TOK_BLOCK, D_MODEL), lambda b, d: (jnp.maximum(b - nb_a, 0), 0)),
                pl.BlockSpec((1, TOP_K, TOK_BLOCK), lambda b, d: (b, 0, 0)),
            ],
            out_specs=pl.BlockSpec(memory_space=pl.ANY),
            scratch_shapes=[pltpu.VMEM((2, PIECES_PER_BLOCK, PIECE, D_MODEL), BF16),
                            pltpu.SemaphoreType.DMA((2,))],
        ),
        out_shape=jax.ShapeDtypeStruct((rows // PIECE, PIECE, D_MODEL), BF16),
        compiler_params=_cparams(("arbitrary",)),
        name="moe_dispatch",
    )(tb["dst"], hn_a, hn_b, tb["pos_t"]).reshape(rows, D_MODEL)
    n_items = rows // FFN_TM + N_EXPERTS
    sorted_o = pl.pallas_call(
        _ffn_kernel,
        grid_spec=pltpu.PrefetchScalarGridSpec(
            num_scalar_prefetch=5,
            grid=(n_items,),
            in_specs=[
                pl.BlockSpec((FFN_TM, D_MODEL), lambda q, e, ti, lo, hi, fi: (ti[q], 0)),
                pl.BlockSpec((1, D_MODEL, 2 * D_FF), lambda q, e, ti, lo, hi, fi: (e[q], 0, 0)),
                pl.BlockSpec((1, 1, 2 * D_FF), lambda q, e, ti, lo, hi, fi: (e[q], 0, 0)),
                pl.BlockSpec((1, D_FF, D_MODEL), lambda q, e, ti, lo, hi, fi: (e[q], 0, 0)),
                pl.BlockSpec((1, 1, D_MODEL), lambda q, e, ti, lo, hi, fi: (e[q], 0, 0)),
            ],
            out_specs=pl.BlockSpec((FFN_TM, D_MODEL), lambda q, e, ti, lo, hi, fi: (ti[q], 0)),
        ),
        out_shape=jax.ShapeDtypeStruct((rows, D_MODEL), BF16),
        compiler_params=_cparams(("arbitrary",)),
        name="moe_ffn",
    )(tb["item_e"], tb["item_tile"], tb["item_lo"], tb["item_hi"], tb["item_kind"], sorted_x, w1, b1, w2, b2)
    sorted_o = sorted_o.reshape(rows // PIECE, PIECE, D_MODEL)
    outs = []
    blk0 = 0
    for _, route, blk, x1 in streams:
        nb_s = blk.shape[0]
        outs.append(pl.pallas_call(
            functools.partial(_combine_kernel, nb=nb_s, blk0=blk0),
            grid_spec=pltpu.PrefetchScalarGridSpec(
                num_scalar_prefetch=1,
                grid=(nb_s,),
                in_specs=[
                    pl.BlockSpec((TOK_BLOCK, LANES), lambda b, s: (b, 0)),
                    pl.BlockSpec((TOK_BLOCK, D_MODEL), lambda b, s: (b, 0)),
                    pl.BlockSpec((1, D_MODEL), lambda b, s: (0, 0)),
                    pl.BlockSpec(memory_space=pl.ANY),
                ],
                out_specs=pl.BlockSpec((TOK_BLOCK, D_MODEL), lambda b, s: (b, 0)),
                scratch_shapes=[pltpu.VMEM((2, PIECES_PER_BLOCK, PIECE, D_MODEL), BF16),
                                pltpu.SemaphoreType.DMA((2,))],
            ),
            out_shape=jax.ShapeDtypeStruct((nb_s * TOK_BLOCK, D_MODEL), F32),
            compiler_params=_cparams(("arbitrary",)),
            name="moe_combine",
        )(tb["src"], route, x1, norm_final, sorted_o))
        blk0 += nb_s
    return outs


def _pad_lanes(v, fill=0.0):
    v = v.reshape(1, -1).astype(F32)
    return jnp.pad(v, ((0, 0), (0, LANES - v.shape[1])), constant_values=fill)


def _mixer(x3, p):
    bsz, seq, _ = x3.shape
    x = x3.reshape(bsz * seq, D_MODEL)
    proj, f_in, dt = _inproj(x, p["norm_mix"], p["w_main"], p["w_dt"])
    f_mix = _fourier_mix(f_in.reshape(bsz, seq, D_F), _dft_tables(seq)).reshape(bsz * seq, D_F)
    xbc = _conv_silu(proj, p["conv_w"], p["conv_b"], seq)
    y1, y2 = _ssd_scan(xbc, dt, p, seq)
    x1, hn, route, blk = _merge_route(y1, y2, proj, f_mix, x, p["ssm_norm"], p["w_fourier"], p["w_ssm_out"],
                                 p["w_out"], p["norm_ffn"], p["w_router"], p["b_router"])
    return hn, route, blk, x1


def kernel(x_prompt, x_sample, norm_mix, w_in, conv_w, conv_b, dt_bias_fwd, dt_bias_bwd, a_log_fwd, a_log_bwd, d_skip, ssm_norm, w_fourier, w_ssm_out, w_out, norm_ffn, w_router, b_router, w_gate_up, b_gate_up, w_down, b_down, norm_final):
    assert norm_mix.shape[0] == 1, "single-layer block"
    w = w_in[0]
    o_z, o_xbc, o_dt, o_gate = D_F, D_F + D_INNER, D_F + D_INNER + CONV_DIM, D_F + D_INNER + CONV_DIM + 2 * N_HEADS
    w_main = jnp.concatenate([w[:, o_z:o_xbc], w[:, o_xbc:o_dt], w[:, o_gate:], w[:, :D_F]], axis=1).astype(BF16)
    pad_dt = lambda m: jnp.pad(m, ((0, 0), (0, LANES - N_HEADS))).astype(BF16)
    head_of_chan = jnp.arange(D_INNER, dtype=jnp.int32) // HEADDIM
    expand = (jnp.arange(LANES, dtype=jnp.int32)[:, None] == head_of_chan[None, :]).astype(BF16)
    p = dict(
        norm_mix=norm_mix[0].reshape(1, D_MODEL),
        w_main=w_main,
        w_dt=jnp.concatenate([pad_dt(w[:, o_dt:o_dt + N_HEADS]), pad_dt(w[:, o_dt + N_HEADS:o_gate])], axis=1),
        conv_w=conv_w[0], conv_b=conv_b[0].reshape(1, CONV_DIM),
        dtb_f=_pad_lanes(dt_bias_fwd[0]), dtb_b=_pad_lanes(dt_bias_bwd[0]),
        alog_f=_pad_lanes(a_log_fwd[0], NEG_BIG), alog_b=_pad_lanes(a_log_bwd[0], NEG_BIG),
        expand=expand,
        d_skip=jnp.repeat(d_skip[0].astype(F32), HEADDIM).reshape(1, D_INNER),
        ssm_norm=ssm_norm[0].reshape(1, D_INNER),
        w_fourier=w_fourier[0].astype(BF16), w_ssm_out=w_ssm_out[0].astype(BF16), w_out=w_out[0].astype(BF16),
        norm_ffn=norm_ffn[0].reshape(1, D_MODEL),
        w_router=jnp.pad(w_router[0], ((0, 0), (0, LANES - N_EXPERTS))).astype(BF16),
        b_router=_pad_lanes(b_router[0], NEG_BIG),
        w1=w_gate_up[0], b1=b_gate_up[0].reshape(N_EXPERTS, 1, 2 * D_FF),
        w2=w_down[0], b2=b_down[0].reshape(N_EXPERTS, 1, D_MODEL),
        norm_final=norm_final.reshape(1, D_MODEL),
    )
    streams = [_mixer(x_prompt, p), _mixer(x_sample, p)]
    y_prompt, y_sample = _moe_routed(streams, p["w1"], p["b1"], p["w2"], p["b2"], p["norm_final"])
    return (y_prompt.reshape(x_prompt.shape), y_sample.reshape(x_sample.shape))
```

```python
import functools
import math

import jax
import jax.numpy as jnp
import numpy as np
from jax import lax
from jax.experimental import pallas as pl
from jax.experimental.pallas import tpu as pltpu

F32 = jnp.float32
BF16 = jnp.bfloat16

D_MODEL = 1024
D_F = 1024
FGROUP = 256
D_INNER = 2048
HEADDIM = 64
N_HEADS = 32
N_GROUPS = 8
HEADS_PER_GROUP = N_HEADS // N_GROUPS
D_STATE = 128
D_CONV = 5
CHUNK = 128
CONV_DIM = D_INNER + 2 * N_GROUPS * D_STATE
N_EXPERTS = 32
TOP_K = 4
D_FF = 1024
SWIGLU_ALPHA = 1.702
SWIGLU_LIMIT = 7.0
EPS = 1e-5
NEG_BIG = -1e30

LANES = 128
HALO = 16
CONV_SUB = 128
DFT_N1_MAX = 128
SSD_CHUNKS_PER_STEP = 2
MERGE_CHUNK = 512
VMEM_LIMIT = 56 * 1024 * 1024

TOK_BLOCK = 512
PIECE = 16
FFN_TM = 512
_WORST_PIECES = (TOK_BLOCK * TOP_K + N_EXPERTS * (PIECE - 1) + PIECE - 1) // PIECE
_PIECES_PER_TILE = FFN_TM // PIECE
PIECES_PER_BLOCK = (_WORST_PIECES + _PIECES_PER_TILE - 1) // _PIECES_PER_TILE * _PIECES_PER_TILE
ROWS_PER_BLOCK = PIECES_PER_BLOCK * PIECE
DMA_UNROLL = 8
ITEM_NONE, ITEM_WRITE, ITEM_MERGE, ITEM_ZERO = 0, 1, 2, 3

COL_Z = 0
COL_XBC = D_INNER
COL_GATE = COL_XBC + CONV_DIM
COL_F = COL_GATE + 2 * D_MODEL
PROJ_MAIN = COL_F + D_F


def _cparams(sem):
    return pltpu.CompilerParams(dimension_semantics=sem, vmem_limit_bytes=VMEM_LIMIT)


def _inproj_kernel(x_ref, nw_ref, w_ref, wdt_ref, proj_ref, f_ref, dt_ref, hn_ref, *, n_main):
    j = pl.program_id(1)

    @pl.when(j == 0)
    def _():
        x = x_ref[...]
        ms = jnp.mean(x * x, axis=-1, keepdims=True)
        hn = (x * lax.rsqrt(ms + EPS) * nw_ref[...]).astype(BF16)
        hn_ref[...] = hn
        dt_ref[...] = jnp.dot(hn, wdt_ref[...], preferred_element_type=F32)

    def tile():
        return jnp.dot(hn_ref[...], w_ref[...], preferred_element_type=F32).astype(BF16)

    @pl.when(j < n_main)
    def _():
        proj_ref[...] = tile()

    @pl.when(j == n_main)
    def _():
        f_ref[...] = tile()


def _inproj(x, norm_w, w_main, w_dt, tm=2048, tn=1024):
    t = x.shape[0]
    tm = min(tm, t)
    assert tn == D_F and COL_F % tn == 0
    n_main = COL_F // tn
    return pl.pallas_call(
        functools.partial(_inproj_kernel, n_main=n_main),
        grid=(t // tm, n_main + 1),
        in_specs=[
            pl.BlockSpec((tm, D_MODEL), lambda i, j: (i, 0)),
            pl.BlockSpec((1, D_MODEL), lambda i, j: (0, 0)),
            pl.BlockSpec((D_MODEL, tn), lambda i, j: (0, j)),
            pl.BlockSpec((D_MODEL, 2 * LANES), lambda i, j: (0, 0)),
        ],
        out_specs=[
            pl.BlockSpec((tm, tn), lambda i, j: (i, jnp.minimum(j, n_main - 1))),
            pl.BlockSpec((tm, D_F), lambda i, j: (i, 0)),
            pl.BlockSpec((tm, 2 * LANES), lambda i, j: (i, 0)),
        ],
        out_shape=[
            jax.ShapeDtypeStruct((t, COL_F), BF16),
            jax.ShapeDtypeStruct((t, D_F), BF16),
            jax.ShapeDtypeStruct((t, 2 * LANES), F32),
        ],
        scratch_shapes=[pltpu.VMEM((tm, D_MODEL), BF16)],
        compiler_params=_cparams(("parallel", "arbitrary")),
        name="inproj",
    )(x, norm_w, w_main, w_dt)


def _dft_factors(seq):
    assert seq & (seq - 1) == 0 and seq >= 256, "power-of-two sequence lengths"
    n1 = min(DFT_N1_MAX, seq // HALO)
    return n1, seq // n1


def _dft_tables(seq):
    n1, n2 = _dft_factors(seq)
    two_pi = 2.0 * math.pi
    c = jnp.arange(FGROUP, dtype=jnp.int32)
    ph = ((c[:, None] * c[None, :]) % FGROUP).astype(F32) * (two_pi / FGROUP)
    cs_chan = jnp.concatenate([jnp.cos(ph), -jnp.sin(ph)], axis=1).astype(BF16)
    k1 = jnp.arange(n1, dtype=jnp.int32)
    nn = (n2 * jnp.arange(n1, dtype=jnp.int32)[None, :] + jnp.arange(n2, dtype=jnp.int32)[:, None])
    al = ((k1[None, :, None] * nn[:, None, :]) % seq).astype(F32) * (two_pi / seq)
    ca, sa = jnp.cos(al), jnp.sin(al)
    g1 = jnp.concatenate([jnp.concatenate([ca, sa], axis=2),
                          jnp.concatenate([-sa, ca], axis=2)], axis=1).astype(BF16)
    k2 = jnp.arange(n2, dtype=jnp.int32)
    be = ((k2[:, None] * k2[None, :]) % n2).astype(F32) * (two_pi / n2)
    g2 = jnp.concatenate([jnp.cos(be), jnp.sin(be)], axis=1).astype(BF16)
    return cs_chan, g1, g2


def _dft1_kernel(x_ref, cs_ref, g_ref, o_ref, *, tn2, n1):
    for j in range(tn2):
        x = x_ref[0, j]
        parts = []
        for q in range(D_F // FGROUP):
            uv = jnp.dot(x[:, q * FGROUP:(q + 1) * FGROUP], cs_ref[...],
                         preferred_element_type=F32).astype(BF16)
            parts.append(jnp.concatenate([uv[:, :FGROUP], uv[:, FGROUP:]], axis=0))
        z = jnp.concatenate(parts, axis=1)
        o_ref[0, j] = jnp.dot(g_ref[j], z, preferred_element_type=F32).astype(BF16)


def _dft2_kernel(a_ref, g_ref, o_ref, *, tk1, pack, scale):
    n2 = o_ref.shape[2]
    for j in range(0, tk1, pack):
        a = a_ref[0, j:j + pack].reshape(pack * 2 * n2, D_F)
        out = jnp.dot(g_ref[...], a, preferred_element_type=F32) * scale
        o_ref[0, j:j + pack] = out.reshape(pack, n2, D_F).astype(BF16)


def _fourier_mix(f_in, tables):
    bsz, seq, _ = f_in.shape
    n1, n2 = _dft_factors(seq)
    cs_chan, g1, g2 = tables
    xt = f_in.reshape(bsz, n1, n2, D_F).transpose(0, 2, 1, 3)
    tn2 = min(8, n2)
    stage1 = pl.pallas_call(
        functools.partial(_dft1_kernel, tn2=tn2, n1=n1),
        grid=(bsz, n2 // tn2),
        in_specs=[
            pl.BlockSpec((1, tn2, n1, D_F), lambda b, i: (b, i, 0, 0)),
            pl.BlockSpec((FGROUP, 2 * FGROUP), lambda b, i: (0, 0)),
            pl.BlockSpec((tn2, 2 * n1, 2 * n1), lambda b, i: (i, 0, 0)),
        ],
        out_specs=pl.BlockSpec((1, tn2, 2 * n1, D_F), lambda b, i: (b, i, 0, 0)),
        out_shape=jax.ShapeDtypeStruct((bsz, n2, 2 * n1, D_F), BF16),
        compiler_params=_cparams(("parallel", "parallel")),
        name="dft_stage1",
    )(xt, cs_chan, g1)
    a2 = (stage1.reshape(bsz, n2, 2, n1, D_F).transpose(0, 3, 2, 1, 4)
          .reshape(bsz, n1, 2 * n2, D_F))
    tk1 = min(8, n1)
    pack = min(tk1, max(1, DFT_N1_MAX // n2))
    g2_packed = jnp.kron(jnp.eye(pack, dtype=BF16), g2)
    scale = 1.0 / math.sqrt(seq * FGROUP)
    stage2 = pl.pallas_call(
        functools.partial(_dft2_kernel, tk1=tk1, pack=pack, scale=scale),
        grid=(bsz, n1 // tk1),
        in_specs=[
            pl.BlockSpec((1, tk1, 2 * n2, D_F), lambda b, i: (b, i, 0, 0)),
            pl.BlockSpec((pack * n2, pack * 2 * n2), lambda b, i: (0, 0)),
        ],
        out_specs=pl.BlockSpec((1, tk1, n2, D_F), lambda b, i: (b, i, 0, 0)),
        out_shape=jax.ShapeDtypeStruct((bsz, n1, n2, D_F), BF16),
        compiler_params=_cparams(("parallel", "parallel")),
        name="dft_stage2",
    )(a2, g2_packed)
    return stage2.transpose(0, 2, 1, 3).reshape(bsz, seq, D_F)


def _conv_kernel(prev_ref, main_ref, next_ref, w_ref, b_ref, o_ref, *, tl, tiles_per_seq):
    i = pl.program_id(0) % tiles_per_seq
    halo_zero = jnp.zeros(prev_ref.shape, BF16)
    prev = jnp.where(i == 0, halo_zero, prev_ref[...])
    nxt = jnp.where(i == tiles_per_seq - 1, halo_zero, next_ref[...])
    full = jnp.concatenate([prev, main_ref[...], nxt], axis=0)
    pad = D_CONV // 2
    win = CONV_SUB + 2 * HALO
    r = lax.broadcasted_iota(jnp.int32, (CONV_SUB, win), 0)
    c = lax.broadcasted_iota(jnp.int32, (CONV_SUB, win), 1)
    taps = [k for k in range(D_CONV) if k != pad]
    shift = jnp.concatenate([jnp.where(c == r + (HALO + k - pad), 1.0, 0.0) for k in taps], axis=1).astype(BF16)
    w_bf = w_ref[...].astype(BF16)
    for j in range(tl // CONV_SUB):
        window = full[j * CONV_SUB:j * CONV_SUB + win]
        stacked = jnp.concatenate([window * w_bf[k:k + 1, :] for k in taps], axis=0)
        acc = (b_ref[...] + w_ref[pad:pad + 1, :] * window[HALO:HALO + CONV_SUB].astype(F32)
               + jnp.dot(shift, stacked, preferred_element_type=F32))
        o_ref[j * CONV_SUB:(j + 1) * CONV_SUB, :] = (acc * jax.nn.sigmoid(acc)).astype(BF16)


def _conv_silu(proj, conv_w, conv_b, seq, tl=512, tc=2048):
    t = proj.shape[0]
    tl = min(tl, seq)
    tiles_per_seq = seq // tl
    cb0 = COL_XBC // tc
    hb = tl // HALO
    last_hb = t // HALO - 1
    return pl.pallas_call(
        functools.partial(_conv_kernel, tl=tl, tiles_per_seq=tiles_per_seq),
        grid=(t // tl, CONV_DIM // tc),
        in_specs=[
            pl.BlockSpec((HALO, tc), lambda i, c: (jnp.maximum(i * hb - 1, 0), cb0 + c)),
            pl.BlockSpec((tl, tc), lambda i, c: (i, cb0 + c)),
            pl.BlockSpec((HALO, tc), lambda i, c: (jnp.minimum((i + 1) * hb, last_hb), cb0 + c)),
            pl.BlockSpec((D_CONV, tc), lambda i, c: (0, c)),
            pl.BlockSpec((1, tc), lambda i, c: (0, c)),
        ],
        out_specs=pl.BlockSpec((tl, tc), lambda i, c: (i, c)),
        out_shape=jax.ShapeDtypeStruct((t, CONV_DIM), BF16),
        compiler_params=_cparams(("parallel", "parallel")),
        name="conv_silu",
    )(proj, proj, proj, conv_w, conv_b)


def _split3(v):
    hi = v.astype(BF16)
    r1 = v - hi.astype(F32)
    mid = r1.astype(BF16)
    lo = (r1 - mid.astype(F32)).astype(BF16)
    return [hi, mid, lo]


def _masked_sums(tri, vals):
    parts = []
    for v in vals:
        parts += _split3(v)
    out = jnp.dot(tri, jnp.concatenate(parts, axis=1), preferred_element_type=F32)
    return [out[:, (3 * i) * LANES:(3 * i + 1) * LANES] + out[:, (3 * i + 1) * LANES:(3 * i + 2) * LANES]
            + out[:, (3 * i + 2) * LANES:(3 * i + 3) * LANES] for i in range(len(vals))]


def _ssd_kernel(xs_ref, b_ref, c_ref, dtf_ref, dtb_ref, xs2_ref, b2_ref, c2_ref, dtb2_ref,
                biasf_ref, biasb_ref, alogf_ref, alogb_ref, exp_ref, dskip_ref,
                y1_ref, y2_ref, statef_ref, stateb_ref):
    @pl.when(pl.program_id(1) == 0)
    def _():
        statef_ref[...] = jnp.zeros(statef_ref.shape, F32)
        stateb_ref[...] = jnp.zeros(stateb_ref.shape, F32)

    for u in range(SSD_CHUNKS_PER_STEP):
        near = pl.ds(u * CHUNK, CHUNK)
        far = pl.ds((SSD_CHUNKS_PER_STEP - 1 - u) * CHUNK, CHUNK)
        _ssd_chunk(xs_ref.at[near], b_ref.at[near], c_ref.at[near], dtf_ref.at[near], dtb_ref.at[near],
                   xs2_ref.at[far], b2_ref.at[far], c2_ref.at[far], dtb2_ref.at[far],
                   biasf_ref, biasb_ref, alogf_ref, alogb_ref, exp_ref, dskip_ref,
                   y1_ref.at[near], y2_ref.at[far], statef_ref, stateb_ref)


def _ssd_chunk(xs_ref, b_ref, c_ref, dtf_ref, dtb_ref, xs2_ref, b2_ref, c2_ref, dtb2_ref,
               biasf_ref, biasb_ref, alogf_ref, alogb_ref, exp_ref, dskip_ref,
               y1_ref, y2_ref, statef_ref, stateb_ref):
    row = lax.broadcasted_iota(jnp.int32, (CHUNK, CHUNK), 0)
    col = lax.broadcasted_iota(jnp.int32, (CHUNK, CHUNK), 1)
    lower = row >= col
    diag = row == col
    tri_f = jnp.where(lower, 1.0, 0.0).astype(BF16)
    tri_b = jnp.where(row <= col, 1.0, 0.0).astype(BF16)

    af_neg = -jnp.exp(alogf_ref[...])
    ab_neg = -jnp.exp(alogb_ref[...])
    dt_f = jax.nn.softplus(dtf_ref[...] + biasf_ref[...])
    dt_b = jax.nn.softplus(dtb_ref[...] + biasb_ref[...])
    dt_b2 = jax.nn.softplus(dtb2_ref[...] + biasb_ref[...])
    (cf,) = _masked_sums(tri_f, [dt_f * af_neg])
    cb, cb2 = _masked_sums(tri_b, [dt_b * ab_neg, dt_b2 * ab_neg])
    tot_f = cf[CHUNK - 1:CHUNK, :]
    tot_b2 = cb2[0:1, :]
    dtb_t = dt_b.T
    srcf_t = (cf - jnp.log(dt_f)).T
    srcb_t = cb.T - jnp.log(dtb_t)

    dec = jnp.concatenate(_split3(jnp.exp(tot_f)) + _split3(jnp.exp(tot_b2))
                          + [jnp.zeros((PIECE - 6, LANES), BF16)], axis=0)
    stack = jnp.concatenate([(dt_f * jnp.exp(tot_f - cf)).astype(BF16), jnp.exp(cf).astype(BF16),
                             (dt_b2 * jnp.exp(tot_b2 - cb2)).astype(BF16), jnp.exp(cb2).astype(BF16), dec], axis=0)
    lane = lax.broadcasted_iota(jnp.int32, (CHUNK, LANES), 1)
    lo = lane < HEADDIM
    gw = HEADS_PER_GROUP * HEADDIM
    tn = (((0,), (0,)), ((), ()))
    d0 = 4 * CHUNK
    for g in range(N_GROUPS):
        gs = slice(g * gw, (g + 1) * gw)
        ns = slice(g * D_STATE, (g + 1) * D_STATE)
        ex = jnp.dot(stack, exp_ref[:, gs], preferred_element_type=F32)
        decf_e = ex[d0:d0 + 1] + ex[d0 + 1:d0 + 2] + ex[d0 + 2:d0 + 3]
        decb_e = ex[d0 + 3:d0 + 4] + ex[d0 + 4:d0 + 5] + ex[d0 + 5:d0 + 6]
        xd_f = (xs_ref[:, gs].astype(F32) * ex[0:CHUNK]).astype(BF16)
        xd_b = (xs2_ref[:, gs].astype(F32) * ex[2 * CHUNK:3 * CHUNK]).astype(BF16)
        bg, cg = b_ref[:, ns], c_ref[:, ns]
        cbm = lax.dot_general(cg, bg, (((1,), (1,)), ((), ())), preferred_element_type=F32)
        prev_f = statef_ref[g]
        y_off = jnp.dot(cg, prev_f.astype(BF16), preferred_element_type=F32) * ex[CHUNK:2 * CHUNK]
        statef_ref[g] = prev_f * decf_e + lax.dot_general(bg, xd_f, tn, preferred_element_type=F32)
        prev_b = stateb_ref[g]
        y2_ref[:, gs] = (jnp.dot(c2_ref[:, ns], prev_b.astype(BF16), preferred_element_type=F32)
                         * ex[3 * CHUNK:4 * CHUNK]).astype(BF16)
        stateb_ref[g] = prev_b * decb_e + lax.dot_general(b2_ref[:, ns], xd_b, tn, preferred_element_type=F32)
        for q in range(HEADS_PER_GROUP // 2):
            ms = []
            for hh in range(2):
                h = g * HEADS_PER_GROUP + 2 * q + hh
                seg = jnp.where(lower, cf[:, h:h + 1] - srcf_t[h:h + 1, :], cb[:, h:h + 1] - srcb_t[h:h + 1, :])
                wgt = jnp.exp(seg) + jnp.where(diag, dtb_t[h:h + 1, :], 0.0)
                ms.append((wgt * cbm).astype(BF16))
            lhs = jnp.concatenate(ms, axis=1)
            c0 = g * gw + 2 * q * HEADDIM
            xp = xs_ref[:, c0:c0 + LANES]
            zero = jnp.zeros_like(xp)
            rhs = jnp.concatenate([jnp.where(lo, xp, zero), jnp.where(lo, zero, xp)], axis=0)
            y = (jnp.dot(lhs, rhs, preferred_element_type=F32) + y_off[:, 2 * q * HEADDIM:2 * q * HEADDIM + LANES]
                 + dskip_ref[:, c0:c0 + LANES] * xp.astype(F32))
            y1_ref[:, c0:c0 + LANES] = y.astype(BF16)


def _ssd_scan(xbc, dt, p, seq):
    t = xbc.shape[0]
    rows = SSD_CHUNKS_PER_STEP * CHUNK
    assert seq % rows == 0
    ns = seq // rows
    near = lambda col: (lambda b, c: (b * ns + c, col))
    far = lambda col: (lambda b, c: (b * ns + ns - 1 - c, col))
    const = lambda b, c: (0, 0)
    gn = N_GROUPS * D_STATE
    state = pltpu.VMEM((N_GROUPS, D_STATE, HEADS_PER_GROUP * HEADDIM), F32)
    return pl.pallas_call(
        _ssd_kernel,
        grid=(t // seq, ns),
        in_specs=[
            pl.BlockSpec((rows, D_INNER), near(0)),
            pl.BlockSpec((rows, gn), near(D_INNER // gn)),
            pl.BlockSpec((rows, gn), near(D_INNER // gn + 1)),
            pl.BlockSpec((rows, LANES), near(0)),
            pl.BlockSpec((rows, LANES), near(1)),
            pl.BlockSpec((rows, D_INNER), far(0)),
            pl.BlockSpec((rows, gn), far(D_INNER // gn)),
            pl.BlockSpec((rows, gn), far(D_INNER // gn + 1)),
            pl.BlockSpec((rows, LANES), far(1)),
            pl.BlockSpec((1, LANES), const), pl.BlockSpec((1, LANES), const),
            pl.BlockSpec((1, LANES), const), pl.BlockSpec((1, LANES), const),
            pl.BlockSpec((LANES, D_INNER), const),
            pl.BlockSpec((1, D_INNER), const),
        ],
        out_specs=[pl.BlockSpec((rows, D_INNER), near(0)), pl.BlockSpec((rows, D_INNER), far(0))],
        out_shape=[jax.ShapeDtypeStruct((t, D_INNER), BF16), jax.ShapeDtypeStruct((t, D_INNER), BF16)],
        scratch_shapes=[state, state],
        compiler_params=_cparams(("parallel", "arbitrary")),
        name="ssd_scan",
    )(xbc, xbc, xbc, dt, dt, xbc, xbc, xbc, dt,
      p["dtb_f"], p["dtb_b"], p["alog_f"], p["alog_b"], p["expand"], p["d_skip"])


def _merge_kernel(y1_ref, y2_ref, z_ref, gate_ref, fm_ref, x_ref, snw_ref, wf_ref, ws_ref, wo_ref, fnw_ref,
                  wr_ref, br_ref, x1_ref, hn_ref, gates_ref, blk_ref):
    acc = None
    ssq = None
    for c0 in range(0, D_INNER, MERGE_CHUNK):
        cs = slice(c0, c0 + MERGE_CHUNK)
        z = z_ref[:, cs].astype(F32)
        yg = (y1_ref[:, cs].astype(F32) + y2_ref[:, cs].astype(F32)) * (z * jax.nn.sigmoid(z))
        sq = jnp.sum(yg * yg, axis=-1, keepdims=True)
        part = jnp.dot((yg * snw_ref[:, cs]).astype(BF16), ws_ref[cs, :], preferred_element_type=F32)
        acc = part if acc is None else acc + part
        ssq = sq if ssq is None else ssq + sq
    u_s = acc * lax.rsqrt(ssq * (1.0 / D_INNER) + EPS)
    x1 = x_ref[...]
    fm = fm_ref[...]
    for c0 in range(0, D_MODEL, MERGE_CHUNK):
        cs = slice(c0, c0 + MERGE_CHUNK)
        u_f = jnp.dot(fm, wf_ref[:, cs], preferred_element_type=F32)
        g_f = jax.nn.sigmoid(gate_ref[:, cs].astype(F32))
        g_s = jax.nn.sigmoid(gate_ref[:, D_MODEL + c0:D_MODEL + c0 + MERGE_CHUNK].astype(F32))
        merged = (g_f * u_f + g_s * u_s[:, cs]).astype(BF16)
        x1 = x1 + jnp.dot(merged, wo_ref[cs, :], preferred_element_type=F32)
    x1_ref[...] = x1
    ms1 = jnp.mean(x1 * x1, axis=-1, keepdims=True)
    hn = (x1 * lax.rsqrt(ms1 + EPS) * fnw_ref[...]).astype(BF16)
    hn_ref[...] = hn
    logits = jnp.dot(hn, wr_ref[...], preferred_element_type=F32) + br_ref[...]
    lane = lax.broadcasted_iota(jnp.int32, logits.shape, 1).astype(F32)
    tm = logits.shape[0]
    work = logits
    top = None
    denom = jnp.zeros((tm, 1), F32)
    route = jnp.zeros(logits.shape, F32)
    probs, onehots = [], []
    for k in range(TOP_K):
        m = jnp.max(work, axis=-1, keepdims=True)
        if k == 0:
            top = m
        first = jnp.min(jnp.where(work == m, lane, float(LANES)), axis=-1, keepdims=True)
        sel = lane == first
        onehots.append(jnp.where(sel, 1.0, 0.0))
        work = jnp.where(sel, NEG_BIG * 2, work)
        pk = jnp.exp(m - top)
        denom = denom + pk
        probs.append(pk)
        route = jnp.where(lane == float(k), first, route)
    inv = 1.0 / denom
    for k in range(TOP_K):
        route = jnp.where(lane == float(TOP_K + k), probs[k] * inv, route)
    ti = lax.broadcasted_iota(jnp.int32, (tm, tm), 0)
    tj = lax.broadcasted_iota(jnp.int32, (tm, tm), 1)
    earlier = jnp.where(tj < ti, 1.0, 0.0).astype(BF16)
    prefix = jnp.dot(earlier, jnp.concatenate(onehots, axis=1).astype(BF16), preferred_element_type=F32)
    cnts = [prefix[tm - 1:tm, k * LANES:(k + 1) * LANES] + onehots[k][tm - 1:tm, :] for k in range(TOP_K)]
    pieces = jnp.floor((cnts[0] + cnts[1] + cnts[2] + cnts[3] + (PIECE - 1.0)) * (1.0 / PIECE))
    ei = lax.broadcasted_iota(jnp.int32, (LANES, LANES), 0)
    ej = lax.broadcasted_iota(jnp.int32, (LANES, LANES), 1)
    lower_experts = jnp.where(ei < ej, 1.0, 0.0).astype(BF16)
    seg_start = jnp.dot(jnp.broadcast_to(pieces, (8, LANES)).astype(BF16), lower_experts,
                        preferred_element_type=F32)[0:1, :]
    base = seg_start * float(PIECE)
    for k in range(TOP_K):
        row_of = base + prefix[:, k * LANES:(k + 1) * LANES]
        pos = jnp.sum(onehots[k] * row_of, axis=-1, keepdims=True)
        route = jnp.where(lane == float(2 * TOP_K + k), pos, route)
        base = base + cnts[k]
    gates_ref[...] = route
    blk_ref[0] = jnp.concatenate([pieces, seg_start, jnp.zeros((6, LANES), F32)], axis=0)


def _merge_route(y1, y2, proj, f_mix, x, ssm_norm, w_fourier, w_ssm_out, w_out, norm_ffn, w_router, b_router):
    t = x.shape[0]
    tm = TOK_BLOCK
    assert t % tm == 0
    full = lambda r, c: pl.BlockSpec((r, c), lambda i: (0, 0))
    return pl.pallas_call(
        _merge_kernel,
        grid=(t // tm,),
        in_specs=[
            pl.BlockSpec((tm, D_INNER), lambda i: (i, 0)),
            pl.BlockSpec((tm, D_INNER), lambda i: (i, 0)),
            pl.BlockSpec((tm, D_INNER), lambda i: (i, COL_Z // D_INNER)),
            pl.BlockSpec((tm, 2 * D_MODEL), lambda i: (i, COL_GATE // (2 * D_MODEL))),
            pl.BlockSpec((tm, D_F), lambda i: (i, 0)),
            pl.BlockSpec((tm, D_MODEL), lambda i: (i, 0)),
            full(1, D_INNER), full(D_F, D_MODEL), full(D_INNER, D_MODEL), full(D_MODEL, D_MODEL),
            full(1, D_MODEL), full(D_MODEL, LANES), full(1, LANES),
        ],
        out_specs=[
            pl.BlockSpec((tm, D_MODEL), lambda i: (i, 0)),
            pl.BlockSpec((tm, D_MODEL), lambda i: (i, 0)),
            pl.BlockSpec((tm, LANES), lambda i: (i, 0)),
            pl.BlockSpec((1, 8, LANES), lambda i: (i, 0, 0)),
        ],
        out_shape=[
            jax.ShapeDtypeStruct((t, D_MODEL), F32),
            jax.ShapeDtypeStruct((t, D_MODEL), BF16),
            jax.ShapeDtypeStruct((t, LANES), F32),
            jax.ShapeDtypeStruct((t // tm, 8, LANES), F32),
        ],
        compiler_params=_cparams(("parallel",)),
        name="merge_route",
    )(y1, y2, proj, proj, f_mix, x, ssm_norm, w_fourier, w_ssm_out, w_out, norm_ffn, w_router, b_router)


def _route_tables(pos, blk):
    i32 = jnp.int32
    nb = blk.shape[0]
    pcs = blk[:, 0, :N_EXPERTS].astype(i32)
    seg_start = blk[:, 1, :N_EXPERTS].astype(i32)
    seg_end = seg_start + pcs
    used = seg_end[:, -1]
    per_e = jnp.sum(pcs, axis=0)
    g_end = jnp.cumsum(per_e)
    g_start = g_end - per_e
    b_prefix = jnp.cumsum(pcs, axis=0) - pcs
    j = jnp.arange(PIECES_PER_BLOCK, dtype=i32)[None, :, None]
    in_seg = jnp.logical_and(j >= seg_start[:, None, :], j < seg_end[:, None, :])
    shift = (g_start[None, :] + b_prefix - seg_start)[:, None, :]
    j2 = j[:, :, 0]
    dst_used = jnp.sum(jnp.where(in_seg, shift, 0), axis=2) + j2
    free = PIECES_PER_BLOCK - used
    dst_unused = g_end[-1] + (jnp.cumsum(free) - free)[:, None] + j2 - used[:, None]
    is_used = j2 < used[:, None]
    dst = jnp.where(is_used, dst_used, dst_unused).reshape(-1)
    src = jnp.where(is_used, dst_used, 0).reshape(-1)
    n_tiles = nb * ROWS_PER_BLOCK // FFN_TM
    n_items_max = n_tiles + N_EXPERTS
    start_rows, end_rows = g_start * PIECE, g_end * PIECE
    t0 = start_rows // FFN_TM
    n_it = jnp.where(end_rows > start_rows, (end_rows - 1) // FFN_TM - t0 + 1, 0)
    it_end = jnp.cumsum(n_it)
    it_start = it_end - n_it
    n_exp = it_end[-1]
    tiles_used = (end_rows[-1] + FFN_TM - 1) // FFN_TM
    q = jnp.arange(n_items_max, dtype=i32)
    qc = jnp.minimum(q, n_exp - 1)[:, None]
    mine = jnp.logical_and(qc >= it_start[None, :], qc < it_end[None, :])
    pick = lambda v: jnp.sum(jnp.where(mine, v[None, :], 0), axis=1)
    e_q = pick(jnp.arange(N_EXPERTS, dtype=i32))
    tile_exp = pick(t0 - it_start) + qc[:, 0]
    tile_q = jnp.where(q < n_exp, tile_exp, jnp.minimum(tiles_used + q - n_exp, n_tiles - 1))
    lo = jnp.maximum(pick(start_rows), tile_q * FFN_TM) - tile_q * FFN_TM
    hi = jnp.minimum(pick(end_rows), (tile_q + 1) * FFN_TM) - tile_q * FFN_TM
    new_tile = jnp.concatenate([jnp.ones((1,), i32), (tile_q[1:] != tile_q[:-1]).astype(i32)])
    kind = jnp.where(q < n_exp, jnp.where(new_tile == 1, ITEM_WRITE, ITEM_MERGE),
                     jnp.where(new_tile == 1, ITEM_ZERO, ITEM_NONE))
    pos_t = pos.astype(i32).reshape(nb, TOK_BLOCK, TOP_K).transpose(0, 2, 1)
    return dict(pos_t=pos_t, dst=dst, src=src,
                item_e=e_q, item_tile=tile_q, item_lo=lo, item_hi=hi, item_kind=kind)


def _piece_copy_out(buf_ref, hbm_ref, sem, slot, j, dst_piece):
    return pltpu.make_async_copy(buf_ref.at[slot, j], hbm_ref.at[dst_piece], sem.at[slot])


def _piece_copy_in(hbm_ref, buf_ref, sem, slot, j, src_piece):
    return pltpu.make_async_copy(hbm_ref.at[src_piece], buf_ref.at[slot, j], sem.at[slot])


def _dispatch_kernel(dst_ref, hna_ref, hnb_ref, post_ref, xs_hbm, buf_ref, sem, *, nb, nb_a):
    b = pl.program_id(0)
    slot = b % 2
    hn = jnp.where(b < nb_a, hna_ref[...], hnb_ref[...])

    def wait_slot(s):
        pltpu.make_async_copy(buf_ref.at[s], xs_hbm.at[pl.ds(0, PIECES_PER_BLOCK)], sem.at[s]).wait()

    @pl.when(b >= 2)
    def _():
        wait_slot(slot)

    r = lax.broadcasted_iota(jnp.int32, (ROWS_PER_BLOCK, TOK_BLOCK), 0)
    onehot = jnp.zeros((ROWS_PER_BLOCK, TOK_BLOCK), F32)
    for k in range(TOP_K):
        onehot = jnp.where(r == post_ref[0, k:k + 1, :], 1.0, onehot)
    sorted_rows = jnp.dot(onehot.astype(BF16), hn, preferred_element_type=F32).astype(BF16)
    buf_ref[slot] = sorted_rows.reshape(PIECES_PER_BLOCK, PIECE, D_MODEL)

    def start(j, c):
        _piece_copy_out(buf_ref, xs_hbm, sem, slot, j, dst_ref[b * PIECES_PER_BLOCK + j]).start()
        return c
    lax.fori_loop(0, PIECES_PER_BLOCK, start, 0, unroll=DMA_UNROLL)

    @pl.when(b == nb - 1)
    def _():
        wait_slot(slot)
        if nb >= 2:
            wait_slot(1 - slot)


def _ffn_kernel(e_ref, tile_ref, lo_ref, hi_ref, kind_ref, x_ref, w1_ref, b1_ref, w2_ref, b2_ref, o_ref):
    q = pl.program_id(0)
    lo, hi, kind = lo_ref[q], hi_ref[q], kind_ref[q]

    def ffn():
        hu = jnp.dot(x_ref[...], w1_ref[0].astype(BF16), preferred_element_type=F32) + b1_ref[0]
        glu = jnp.minimum(hu[:, :D_FF], SWIGLU_LIMIT)
        lin = jnp.clip(hu[:, D_FF:], -SWIGLU_LIMIT, SWIGLU_LIMIT)
        act = (glu * jax.nn.sigmoid(SWIGLU_ALPHA * glu) * (lin + 1.0)).astype(BF16)
        return (jnp.dot(act, w2_ref[0].astype(BF16), preferred_element_type=F32) + b2_ref[0]).astype(BF16)

    @pl.when(kind == ITEM_WRITE)
    def _():
        o_ref[...] = ffn()

    @pl.when(kind == ITEM_MERGE)
    def _():
        row = lax.broadcasted_iota(jnp.int32, (FFN_TM, 1), 0)
        mine = jnp.logical_and(row >= lo, row < hi)
        o_ref[...] = jnp.where(mine, ffn(), o_ref[...])

    @pl.when(kind == ITEM_ZERO)
    def _():
        o_ref[...] = jnp.zeros(o_ref.shape, BF16)


def _combine_kernel(src_ref, route_ref, x1_ref, nf_ref, os_hbm, o_ref, buf_ref, sem, *, nb, blk0):
    b = pl.program_id(0)
    slot = b % 2

    def fetch(bb, s):
        def body(j, c):
            _piece_copy_in(os_hbm, buf_ref, sem, s, j, src_ref[(blk0 + bb) * PIECES_PER_BLOCK + j]).start()
            return c
        lax.fori_loop(0, PIECES_PER_BLOCK, body, 0, unroll=DMA_UNROLL)

    @pl.when(b == 0)
    def _():
        fetch(0, 0)

    @pl.when(b + 1 < nb)
    def _():
        fetch(b + 1, 1 - slot)

    pltpu.make_async_copy(os_hbm.at[pl.ds(0, PIECES_PER_BLOCK)], buf_ref.at[slot], sem.at[slot]).wait()

    r = lax.broadcasted_iota(jnp.int32, (TOK_BLOCK, ROWS_PER_BLOCK), 1).astype(F32)
    route = route_ref[...]
    wmat = jnp.zeros((TOK_BLOCK, ROWS_PER_BLOCK), F32)
    for k in range(TOP_K):
        wmat = jnp.where(r == route[:, 2 * TOP_K + k:2 * TOP_K + k + 1], route[:, TOP_K + k:TOP_K + k + 1], wmat)
    y = jnp.dot(wmat.astype(BF16), buf_ref[slot].reshape(ROWS_PER_BLOCK, D_MODEL), preferred_element_type=F32)
    x2 = x1_ref[...] + y
    ms = jnp.mean(x2 * x2, axis=-1, keepdims=True)
    o_ref[...] = x2 * lax.rsqrt(ms + EPS) * nf_ref[...]


def _moe_routed(streams, w1, b1, w2, b2, norm_final):
    (hn_a, route_a, blk_a, _), (hn_b, route_b, blk_b, _) = streams
    assert ROWS_PER_BLOCK % FFN_TM == 0
    nb_a, nb_b = blk_a.shape[0], blk_b.shape[0]
    nb = nb_a + nb_b
    rows = nb * ROWS_PER_BLOCK
    pos = jnp.concatenate([route_a[:, 2 * TOP_K:3 * TOP_K], route_b[:, 2 * TOP_K:3 * TOP_K]], axis=0)
    tb = _route_tables(pos, jnp.concatenate([blk_a, blk_b], axis=0))
    sorted_x = pl.pallas_call(
        functools.partial(_dispatch_kernel, nb=nb, nb_a=nb_a),
        grid_spec=pltpu.PrefetchScalarGridSpec(
            num_scalar_prefetch=1,
            grid=(nb,),
            in_specs=[
                pl.BlockSpec((TOK_BLOCK, D_MODEL), lambda b, d: (jnp.minimum(b, nb_a - 1), 0)),
                pl.BlockSpec((TOK_BLOCK, D_MODEL), lambda b, d: (jnp.maximum(b - nb_a, 0), 0)),
                pl.BlockSpec((1, TOP_K, TOK_BLOCK), lambda b, d: (b, 0, 0)),
            ],
            out_specs=pl.BlockSpec(memory_space=pl.ANY),
            scratch_shapes=[pltpu.VMEM((2, PIECES_PER_BLOCK, PIECE, D_MODEL), BF16),
                            pltpu.SemaphoreType.DMA((2,))],
        ),
        out_shape=jax.ShapeDtypeStruct((rows // PIECE, PIECE, D_MODEL), BF16),
        compiler_params=_cparams(("arbitrary",)),
        name="moe_dispatch",
    )(tb["dst"], hn_a, hn_b, tb["pos_t"]).reshape(rows, D_MODEL)
    n_items = rows // FFN_TM + N_EXPERTS
    sorted_o = pl.pallas_call(
        _ffn_kernel,
        grid_spec=pltpu.PrefetchScalarGridSpec(
            num_scalar_prefetch=5,
            grid=(n_items,),
            in_specs=[
                pl.BlockSpec((FFN_TM, D_MODEL), lambda q, e, ti, lo, hi, fi: (ti[q], 0)),
                pl.BlockSpec((1, D_MODEL, 2 * D_FF), lambda q, e, ti, lo, hi, fi: (e[q], 0, 0)),
                pl.BlockSpec((1, 1, 2 * D_FF), lambda q, e, ti, lo, hi, fi: (e[q], 0, 0)),
                pl.BlockSpec((1, D_FF, D_MODEL), lambda q, e, ti, lo, hi, fi: (e[q], 0, 0)),
                pl.BlockSpec((1, 1, D_MODEL), lambda q, e, ti, lo, hi, fi: (e[q], 0, 0)),
            ],
            out_specs=pl.BlockSpec((FFN_TM, D_MODEL), lambda q, e, ti, lo, hi, fi: (ti[q], 0)),
        ),
        out_shape=jax.ShapeDtypeStruct((rows, D_MODEL), BF16),
        compiler_params=_cparams(("arbitrary",)),
        name="moe_ffn",
    )(tb["item_e"], tb["item_tile"], tb["item_lo"], tb["item_hi"], tb["item_kind"], sorted_x, w1, b1, w2, b2)
    sorted_o = sorted_o.reshape(rows // PIECE, PIECE, D_MODEL)
    outs = []
    blk0 = 0
    for _, route, blk, x1 in streams:
        nb_s = blk.shape[0]
        outs.append(pl.pallas_call(
            functools.partial(_combine_kernel, nb=nb_s, blk0=blk0),
            grid_spec=pltpu.PrefetchScalarGridSpec(
                num_scalar_prefetch=1,
                grid=(nb_s,),
                in_specs=[
                    pl.BlockSpec((TOK_BLOCK, LANES), lambda b, s: (b, 0)),
                    pl.BlockSpec((TOK_BLOCK, D_MODEL), lambda b, s: (b, 0)),
                    pl.BlockSpec((1, D_MODEL), lambda b, s: (0, 0)),
                    pl.BlockSpec(memory_space=pl.ANY),
                ],
                out_specs=pl.BlockSpec((TOK_BLOCK, D_MODEL), lambda b, s: (b, 0)),
                scratch_shapes=[pltpu.VMEM((2, PIECES_PER_BLOCK, PIECE, D_MODEL), BF16),
                                pltpu.SemaphoreType.DMA((2,))],
            ),
            out_shape=jax.ShapeDtypeStruct((nb_s * TOK_BLOCK, D_MODEL), F32),
            compiler_params=_cparams(("arbitrary",)),
            name="moe_combine",
        )(tb["src"], route, x1, norm_final, sorted_o))
        blk0 += nb_s
    return outs


def _pad_lanes(v, fill=0.0):
    v = v.reshape(1, -1).astype(F32)
    return jnp.pad(v, ((0, 0), (0, LANES - v.shape[1])), constant_values=fill)


def _mixer(x3, p):
    bsz, seq, _ = x3.shape
    x = x3.reshape(bsz * seq, D_MODEL)
    proj, f_in, dt = _inproj(x, p["norm_mix"], p["w_main"], p["w_dt"])
    f_mix = _fourier_mix(f_in.reshape(bsz, seq, D_F), _dft_tables(seq)).reshape(bsz * seq, D_F)
    xbc = _conv_silu(proj, p["conv_w"], p["conv_b"], seq)
    y1, y2 = _ssd_scan(xbc, dt, p, seq)
    x1, hn, route, blk = _merge_route(y1, y2, proj, f_mix, x, p["ssm_norm"], p["w_fourier"], p["w_ssm_out"],
                                 p["w_out"], p["norm_ffn"], p["w_router"], p["b_router"])
    return hn, route, blk, x1


def kernel(x_prompt, x_sample, norm_mix, w_in, conv_w, conv_b, dt_bias_fwd, dt_bias_bwd, a_log_fwd, a_log_bwd, d_skip, ssm_norm, w_fourier, w_ssm_out, w_out, norm_ffn, w_router, b_router, w_gate_up, b_gate_up, w_down, b_down, norm_final):
    assert norm_mix.shape[0] == 1, "single-layer block"
    w = w_in[0]
    o_z, o_xbc, o_dt, o_gate = D_F, D_F + D_INNER, D_F + D_INNER + CONV_DIM, D_F + D_INNER + CONV_DIM + 2 * N_HEADS
    w_main = jnp.concatenate([w[:, o_z:o_xbc], w[:, o_xbc:o_dt], w[:, o_gate:], w[:, :D_F]], axis=1).astype(BF16)
    pad_dt = lambda m: jnp.pad(m, ((0, 0), (0, LANES - N_HEADS))).astype(BF16)
    head_of_chan = jnp.arange(D_INNER, dtype=jnp.int32) // HEADDIM
    expand = (jnp.arange(LANES, dtype=jnp.int32)[:, None] == head_of_chan[None, :]).astype(BF16)
    p = dict(
        norm_mix=norm_mix[0].reshape(1, D_MODEL),
        w_main=w_main,
        w_dt=jnp.concatenate([pad_dt(w[:, o_dt:o_dt + N_HEADS]), pad_dt(w[:, o_dt + N_HEADS:o_gate])], axis=1),
        conv_w=conv_w[0], conv_b=conv_b[0].reshape(1, CONV_DIM),
        dtb_f=_pad_lanes(dt_bias_fwd[0]), dtb_b=_pad_lanes(dt_bias_bwd[0]),
        alog_f=_pad_lanes(a_log_fwd[0], NEG_BIG), alog_b=_pad_lanes(a_log_bwd[0], NEG_BIG),
        expand=expand,
        d_skip=jnp.repeat(d_skip[0].astype(F32), HEADDIM).reshape(1, D_INNER),
        ssm_norm=ssm_norm[0].reshape(1, D_INNER),
        w_fourier=w_fourier[0].astype(BF16), w_ssm_out=w_ssm_out[0].astype(BF16), w_out=w_out[0].astype(BF16),
        norm_ffn=norm_ffn[0].reshape(1, D_MODEL),
        w_router=jnp.pad(w_router[0], ((0, 0), (0, LANES - N_EXPERTS))).astype(BF16),
        b_router=_pad_lanes(b_router[0], NEG_BIG),
        w1=w_gate_up[0], b1=b_gate_up[0].reshape(N_EXPERTS, 1, 2 * D_FF),
        w2=w_down[0], b2=b_down[0].reshape(N_EXPERTS, 1, D_MODEL),
        norm_final=norm_final.reshape(1, D_MODEL),
    )
    streams = [_mixer(x_prompt, p), _mixer(x_sample, p)]
    y_prompt, y_sample = _moe_routed(streams, p["w1"], p["b1"], p["w2"], p["b2"], p["norm_final"])
    return (y_prompt.reshape(x_prompt.shape), y_sample.reshape(x_sample.shape))
```

```python
import functools
import math

import jax
import jax.numpy as jnp
import numpy as np
from jax import lax
from jax.experimental import pallas as pl
from jax.experimental.pallas import tpu as pltpu

F32 = jnp.float32
BF16 = jnp.bfloat16

D_MODEL = 1024
D_F = 1024
FGROUP = 256
D_INNER = 2048
HEADDIM = 64
N_HEADS = 32
N_GROUPS = 8
HEADS_PER_GROUP = N_HEADS // N_GROUPS
D_STATE = 128
D_CONV = 5
CHUNK = 128
CONV_DIM = D_INNER + 2 * N_GROUPS * D_STATE
N_EXPERTS = 32
TOP_K = 4
D_FF = 1024
SWIGLU_ALPHA = 1.702
SWIGLU_LIMIT = 7.0
EPS = 1e-5
NEG_BIG = -1e30

LANES = 128
HALO = 16
CONV_SUB = 128
DFT_BLOCK_BYTES = 4 * 1024 * 1024
DFT_N1_MAX = 128
SSD_CHUNKS_PER_STEP = 2
MERGE_CHUNK = 512
VMEM_LIMIT = 56 * 1024 * 1024

TOK_BLOCK = 512
PIECE = 16
FFN_TM = 512
_WORST_PIECES = (TOK_BLOCK * TOP_K + N_EXPERTS * (PIECE - 1) + PIECE - 1) // PIECE
_PIECES_PER_TILE = FFN_TM // PIECE
PIECES_PER_BLOCK = (_WORST_PIECES + _PIECES_PER_TILE - 1) // _PIECES_PER_TILE * _PIECES_PER_TILE
ROWS_PER_BLOCK = PIECES_PER_BLOCK * PIECE
DMA_UNROLL = 8
ITEM_NONE, ITEM_WRITE, ITEM_MERGE, ITEM_ZERO = 0, 1, 2, 3

COL_Z = 0
COL_XBC = D_INNER
COL_GATE = COL_XBC + CONV_DIM
COL_F = COL_GATE + 2 * D_MODEL
PROJ_MAIN = COL_F + D_F


def _cparams(sem):
    return pltpu.CompilerParams(dimension_semantics=sem, vmem_limit_bytes=VMEM_LIMIT)


def _inproj_kernel(x_ref, nw_ref, w_ref, wdt_ref, proj_ref, f_ref, dt_ref, hn_ref, *, n_main):
    j = pl.program_id(1)

    @pl.when(j == 0)
    def _():
        x = x_ref[...]
        ms = jnp.mean(x * x, axis=-1, keepdims=True)
        hn = (x * lax.rsqrt(ms + EPS) * nw_ref[...]).astype(BF16)
        hn_ref[...] = hn
        dt_ref[...] = jnp.dot(hn, wdt_ref[...], preferred_element_type=F32)

    def tile():
        return jnp.dot(hn_ref[...], w_ref[...], preferred_element_type=F32).astype(BF16)

    @pl.when(j < n_main)
    def _():
        proj_ref[...] = tile()

    @pl.when(j == n_main)
    def _():
        f_ref[...] = tile()


def _inproj(x, norm_w, w_main, w_dt, tm=2048, tn=1024):
    t = x.shape[0]
    tm = min(tm, t)
    assert tn == D_F and COL_F % tn == 0
    n_main = COL_F // tn
    return pl.pallas_call(
        functools.partial(_inproj_kernel, n_main=n_main),
        grid=(t // tm, n_main + 1),
        in_specs=[
            pl.BlockSpec((tm, D_MODEL), lambda i, j: (i, 0)),
            pl.BlockSpec((1, D_MODEL), lambda i, j: (0, 0)),
            pl.BlockSpec((D_MODEL, tn), lambda i, j: (0, j)),
            pl.BlockSpec((D_MODEL, 2 * LANES), lambda i, j: (0, 0)),
        ],
        out_specs=[
            pl.BlockSpec((tm, tn), lambda i, j: (i, jnp.minimum(j, n_main - 1))),
            pl.BlockSpec((tm, D_F), lambda i, j: (i, 0)),
            pl.BlockSpec((tm, 2 * LANES), lambda i, j: (i, 0)),
        ],
        out_shape=[
            jax.ShapeDtypeStruct((t, COL_F), BF16),
            jax.ShapeDtypeStruct((t, D_F), BF16),
            jax.ShapeDtypeStruct((t, 2 * LANES), F32),
        ],
        scratch_shapes=[pltpu.VMEM((tm, D_MODEL), BF16)],
        compiler_params=_cparams(("parallel", "arbitrary")),
        name="inproj",
    )(x, norm_w, w_main, w_dt)


def _dft_factors(seq):
    assert seq & (seq - 1) == 0 and seq >= 256, "power-of-two sequence lengths"
    n1 = min(DFT_N1_MAX, seq // HALO)
    return n1, seq // n1


def _dft_tables(seq):
    n1, n2 = _dft_factors(seq)
    two_pi = 2.0 * math.pi
    c = jnp.arange(FGROUP, dtype=jnp.int32)
    ph = ((c[:, None] * c[None, :]) % FGROUP).astype(F32) * (two_pi / FGROUP)
    cs_chan = jnp.concatenate([jnp.cos(ph), -jnp.sin(ph)], axis=1).astype(BF16)
    k1 = jnp.arange(n1, dtype=jnp.int32)
    nn = (n2 * jnp.arange(n1, dtype=jnp.int32)[None, :] + jnp.arange(n2, dtype=jnp.int32)[:, None])
    al = ((k1[None, :, None] * nn[:, None, :]) % seq).astype(F32) * (two_pi / seq)
    ca, sa = jnp.cos(al), jnp.sin(al)
    g1 = jnp.concatenate([jnp.concatenate([ca, sa], axis=2),
                          jnp.concatenate([-sa, ca], axis=2)], axis=1).astype(BF16)
    k2 = jnp.arange(n2, dtype=jnp.int32)
    be = ((k2[:, None] * k2[None, :]) % n2).astype(F32) * (two_pi / n2)
    g2 = jnp.concatenate([jnp.cos(be), jnp.sin(be)], axis=1).astype(BF16)
    return cs_chan, g1, g2


def _dft1_kernel(x_ref, cs_ref, g_ref, o_ref, *, tn2, n1):
    for j in range(tn2):
        x = x_ref[0, j]
        parts = []
        for q in range(D_F // FGROUP):
            uv = jnp.dot(x[:, q * FGROUP:(q + 1) * FGROUP], cs_ref[...],
                         preferred_element_type=F32).astype(BF16)
            parts.append(jnp.concatenate([uv[:, :FGROUP], uv[:, FGROUP:]], axis=0))
        z = jnp.concatenate(parts, axis=1)
        o_ref[0, j] = jnp.dot(g_ref[j], z, preferred_element_type=F32).astype(BF16)


def _dft2_kernel(a_ref, g_ref, o_ref, *, tk1, pack, scale):
    n2 = o_ref.shape[2]
    for j in range(0, tk1, pack):
        a = a_ref[0, j:j + pack].reshape(pack * 2 * n2, D_F)
        out = jnp.dot(g_ref[...], a, preferred_element_type=F32) * scale
        o_ref[0, j:j + pack] = out.reshape(pack, n2, D_F).astype(BF16)


def _fourier_mix(f_in, tables):
    bsz, seq, _ = f_in.shape
    n1, n2 = _dft_factors(seq)
    cs_chan, g1, g2 = tables
    xt = f_in.reshape(bsz, n1, n2, D_F).transpose(0, 2, 1, 3)
    tn2 = min(n2, DFT_BLOCK_BYTES // (n1 * D_F * 2))
    stage1 = pl.pallas_call(
        functools.partial(_dft1_kernel, tn2=tn2, n1=n1),
        grid=(bsz, n2 // tn2),
        in_specs=[
            pl.BlockSpec((1, tn2, n1, D_F), lambda b, i: (b, i, 0, 0)),
            pl.BlockSpec((FGROUP, 2 * FGROUP), lambda b, i: (0, 0)),
            pl.BlockSpec((tn2, 2 * n1, 2 * n1), lambda b, i: (i, 0, 0)),
        ],
        out_specs=pl.BlockSpec((1, tn2, 2 * n1, D_F), lambda b, i: (b, i, 0, 0)),
        out_shape=jax.ShapeDtypeStruct((bsz, n2, 2 * n1, D_F), BF16),
        compiler_params=_cparams(("parallel", "parallel")),
        name="dft_stage1",
    )(xt, cs_chan, g1)
    a2 = (stage1.reshape(bsz, n2, 2, n1, D_F).transpose(0, 3, 2, 1, 4)
          .reshape(bsz, n1, 2 * n2, D_F))
    tk1 = min(n1, DFT_BLOCK_BYTES // (2 * n2 * D_F * 2))
    pack = min(tk1,max(1, DFT_N1_MAX // n2))
    g2_packed = jnp.kron(jnp.eye(pack, dtype=BF16), g2)
    scale = 1.0 / math.sqrt(seq * FGROUP)
    stage2 = pl.pallas_call(
        functools.partial(_dft2_kernel, tk1=tk1, pack=pack, scale=scale),
        grid=(bsz, n1 // tk1),
        in_specs=[
            pl.BlockSpec((1, tk1, 2 * n2, D_F), lambda b, i: (b, i, 0, 0)),
            pl.BlockSpec((pack * n2, pack * 2 * n2), lambda b, i: (0, 0)),
        ],
        out_specs=pl.BlockSpec((1, tk1, n2, D_F), lambda b, i: (b, i, 0, 0)),
        out_shape=jax.ShapeDtypeStruct((bsz, n1, n2, D_F), BF16),
        compiler_params=_cparams(("parallel", "parallel")),
        name="dft_stage2",
    )(a2, g2_packed)
    return stage2.transpose(0, 2, 1, 3).reshape(bsz, seq, D_F)


def _conv_kernel(prev_ref, main_ref, next_ref, w_ref, b_ref, o_ref, *, tl, tiles_per_seq):
    i = pl.program_id(0) % tiles_per_seq
    halo_zero = jnp.zeros(prev_ref.shape, BF16)
    prev = jnp.where(i == 0, halo_zero, prev_ref[...])
    nxt = jnp.where(i == tiles_per_seq - 1, halo_zero, next_ref[...])
    full = jnp.concatenate([prev, main_ref[...], nxt], axis=0)
    pad = D_CONV // 2
    win = CONV_SUB + 2 * HALO
    r = lax.broadcasted_iota(jnp.int32, (CONV_SUB, win), 0)
    c = lax.broadcasted_iota(jnp.int32, (CONV_SUB, win), 1)
    taps = [k for k in range(D_CONV) if k != pad]
    shift = jnp.concatenate([jnp.where(c == r + (HALO + k - pad), 1.0, 0.0) for k in taps], axis=1).astype(BF16)
    w_bf = w_ref[...].astype(BF16)
    for j in range(tl // CONV_SUB):
        window = full[j * CONV_SUB:j * CONV_SUB + win]
        stacked = jnp.concatenate([window * w_bf[k:k + 1, :] for k in taps], axis=0)
        acc = (b_ref[...] + w_ref[pad:pad + 1, :] * window[HALO:HALO + CONV_SUB].astype(F32)
               + jnp.dot(shift, stacked, preferred_element_type=F32))
        o_ref[j * CONV_SUB:(j + 1) * CONV_SUB, :] = (acc * jax.nn.sigmoid(acc)).astype(BF16)


def _conv_silu(proj, conv_w, conv_b, seq, tl=512, tc=2048):
    t = proj.shape[0]
    tl = min(tl, seq)
    tiles_per_seq = seq // tl
    cb0 = COL_XBC // tc
    hb = tl // HALO
    last_hb = t // HALO - 1
    return pl.pallas_call(
        functools.partial(_conv_kernel, tl=tl, tiles_per_seq=tiles_per_seq),
        grid=(t // tl, CONV_DIM // tc),
        in_specs=[
            pl.BlockSpec((HALO, tc), lambda i, c: (jnp.maximum(i * hb - 1, 0), cb0 + c)),
            pl.BlockSpec((tl, tc), lambda i, c: (i, cb0 + c)),
            pl.BlockSpec((HALO, tc), lambda i, c: (jnp.minimum((i + 1) * hb, last_hb), cb0 + c)),
            pl.BlockSpec((D_CONV, tc), lambda i, c: (0, c)),
            pl.BlockSpec((1, tc), lambda i, c: (0, c)),
        ],
        out_specs=pl.BlockSpec((tl, tc), lambda i, c: (i, c)),
        out_shape=jax.ShapeDtypeStruct((t, CONV_DIM), BF16),
        compiler_params=_cparams(("parallel", "parallel")),
        name="conv_silu",
    )(proj, proj, proj, conv_w, conv_b)


def _split3(v):
    hi = v.astype(BF16)
    r1 = v - hi.astype(F32)
    mid = r1.astype(BF16)
    lo = (r1 - mid.astype(F32)).astype(BF16)
    return [hi, mid, lo]


def _masked_sums(tri, vals):
    parts = []
    for v in vals:
        parts += _split3(v)
    out = jnp.dot(tri, jnp.concatenate(parts, axis=1), preferred_element_type=F32)
    return [out[:, (3 * i) * LANES:(3 * i + 1) * LANES] + out[:, (3 * i + 1) * LANES:(3 * i + 2) * LANES]
            + out[:, (3 * i + 2) * LANES:(3 * i + 3) * LANES] for i in range(len(vals))]


def _ssd_kernel(xs_ref, b_ref, c_ref, dtf_ref, dtb_ref, xs2_ref, b2_ref, c2_ref, dtb2_ref,
                biasf_ref, biasb_ref, alogf_ref, alogb_ref, exp_ref, dskip_ref,
                y1_ref, y2_ref, statef_ref, stateb_ref):
    @pl.when(pl.program_id(1) == 0)
    def _():
        statef_ref[...] = jnp.zeros(statef_ref.shape, F32)
        stateb_ref[...] = jnp.zeros(stateb_ref.shape, F32)

    for u in range(SSD_CHUNKS_PER_STEP):
        near = pl.ds(u * CHUNK, CHUNK)
        far = pl.ds((SSD_CHUNKS_PER_STEP - 1 - u) * CHUNK, CHUNK)
        _ssd_chunk(xs_ref.at[near], b_ref.at[near], c_ref.at[near], dtf_ref.at[near], dtb_ref.at[near],
                   xs2_ref.at[far], b2_ref.at[far], c2_ref.at[far], dtb2_ref.at[far],
                   biasf_ref, biasb_ref, alogf_ref, alogb_ref, exp_ref, dskip_ref,
                   y1_ref.at[near], y2_ref.at[far], statef_ref, stateb_ref)


def _ssd_chunk(xs_ref, b_ref, c_ref, dtf_ref, dtb_ref, xs2_ref, b2_ref, c2_ref, dtb2_ref,
               biasf_ref, biasb_ref, alogf_ref, alogb_ref, exp_ref, dskip_ref,
               y1_ref, y2_ref, statef_ref, stateb_ref):
    row = lax.broadcasted_iota(jnp.int32, (CHUNK, CHUNK), 0)
    col = lax.broadcasted_iota(jnp.int32, (CHUNK, CHUNK), 1)
    lower = row >= col
    diag = row == col
    tri_f = jnp.where(lower, 1.0, 0.0).astype(BF16)
    tri_b = jnp.where(row <= col, 1.0, 0.0).astype(BF16)

    af_neg = -jnp.exp(alogf_ref[...])
    ab_neg = -jnp.exp(alogb_ref[...])
    dt_f = jax.nn.softplus(dtf_ref[...] + biasf_ref[...])
    dt_b = jax.nn.softplus(dtb_ref[...] + biasb_ref[...])
    dt_b2 = jax.nn.softplus(dtb2_ref[...] + biasb_ref[...])
    (cf,) = _masked_sums(tri_f, [dt_f * af_neg])
    cb, cb2 = _masked_sums(tri_b, [dt_b * ab_neg, dt_b2 * ab_neg])
    tot_f = cf[CHUNK - 1:CHUNK, :]
    tot_b2 = cb2[0:1, :]
    dtb_t = dt_b.T
    srcf_t = (cf - jnp.log(dt_f)).T
    srcb_t = cb.T - jnp.log(dtb_t)

    dec = jnp.concatenate(_split3(jnp.exp(tot_f)) + _split3(jnp.exp(tot_b2))
                          + [jnp.zeros((PIECE - 6, LANES), BF16)], axis=0)
    stack = jnp.concatenate([(dt_f * jnp.exp(tot_f - cf)).astype(BF16), jnp.exp(cf).astype(BF16),
                             (dt_b2 * jnp.exp(tot_b2 - cb2)).astype(BF16), jnp.exp(cb2).astype(BF16), dec], axis=0)
    lane = lax.broadcasted_iota(jnp.int32, (CHUNK, LANES), 1)
    lo = lane < HEADDIM
    gw = HEADS_PER_GROUP * HEADDIM
    tn = (((0,), (0,)), ((), ()))
    d0 = 4 * CHUNK
    for g in range(N_GROUPS):
        gs = slice(g * gw, (g + 1) * gw)
        ns = slice(g * D_STATE, (g + 1) * D_STATE)
        ex = jnp.dot(stack, exp_ref[:, gs], preferred_element_type=F32)
        decf_e = ex[d0:d0 + 1] + ex[d0 + 1:d0 + 2] + ex[d0 + 2:d0 + 3]
        decb_e = ex[d0 + 3:d0 + 4] + ex[d0 + 4:d0 + 5] + ex[d0 + 5:d0 + 6]
        xd_f = (xs_ref[:, gs].astype(F32) * ex[0:CHUNK]).astype(BF16)
        xd_b = (xs2_ref[:, gs].astype(F32) * ex[2 * CHUNK:3 * CHUNK]).astype(BF16)
        bg, cg = b_ref[:, ns], c_ref[:, ns]
        cbm = lax.dot_general(cg, bg, (((1,), (1,)), ((), ())), preferred_element_type=F32)
        prev_f = statef_ref[g]
        y_off = jnp.dot(cg, prev_f.astype(BF16), preferred_element_type=F32) * ex[CHUNK:2 * CHUNK]
        statef_ref[g] = prev_f * decf_e + lax.dot_general(bg, xd_f, tn, preferred_element_type=F32)
        prev_b = stateb_ref[g]
        y2_ref[:, gs] = (jnp.dot(c2_ref[:, ns], prev_b.astype(BF16), preferred_element_type=F32)
                         * ex[3 * CHUNK:4 * CHUNK]).astype(BF16)
        stateb_ref[g] = prev_b * decb_e + lax.dot_general(b2_ref[:, ns], xd_b, tn, preferred_element_type=F32)
        for q in range(HEADS_PER_GROUP // 2):
            ms = []
            for hh in range(2):
                h = g * HEADS_PER_GROUP + 2 * q + hh
                seg = jnp.where(lower, cf[:, h:h + 1] - srcf_t[h:h + 1, :], cb[:, h:h + 1] - srcb_t[h:h + 1, :])
                wgt = jnp.exp(seg) + jnp.where(diag, dtb_t[h:h + 1, :], 0.0)
                ms.append((wgt * cbm).astype(BF16))
            lhs = jnp.concatenate(ms, axis=1)
            c0 = g * gw + 2 * q * HEADDIM
            xp = xs_ref[:, c0:c0 + LANES]
            zero = jnp.zeros_like(xp)
            rhs = jnp.concatenate([jnp.where(lo, xp, zero), jnp.where(lo, zero, xp)], axis=0)
            y = (jnp.dot(lhs, rhs, preferred_element_type=F32) + y_off[:, 2 * q * HEADDIM:2 * q * HEADDIM + LANES]
                 + dskip_ref[:, c0:c0 + LANES] * xp.astype(F32))
            y1_ref[:, c0:c0 + LANES] = y.astype(BF16)


def _ssd_scan(xbc, dt, p, seq):
    t = xbc.shape[0]
    rows = SSD_CHUNKS_PER_STEP * CHUNK
    assert seq % rows == 0
    ns = seq // rows
    near = lambda col: (lambda b, c: (b * ns + c, col))
    far = lambda col: (lambda b, c: (b * ns + ns - 1 - c, col))
    const = lambda b, c: (0, 0)
    gn = N_GROUPS * D_STATE
    state = pltpu.VMEM((N_GROUPS, D_STATE, HEADS_PER_GROUP * HEADDIM), F32)
    return pl.pallas_call(
        _ssd_kernel,
        grid=(t // seq, ns),
        in_specs=[
            pl.BlockSpec((rows, D_INNER), near(0)),
            pl.BlockSpec((rows, gn), near(D_INNER // gn)),
            pl.BlockSpec((rows, gn), near(D_INNER // gn + 1)),
            pl.BlockSpec((rows, LANES), near(0)),
            pl.BlockSpec((rows, LANES), near(1)),
            pl.BlockSpec((rows, D_INNER), far(0)),
            pl.BlockSpec((rows, gn), far(D_INNER // gn)),
            pl.BlockSpec((rows, gn), far(D_INNER // gn + 1)),
            pl.BlockSpec((rows, LANES), far(1)),
            pl.BlockSpec((1, LANES), const), pl.BlockSpec((1, LANES), const),
            pl.BlockSpec((1, LANES), const), pl.BlockSpec((1, LANES), const),
            pl.BlockSpec((LANES, D_INNER), const),
            pl.BlockSpec((1, D_INNER), const),
        ],
        out_specs=[pl.BlockSpec((rows, D_INNER), near(0)), pl.BlockSpec((rows, D_INNER), far(0))],
        out_shape=[jax.ShapeDtypeStruct((t, D_INNER), BF16), jax.ShapeDtypeStruct((t, D_INNER), BF16)],
        scratch_shapes=[state, state],
        compiler_params=_cparams(("parallel", "arbitrary")),
        name="ssd_scan",
    )(xbc, xbc, xbc, dt, dt, xbc, xbc, xbc, dt,
      p["dtb_f"], p["dtb_b"], p["alog_f"], p["alog_b"], p["expand"], p["d_skip"])


def _merge_kernel(y1_ref, y2_ref, z_ref, gate_ref, fm_ref, x_ref, snw_ref, wf_ref, ws_ref, wo_ref, fnw_ref,
                  wr_ref, br_ref, x1_ref, hn_ref, gates_ref, blk_ref):
    acc = None
    ssq = None
    for c0 in range(0, D_INNER, MERGE_CHUNK):
        cs = slice(c0, c0 + MERGE_CHUNK)
        z = z_ref[:, cs].astype(F32)
        yg = (y1_ref[:, cs].astype(F32) + y2_ref[:, cs].astype(F32)) * (z * jax.nn.sigmoid(z))
        sq = jnp.sum(yg * yg, axis=-1, keepdims=True)
        part = jnp.dot((yg * snw_ref[:, cs]).astype(BF16), ws_ref[cs, :], preferred_element_type=F32)
        acc = part if acc is None else acc + part
        ssq = sq if ssq is None else ssq + sq
    u_s = acc * lax.rsqrt(ssq * (1.0 / D_INNER) + EPS)
    x1 = x_ref[...]
    fm = fm_ref[...]
    for c0 in range(0, D_MODEL, MERGE_CHUNK):
        cs = slice(c0, c0 + MERGE_CHUNK)
        u_f = jnp.dot(fm, wf_ref[:, cs], preferred_element_type=F32)
        g_f = jax.nn.sigmoid(gate_ref[:, cs].astype(F32))
        g_s = jax.nn.sigmoid(gate_ref[:, D_MODEL + c0:D_MODEL + c0 + MERGE_CHUNK].astype(F32))
        merged = (g_f * u_f + g_s * u_s[:, cs]).astype(BF16)
        x1 = x1 + jnp.dot(merged, wo_ref[cs, :], preferred_element_type=F32)
    x1_ref[...] = x1
    ms1 = jnp.mean(x1 * x1, axis=-1, keepdims=True)
    hn = (x1 * lax.rsqrt(ms1 + EPS) * fnw_ref[...]).astype(BF16)
    hn_ref[...] = hn
    logits = jnp.dot(hn, wr_ref[...], preferred_element_type=F32) + br_ref[...]
    lane = lax.broadcasted_iota(jnp.int32, logits.shape, 1).astype(F32)
    tm = logits.shape[0]
    work = logits
    top = None
    denom = jnp.zeros((tm, 1), F32)
    route = jnp.zeros(logits.shape, F32)
    probs, onehots = [], []
    for k in range(TOP_K):
        m = jnp.max(work, axis=-1, keepdims=True)
        if k == 0:
            top = m
        first = jnp.min(jnp.where(work == m, lane, float(LANES)), axis=-1, keepdims=True)
        sel = lane == first
        onehots.append(jnp.where(sel, 1.0, 0.0))
        work = jnp.where(sel, NEG_BIG * 2, work)
        pk = jnp.exp(m - top)
        denom = denom + pk
        probs.append(pk)
        route = jnp.where(lane == float(k), first, route)
    inv = 1.0 / denom
    for k in range(TOP_K):
        route = jnp.where(lane == float(TOP_K + k), probs[k] * inv, route)
    ti = lax.broadcasted_iota(jnp.int32, (tm, tm), 0)
    tj = lax.broadcasted_iota(jnp.int32, (tm, tm), 1)
    earlier = jnp.where(tj < ti, 1.0, 0.0).astype(BF16)
    prefix = jnp.dot(earlier, jnp.concatenate(onehots, axis=1).astype(BF16), preferred_element_type=F32)
    cnts = [prefix[tm - 1:tm, k * LANES:(k + 1) * LANES] + onehots[k][tm - 1:tm, :] for k in range(TOP_K)]
    pieces = jnp.floor((cnts[0] + cnts[1] + cnts[2] + cnts[3] + (PIECE - 1.0)) * (1.0 / PIECE))
    ei = lax.broadcasted_iota(jnp.int32, (LANES, LANES), 0)
    ej = lax.broadcasted_iota(jnp.int32, (LANES, LANES), 1)
    lower_experts = jnp.where(ei < ej, 1.0, 0.0).astype(BF16)
    seg_start = jnp.dot(jnp.broadcast_to(pieces, (8, LANES)).astype(BF16), lower_experts,
                        preferred_element_type=F32)[0:1, :]
    base = seg_start * float(PIECE)
    for k in range(TOP_K):
        row_of = base + prefix[:, k * LANES:(k + 1) * LANES]
        pos = jnp.sum(onehots[k] * row_of, axis=-1, keepdims=True)
        route = jnp.where(lane == float(2 * TOP_K + k), pos, route)
        base = base + cnts[k]
    gates_ref[...] = route
    blk_ref[0] = jnp.concatenate([pieces, seg_start, jnp.zeros((6, LANES), F32)], axis=0)


def _merge_route(y1, y2, proj, f_mix, x, ssm_norm, w_fourier, w_ssm_out, w_out, norm_ffn, w_router, b_router):
    t = x.shape[0]
    tm = TOK_BLOCK
    assert t % tm == 0
    full = lambda r, c: pl.BlockSpec((r, c), lambda i: (0, 0))
    return pl.pallas_call(
        _merge_kernel,
        grid=(t // tm,),
        in_specs=[
            pl.BlockSpec((tm, D_INNER), lambda i: (i, 0)),
            pl.BlockSpec((tm, D_INNER), lambda i: (i, 0)),
            pl.BlockSpec((tm, D_INNER), lambda i: (i, COL_Z // D_INNER)),
            pl.BlockSpec((tm, 2 * D_MODEL), lambda i: (i, COL_GATE // (2 * D_MODEL))),
            pl.BlockSpec((tm, D_F), lambda i: (i, 0)),
            pl.BlockSpec((tm, D_MODEL), lambda i: (i, 0)),
            full(1, D_INNER), full(D_F, D_MODEL), full(D_INNER, D_MODEL), full(D_MODEL, D_MODEL),
            full(1, D_MODEL), full(D_MODEL, LANES), full(1, LANES),
        ],
        out_specs=[
            pl.BlockSpec((tm, D_MODEL), lambda i: (i, 0)),
            pl.BlockSpec((tm, D_MODEL), lambda i: (i, 0)),
            pl.BlockSpec((tm, LANES), lambda i: (i, 0)),
            pl.BlockSpec((1, 8, LANES), lambda i: (i, 0, 0)),
        ],
        out_shape=[
            jax.ShapeDtypeStruct((t, D_MODEL), F32),
            jax.ShapeDtypeStruct((t, D_MODEL), BF16),
            jax.ShapeDtypeStruct((t, LANES), F32),
            jax.ShapeDtypeStruct((t // tm, 8, LANES), F32),
        ],
        compiler_params=_cparams(("parallel",)),
        name="merge_route",
    )(y1, y2, proj, proj, f_mix, x, ssm_norm, w_fourier, w_ssm_out, w_out, norm_ffn, w_router, b_router)


def _route_tables(pos, blk):
    i32 = jnp.int32
    nb = blk.shape[0]
    pcs = blk[:, 0, :N_EXPERTS].astype(i32)
    seg_start = blk[:, 1, :N_EXPERTS].astype(i32)
    seg_end = seg_start + pcs
    used = seg_end[:, -1]
    per_e = jnp.sum(pcs, axis=0)
    g_end = jnp.cumsum(per_e)
    g_start = g_end - per_e
    b_prefix = jnp.cumsum(pcs, axis=0) - pcs
    j = jnp.arange(PIECES_PER_BLOCK, dtype=i32)[None, :, None]
    in_seg = jnp.logical_and(j >= seg_start[:, None, :], j < seg_end[:, None, :])
    shift = (g_start[None, :] + b_prefix - seg_start)[:, None, :]
    j2 = j[:, :, 0]
    dst_used = jnp.sum(jnp.where(in_seg, shift, 0), axis=2) + j2
    free = PIECES_PER_BLOCK - used
    dst_unused = g_end[-1] + (jnp.cumsum(free) - free)[:, None] + j2 - used[:, None]
    is_used = j2 < used[:, None]
    dst = jnp.where(is_used, dst_used, dst_unused).reshape(-1)
    src = jnp.where(is_used, dst_used, 0).reshape(-1)
    n_tiles = nb * ROWS_PER_BLOCK // FFN_TM
    n_items_max = n_tiles + N_EXPERTS
    start_rows, end_rows = g_start * PIECE, g_end * PIECE
    t0 = start_rows // FFN_TM
    n_it = jnp.where(end_rows > start_rows, (end_rows - 1) // FFN_TM - t0 + 1, 0)
    it_end = jnp.cumsum(n_it)
    it_start = it_end - n_it
    n_exp = it_end[-1]
    tiles_used = (end_rows[-1] + FFN_TM - 1) // FFN_TM
    q = jnp.arange(n_items_max, dtype=i32)
    qc = jnp.minimum(q, n_exp - 1)[:, None]
    mine = jnp.logical_and(qc >= it_start[None, :], qc < it_end[None, :])
    pick = lambda v: jnp.sum(jnp.where(mine, v[None, :], 0), axis=1)
    e_q = pick(jnp.arange(N_EXPERTS, dtype=i32))
    tile_exp = pick(t0 - it_start) + qc[:, 0]
    tile_q = jnp.where(q < n_exp, tile_exp, jnp.minimum(tiles_used + q - n_exp, n_tiles - 1))
    lo = jnp.maximum(pick(start_rows), tile_q * FFN_TM) - tile_q * FFN_TM
    hi = jnp.minimum(pick(end_rows), (tile_q + 1) * FFN_TM) - tile_q * FFN_TM
    new_tile = jnp.concatenate([jnp.ones((1,), i32), (tile_q[1:] != tile_q[:-1]).astype(i32)])
    kind = jnp.where(q < n_exp, jnp.where(new_tile == 1, ITEM_WRITE, ITEM_MERGE),
                     jnp.where(new_tile == 1, ITEM_ZERO, ITEM_NONE))
    pos_t = pos.astype(i32).reshape(nb, TOK_BLOCK, TOP_K).transpose(0, 2, 1)
    return dict(pos_t=pos_t, dst=dst, src=src,
                item_e=e_q, item_tile=tile_q, item_lo=lo, item_hi=hi, item_kind=kind)


def _piece_copy_out(buf_ref, hbm_ref, sem, slot, j, dst_piece):
    return pltpu.make_async_copy(buf_ref.at[slot, j], hbm_ref.at[dst_piece], sem.at[slot])


def _piece_copy_in(hbm_ref, buf_ref, sem, slot, j, src_piece):
    return pltpu.make_async_copy(hbm_ref.at[src_piece], buf_ref.at[slot, j], sem.at[slot])


def _dispatch_kernel(dst_ref, hna_ref, hnb_ref, post_ref, xs_hbm, buf_ref, sem, *, nb, nb_a):
    b = pl.program_id(0)
    slot = b % 2
    hn = jnp.where(b < nb_a, hna_ref[...], hnb_ref[...])

    def wait_slot(s):
        pltpu.make_async_copy(buf_ref.at[s], xs_hbm.at[pl.ds(0, PIECES_PER_BLOCK)], sem.at[s]).wait()

    @pl.when(b >= 2)
    def _():
        wait_slot(slot)

    r = lax.broadcasted_iota(jnp.int32, (ROWS_PER_BLOCK, TOK_BLOCK), 0)
    onehot = jnp.zeros((ROWS_PER_BLOCK, TOK_BLOCK), F32)
    for k in range(TOP_K):
        onehot = jnp.where(r == post_ref[0, k:k + 1, :], 1.0, onehot)
    sorted_rows = jnp.dot(onehot.astype(BF16), hn, preferred_element_type=F32).astype(BF16)
    buf_ref[slot] = sorted_rows.reshape(PIECES_PER_BLOCK, PIECE, D_MODEL)

    def start(j, c):
        _piece_copy_out(buf_ref, xs_hbm, sem, slot, j, dst_ref[b * PIECES_PER_BLOCK + j]).start()
        return c
    lax.fori_loop(0, PIECES_PER_BLOCK, start, 0, unroll=DMA_UNROLL)

    @pl.when(b == nb - 1)
    def _():
        wait_slot(slot)
        if nb >= 2:
            wait_slot(1 - slot)


def _ffn_kernel(e_ref, tile_ref, lo_ref, hi_ref, kind_ref, x_ref, w1_ref, b1_ref, w2_ref, b2_ref, o_ref):
    q = pl.program_id(0)
    lo, hi, kind = lo_ref[q], hi_ref[q], kind_ref[q]

    def ffn():
        hu = jnp.dot(x_ref[...], w1_ref[0].astype(BF16), preferred_element_type=F32) + b1_ref[0]
        glu = jnp.minimum(hu[:, :D_FF], SWIGLU_LIMIT)
        lin = jnp.clip(hu[:, D_FF:], -SWIGLU_LIMIT, SWIGLU_LIMIT)
        act = (glu * jax.nn.sigmoid(SWIGLU_ALPHA * glu) * (lin + 1.0)).astype(BF16)
        return (jnp.dot(act, w2_ref[0].astype(BF16), preferred_element_type=F32) + b2_ref[0]).astype(BF16)

    @pl.when(kind == ITEM_WRITE)
    def _():
        o_ref[...] = ffn()

    @pl.when(kind == ITEM_MERGE)
    def _():
        row = lax.broadcasted_iota(jnp.int32, (FFN_TM, 1), 0)
        mine = jnp.logical_and(row >= lo, row < hi)
        o_ref[...] = jnp.where(mine, ffn(), o_ref[...])

    @pl.when(kind == ITEM_ZERO)
    def _():
        o_ref[...] = jnp.zeros(o_ref.shape, BF16)


def _combine_kernel(src_ref, route_ref, x1_ref, nf_ref, os_hbm, o_ref, buf_ref, sem, *, nb, blk0):
    b = pl.program_id(0)
    slot = b % 2

    def fetch(bb, s):
        def body(j, c):
            _piece_copy_in(os_hbm, buf_ref, sem, s, j, src_ref[(blk0 + bb) * PIECES_PER_BLOCK + j]).start()
            return c
        lax.fori_loop(0, PIECES_PER_BLOCK, body, 0, unroll=DMA_UNROLL)

    @pl.when(b == 0)
    def _():
        fetch(0, 0)

    @pl.when(b + 1 < nb)
    def _():
        fetch(b + 1, 1 - slot)

    pltpu.make_async_copy(os_hbm.at[pl.ds(0, PIECES_PER_BLOCK)], buf_ref.at[slot], sem.at[slot]).wait()

    r = lax.broadcasted_iota(jnp.int32, (TOK_BLOCK, ROWS_PER_BLOCK), 1).astype(F32)
    route = route_ref[...]
    wmat = jnp.zeros((TOK_BLOCK, ROWS_PER_BLOCK), F32)
    for k in range(TOP_K):
        wmat = jnp.where(r == route[:, 2 * TOP_K + k:2 * TOP_K + k + 1], route[:, TOP_K + k:TOP_K + k + 1], wmat)
    y = jnp.dot(wmat.astype(BF16), buf_ref[slot].reshape(ROWS_PER_BLOCK, D_MODEL), preferred_element_type=F32)
    x2 = x1_ref[...] + y
    ms = jnp.mean(x2 * x2, axis=-1, keepdims=True)
    o_ref[...] = x2 * lax.rsqrt(ms + EPS) * nf_ref[...]


def _moe_routed(streams, w1, b1, w2, b2, norm_final):
    (hn_a, route_a, blk_a, _), (hn_b, route_b, blk_b, _) = streams
    assert ROWS_PER_BLOCK % FFN_TM == 0
    nb_a, nb_b = blk_a.shape[0], blk_b.shape[0]
    nb = nb_a + nb_b
    rows = nb * ROWS_PER_BLOCK
    pos = jnp.concatenate([route_a[:, 2 * TOP_K:3 * TOP_K], route_b[:, 2 * TOP_K:3 * TOP_K]], axis=0)
    tb = _route_tables(pos, jnp.concatenate([blk_a, blk_b], axis=0))
    sorted_x = pl.pallas_call(
        functools.partial(_dispatch_kernel, nb=nb, nb_a=nb_a),
        grid_spec=pltpu.PrefetchScalarGridSpec(
            num_scalar_prefetch=1,
            grid=(nb,),
            in_specs=[
                pl.BlockSpec((TOK_BLOCK, D_MODEL), lambda b, d: (jnp.minimum(b, nb_a - 1), 0)),
                pl.BlockSpec((TOK_BLOCK, D_MODEL), lambda b, d: (jnp.maximum(b - nb_a, 0), 0)),
                pl.BlockSpec((1, TOP_K, TOK_BLOCK), lambda b, d: (b, 0, 0)),
            ],
            out_specs=pl.BlockSpec(memory_space=pl.ANY),
            scratch_shapes=[pltpu.VMEM((2, PIECES_PER_BLOCK, PIECE, D_MODEL), BF16),
                            pltpu.SemaphoreType.DMA((2,))],
        ),
        out_shape=jax.ShapeDtypeStruct((rows // PIECE, PIECE, D_MODEL), BF16),
        compiler_params=_cparams(("arbitrary",)),
        name="moe_dispatch",
    )(tb["dst"], hn_a, hn_b, tb["pos_t"]).reshape(rows, D_MODEL)
    n_items = rows // FFN_TM + N_EXPERTS
    sorted_o = pl.pallas_call(
        _ffn_kernel,
        grid_spec=pltpu.PrefetchScalarGridSpec(
            num_scalar_prefetch=5,
            grid=(n_items,),
            in_specs=[
                pl.BlockSpec((FFN_TM, D_MODEL), lambda q, e, ti, lo, hi, fi: (ti[q], 0)),
                pl.BlockSpec((1, D_MODEL, 2 * D_FF), lambda q, e, ti, lo, hi, fi: (e[q], 0, 0)),
                pl.BlockSpec((1, 1, 2 * D_FF), lambda q, e, ti, lo, hi, fi: (e[q], 0, 0)),
                pl.BlockSpec((1, D_FF, D_MODEL), lambda q, e, ti, lo, hi, fi: (e[q], 0, 0)),
                pl.BlockSpec((1, 1, D_MODEL), lambda q, e, ti, lo, hi, fi: (e[q], 0, 0)),
            ],
            out_specs=pl.BlockSpec((FFN_TM, D_MODEL), lambda q, e, ti, lo, hi, fi: (ti[q], 0)),
        ),
        out_shape=jax.ShapeDtypeStruct((rows, D_MODEL), BF16),
        compiler_params=_cparams(("arbitrary",)),
        name="moe_ffn",
    )(tb["item_e"], tb["item_tile"], tb["item_lo"], tb["item_hi"], tb["item_kind"], sorted_x, w1, b1, w2, b2)
    sorted_o = sorted_o.reshape(rows // PIECE, PIECE, D_MODEL)
    outs = []
    blk0 = 0
    for _, route, blk, x1 in streams:
        nb_s = blk.shape[0]
        outs.append(pl.pallas_call(
            functools.partial(_combine_kernel, nb=nb_s, blk0=blk0),
            grid_spec=pltpu.PrefetchScalarGridSpec(
                num_scalar_prefetch=1,
                grid=(nb_s,),
                in_specs=[
                    pl.BlockSpec((TOK_BLOCK, LANES), lambda b, s: (b, 0)),
                    pl.BlockSpec((TOK_BLOCK, D_MODEL), lambda b, s: (b, 0)),
                    pl.BlockSpec((1, D_MODEL), lambda b, s: (0, 0)),
                    pl.BlockSpec(memory_space=pl.ANY),
                ],
                out_specs=pl.BlockSpec((TOK_BLOCK, D_MODEL), lambda b, s: (b, 0)),
                scratch_shapes=[pltpu.VMEM((2, PIECES_PER_BLOCK, PIECE, D_MODEL), BF16),
                                pltpu.SemaphoreType.DMA((2,))],
            ),
            out_shape=jax.ShapeDtypeStruct((nb_s * TOK_BLOCK, D_MODEL), F32),
            compiler_params=_cparams(("arbitrary",)),
            name="moe_combine",
        )(tb["src"], route, x1, norm_final, sorted_o))
        blk0 += nb_s
    return outs


def _pad_lanes(v, fill=0.0):
    v = v.reshape(1, -1).astype(F32)
    return jnp.pad(v, ((0, 0), (0, LANES - v.shape[1])), constant_values=fill)


def _mixer(x3, p):
    bsz, seq, _ = x3.shape
    x = x3.reshape(bsz * seq, D_MODEL)
    proj, f_in, dt = _inproj(x, p["norm_mix"], p["w_main"], p["w_dt"])
    f_mix = _fourier_mix(f_in.reshape(bsz, seq, D_F), _dft_tables(seq)).reshape(bsz * seq, D_F)
    xbc = _conv_silu(proj, p["conv_w"], p["conv_b"], seq)
    y1, y2 = _ssd_scan(xbc, dt, p, seq)
    x1, hn, route, blk = _merge_route(y1, y2, proj, f_mix, x, p["ssm_norm"], p["w_fourier"], p["w_ssm_out"],
                                 p["w_out"], p["norm_ffn"], p["w_router"], p["b_router"])
    return hn, route, blk, x1


def kernel(x_prompt, x_sample, norm_mix, w_in, conv_w, conv_b, dt_bias_fwd, dt_bias_bwd, a_log_fwd, a_log_bwd, d_skip, ssm_norm, w_fourier, w_ssm_out, w_out, norm_ffn, w_router, b_router, w_gate_up, b_gate_up, w_down, b_down, norm_final):
    assert norm_mix.shape[0] == 1, "single-layer block"
    w = w_in[0]
    o_z, o_xbc, o_dt, o_gate = D_F, D_F + D_INNER, D_F + D_INNER + CONV_DIM, D_F + D_INNER + CONV_DIM + 2 * N_HEADS
    w_main = jnp.concatenate([w[:, o_z:o_xbc], w[:, o_xbc:o_dt], w[:, o_gate:], w[:, :D_F]], axis=1).astype(BF16)
    pad_dt = lambda m: jnp.pad(m, ((0, 0), (0, LANES - N_HEADS))).astype(BF16)
    head_of_chan = jnp.arange(D_INNER, dtype=jnp.int32) // HEADDIM
    expand = (jnp.arange(LANES, dtype=jnp.int32)[:, None] == head_of_chan[None, :]).astype(BF16)
    p = dict(
        norm_mix=norm_mix[0].reshape(1, D_MODEL),
        w_main=w_main,
        w_dt=jnp.concatenate([pad_dt(w[:, o_dt:o_dt + N_HEADS]), pad_dt(w[:, o_dt + N_HEADS:o_gate])], axis=1),
        conv_w=conv_w[0], conv_b=conv_b[0].reshape(1, CONV_DIM),
        dtb_f=_pad_lanes(dt_bias_fwd[0]), dtb_b=_pad_lanes(dt_bias_bwd[0]),
        alog_f=_pad_lanes(a_log_fwd[0], NEG_BIG), alog_b=_pad_lanes(a_log_bwd[0], NEG_BIG),
        expand=expand,
        d_skip=jnp.repeat(d_skip[0].astype(F32), HEADDIM).reshape(1, D_INNER),
        ssm_norm=ssm_norm[0].reshape(1, D_INNER),
        w_fourier=w_fourier[0].astype(BF16), w_ssm_out=w_ssm_out[0].astype(BF16), w_out=w_out[0].astype(BF16),
        norm_ffn=norm_ffn[0].reshape(1, D_MODEL),
        w_router=jnp.pad(w_router[0], ((0, 0), (0, LANES - N_EXPERTS))).astype(BF16),
        b_router=_pad_lanes(b_router[0], NEG_BIG),
        w1=w_gate_up[0], b1=b_gate_up[0].reshape(N_EXPERTS, 1, 2 * D_FF),
        w2=w_down[0], b2=b_down[0].reshape(N_EXPERTS, 1, D_MODEL),
        norm_final=norm_final.reshape(1, D_MODEL),
    )
    streams = [_mixer(x_prompt, p), _mixer(x_sample, p)]
    y_prompt, y_sample = _moe_routed(streams, p["w1"], p["b1"], p["w2"], p["b2"], p["norm_final"])
    return (y_prompt.reshape(x_prompt.shape), y_sample.reshape(x_sample.shape))
```

```python
import functools
import math

import jax
import jax.numpy as jnp
import numpy as np
from jax import lax
from jax.experimental import pallas as pl
from jax.experimental.pallas import tpu as pltpu

F32 = jnp.float32
BF16 = jnp.bfloat16

D_MODEL = 1024
D_F = 1024
FGROUP = 256
D_INNER = 2048
HEADDIM = 64
N_HEADS = 32
N_GROUPS = 8
HEADS_PER_GROUP = N_HEADS // N_GROUPS
D_STATE = 128
D_CONV = 5
CHUNK = 128
CONV_DIM = D_INNER + 2 * N_GROUPS * D_STATE
N_EXPERTS = 32
TOP_K = 4
D_FF = 1024
SWIGLU_ALPHA = 1.702
SWIGLU_LIMIT = 7.0
EPS = 1e-5
NEG_BIG = -1e30

LANES = 128
HALO = 16
CONV_SUB = 128
DFT_BLOCK_BYTES = 4 * 1024 * 1024
DFT_N1_MAX = 128
SSD_CHUNKS_PER_STEP = 4
MERGE_CHUNK = 512
VMEM_LIMIT = 56 * 1024 * 1024

TOK_BLOCK = 512
PIECE = 16
FFN_TM = 512
_WORST_PIECES = (TOK_BLOCK * TOP_K + N_EXPERTS * (PIECE - 1) + PIECE - 1) // PIECE
_PIECES_PER_TILE = FFN_TM // PIECE
PIECES_PER_BLOCK = (_WORST_PIECES + _PIECES_PER_TILE - 1) // _PIECES_PER_TILE * _PIECES_PER_TILE
ROWS_PER_BLOCK = PIECES_PER_BLOCK * PIECE
DMA_UNROLL = 8
ITEM_NONE, ITEM_WRITE, ITEM_MERGE, ITEM_ZERO = 0, 1, 2, 3

COL_Z = 0
COL_XBC = D_INNER
COL_GATE = COL_XBC + CONV_DIM
COL_F = COL_GATE + 2 * D_MODEL
PROJ_MAIN = COL_F + D_F


def _cparams(sem):
    return pltpu.CompilerParams(dimension_semantics=sem, vmem_limit_bytes=VMEM_LIMIT)


def _inproj_kernel(x_ref, nw_ref, w_ref, wdt_ref, proj_ref, f_ref, dt_ref, hn_ref, *, n_main):
    j = pl.program_id(1)

    @pl.when(j == 0)
    def _():
        x = x_ref[...]
        ms = jnp.mean(x * x, axis=-1, keepdims=True)
        hn = (x * lax.rsqrt(ms + EPS) * nw_ref[...]).astype(BF16)
        hn_ref[...] = hn
        dt_ref[...] = jnp.dot(hn, wdt_ref[...], preferred_element_type=F32)

    def tile():
        return jnp.dot(hn_ref[...], w_ref[...], preferred_element_type=F32).astype(BF16)

    @pl.when(j < n_main)
    def _():
        proj_ref[...] = tile()

    @pl.when(j == n_main)
    def _():
        f_ref[...] = tile()


def _inproj(x, norm_w, w_main, w_dt, tm=2048, tn=1024):
    t = x.shape[0]
    tm = min(tm, t)
    assert tn == D_F and COL_F % tn == 0
    n_main = COL_F // tn
    return pl.pallas_call(
        functools.partial(_inproj_kernel, n_main=n_main),
        grid=(t // tm, n_main + 1),
        in_specs=[
            pl.BlockSpec((tm, D_MODEL), lambda i, j: (i, 0)),
            pl.BlockSpec((1, D_MODEL), lambda i, j: (0, 0)),
            pl.BlockSpec((D_MODEL, tn), lambda i, j: (0, j)),
            pl.BlockSpec((D_MODEL, 2 * LANES), lambda i, j: (0, 0)),
        ],
        out_specs=[
            pl.BlockSpec((tm, tn), lambda i, j: (i, jnp.minimum(j, n_main - 1))),
            pl.BlockSpec((tm, D_F), lambda i, j: (i, 0)),
            pl.BlockSpec((tm, 2 * LANES), lambda i, j: (i, 0)),
        ],
        out_shape=[
            jax.ShapeDtypeStruct((t, COL_F), BF16),
            jax.ShapeDtypeStruct((t, D_F), BF16),
            jax.ShapeDtypeStruct((t, 2 * LANES), F32),
        ],
        scratch_shapes=[pltpu.VMEM((tm, D_MODEL), BF16)],
        compiler_params=_cparams(("parallel", "arbitrary")),
        name="inproj",
    )(x, norm_w, w_main, w_dt)


def _dft_factors(seq):
    assert seq & (seq - 1) == 0 and seq >= 256, "power-of-two sequence lengths"
    n1 = min(DFT_N1_MAX, seq // HALO)
    return n1, seq // n1


def _dft_tables(seq):
    n1, n2 = _dft_factors(seq)
    two_pi = 2.0 * math.pi
    c = jnp.arange(FGROUP, dtype=jnp.int32)
    ph = ((c[:, None] * c[None, :]) % FGROUP).astype(F32) * (two_pi / FGROUP)
    cs_chan = jnp.concatenate([jnp.cos(ph), -jnp.sin(ph)], axis=1).astype(BF16)
    k1 = jnp.arange(n1, dtype=jnp.int32)
    nn = (n2 * jnp.arange(n1, dtype=jnp.int32)[None, :] + jnp.arange(n2, dtype=jnp.int32)[:, None])
    al = ((k1[None, :, None] * nn[:, None, :]) % seq).astype(F32) * (two_pi / seq)
    ca, sa = jnp.cos(al), jnp.sin(al)
    g1 = jnp.concatenate([jnp.concatenate([ca, sa], axis=2),
                          jnp.concatenate([-sa, ca], axis=2)], axis=1).astype(BF16)
    k2 = jnp.arange(n2, dtype=jnp.int32)
    be = ((k2[:, None] * k2[None, :]) % n2).astype(F32) * (two_pi / n2)
    g2 = jnp.concatenate([jnp.cos(be), jnp.sin(be)], axis=1).astype(BF16)
    return cs_chan, g1, g2


def _dft1_kernel(x_ref, cs_ref, g_ref, o_ref, *, tn2, n1):
    for j in range(tn2):
        x = x_ref[0, j]
        parts = []
        for q in range(D_F // FGROUP):
            uv = jnp.dot(x[:, q * FGROUP:(q + 1) * FGROUP], cs_ref[...],
                         preferred_element_type=F32).astype(BF16)
            parts.append(jnp.concatenate([uv[:, :FGROUP], uv[:, FGROUP:]], axis=0))
        z = jnp.concatenate(parts, axis=1)
        o_ref[0, j] = jnp.dot(g_ref[j], z, preferred_element_type=F32).astype(BF16)


def _dft2_kernel(a_ref, g_ref, o_ref, *, tk1, pack, scale):
    n2 = o_ref.shape[2]
    for j in range(0, tk1, pack):
        a = a_ref[0, j:j + pack].reshape(pack * 2 * n2, D_F)
        out = jnp.dot(g_ref[...], a, preferred_element_type=F32) * scale
        o_ref[0, j:j + pack] = out.reshape(pack, n2, D_F).astype(BF16)


def _fourier_mix(f_in, tables):
    bsz, seq, _ = f_in.shape
    n1, n2 = _dft_factors(seq)
    cs_chan, g1, g2 = tables
    xt = f_in.reshape(bsz, n1, n2, D_F).transpose(0, 2, 1, 3)
    tn2 = min(n2, DFT_BLOCK_BYTES // (n1 * D_F * 2))
    stage1 = pl.pallas_call(
        functools.partial(_dft1_kernel, tn2=tn2, n1=n1),
        grid=(bsz, n2 // tn2),
        in_specs=[
            pl.BlockSpec((1, tn2, n1, D_F), lambda b, i: (b, i, 0, 0)),
            pl.BlockSpec((FGROUP, 2 * FGROUP), lambda b, i: (0, 0)),
            pl.BlockSpec((tn2, 2 * n1, 2 * n1), lambda b, i: (i, 0, 0)),
        ],
        out_specs=pl.BlockSpec((1, tn2, 2 * n1, D_F), lambda b, i: (b, i, 0, 0)),
        out_shape=jax.ShapeDtypeStruct((bsz, n2, 2 * n1, D_F), BF16),
        compiler_params=_cparams(("parallel", "parallel")),
        name="dft_stage1",
    )(xt, cs_chan, g1)
    a2 = (stage1.reshape(bsz, n2, 2, n1, D_F).transpose(0, 3, 2, 1, 4)
          .reshape(bsz, n1, 2 * n2, D_F))
    tk1 = min(n1, DFT_BLOCK_BYTES // (2 * n2 * D_F * 2))
    pack = min(tk1,max(1, DFT_N1_MAX // n2))
    g2_packed = jnp.kron(jnp.eye(pack, dtype=BF16), g2)
    scale = 1.0 / math.sqrt(seq * FGROUP)
    stage2 = pl.pallas_call(
        functools.partial(_dft2_kernel, tk1=tk1, pack=pack, scale=scale),
        grid=(bsz, n1 // tk1),
        in_specs=[
            pl.BlockSpec((1, tk1, 2 * n2, D_F), lambda b, i: (b, i, 0, 0)),
            pl.BlockSpec((pack * n2, pack * 2 * n2), lambda b, i: (0, 0)),
        ],
        out_specs=pl.BlockSpec((1, tk1, n2, D_F), lambda b, i: (b, i, 0, 0)),
        out_shape=jax.ShapeDtypeStruct((bsz, n1, n2, D_F), BF16),
        compiler_params=_cparams(("parallel", "parallel")),
        name="dft_stage2",
    )(a2, g2_packed)
    return stage2.transpose(0, 2, 1, 3).reshape(bsz, seq, D_F)


def _conv_kernel(prev_ref, main_ref, next_ref, w_ref, b_ref, o_ref, *, tl, tiles_per_seq):
    i = pl.program_id(0) % tiles_per_seq
    halo_zero = jnp.zeros(prev_ref.shape, BF16)
    prev = jnp.where(i == 0, halo_zero, prev_ref[...])
    nxt = jnp.where(i == tiles_per_seq - 1, halo_zero, next_ref[...])
    full = jnp.concatenate([prev, main_ref[...], nxt], axis=0)
    pad = D_CONV // 2
    win = CONV_SUB + 2 * HALO
    r = lax.broadcasted_iota(jnp.int32, (CONV_SUB, win), 0)
    c = lax.broadcasted_iota(jnp.int32, (CONV_SUB, win), 1)
    taps = [k for k in range(D_CONV) if k != pad]
    shift = jnp.concatenate([jnp.where(c == r + (HALO + k - pad), 1.0, 0.0) for k in taps], axis=1).astype(BF16)
    w_bf = w_ref[...].astype(BF16)
    for j in range(tl // CONV_SUB):
        window = full[j * CONV_SUB:j * CONV_SUB + win]
        stacked = jnp.concatenate([window * w_bf[k:k + 1, :] for k in taps], axis=0)
        acc = (b_ref[...] + w_ref[pad:pad + 1, :] * window[HALO:HALO + CONV_SUB].astype(F32)
               + jnp.dot(shift, stacked, preferred_element_type=F32))
        o_ref[j * CONV_SUB:(j + 1) * CONV_SUB, :] = (acc * jax.nn.sigmoid(acc)).astype(BF16)


def _conv_silu(proj, conv_w, conv_b, seq, tl=512, tc=2048):
    t = proj.shape[0]
    tl = min(tl, seq)
    tiles_per_seq = seq // tl
    cb0 = COL_XBC // tc
    hb = tl // HALO
    last_hb = t // HALO - 1
    return pl.pallas_call(
        functools.partial(_conv_kernel, tl=tl, tiles_per_seq=tiles_per_seq),
        grid=(t // tl, CONV_DIM // tc),
        in_specs=[
            pl.BlockSpec((HALO, tc), lambda i, c: (jnp.maximum(i * hb - 1, 0), cb0 + c)),
            pl.BlockSpec((tl, tc), lambda i, c: (i, cb0 + c)),
            pl.BlockSpec((HALO, tc), lambda i, c: (jnp.minimum((i + 1) * hb, last_hb), cb0 + c)),
            pl.BlockSpec((D_CONV, tc), lambda i, c: (0, c)),
            pl.BlockSpec((1, tc), lambda i, c: (0, c)),
        ],
        out_specs=pl.BlockSpec((tl, tc), lambda i, c: (i, c)),
        out_shape=jax.ShapeDtypeStruct((t, CONV_DIM), BF16),
        compiler_params=_cparams(("parallel", "parallel")),
        name="conv_silu",
    )(proj, proj, proj, conv_w, conv_b)


def _split3(v):
    hi = v.astype(BF16)
    r1 = v - hi.astype(F32)
    mid = r1.astype(BF16)
    lo = (r1 - mid.astype(F32)).astype(BF16)
    return [hi, mid, lo]


def _masked_sums(tri, vals):
    parts = []
    for v in vals:
        parts += _split3(v)
    out = jnp.dot(tri, jnp.concatenate(parts, axis=1), preferred_element_type=F32)
    return [out[:, (3 * i) * LANES:(3 * i + 1) * LANES] + out[:, (3 * i + 1) * LANES:(3 * i + 2) * LANES]
            + out[:, (3 * i + 2) * LANES:(3 * i + 3) * LANES] for i in range(len(vals))]


def _ssd_kernel(xs_ref, b_ref, c_ref, dtf_ref, dtb_ref, xs2_ref, b2_ref, c2_ref, dtb2_ref,
                biasf_ref, biasb_ref, alogf_ref, alogb_ref, exp_ref, dskip_ref,
                y1_ref, y2_ref, statef_ref, stateb_ref):
    @pl.when(pl.program_id(1) == 0)
    def _():
        statef_ref[...] = jnp.zeros(statef_ref.shape, F32)
        stateb_ref[...] = jnp.zeros(stateb_ref.shape, F32)

    for u in range(SSD_CHUNKS_PER_STEP):
        near = pl.ds(u * CHUNK, CHUNK)
        far = pl.ds((SSD_CHUNKS_PER_STEP - 1 - u) * CHUNK, CHUNK)
        _ssd_chunk(xs_ref.at[near], b_ref.at[near], c_ref.at[near], dtf_ref.at[near], dtb_ref.at[near],
                   xs2_ref.at[far], b2_ref.at[far], c2_ref.at[far], dtb2_ref.at[far],
                   biasf_ref, biasb_ref, alogf_ref, alogb_ref, exp_ref, dskip_ref,
                   y1_ref.at[near], y2_ref.at[far], statef_ref, stateb_ref)


def _ssd_chunk(xs_ref, b_ref, c_ref, dtf_ref, dtb_ref, xs2_ref, b2_ref, c2_ref, dtb2_ref,
               biasf_ref, biasb_ref, alogf_ref, alogb_ref, exp_ref, dskip_ref,
               y1_ref, y2_ref, statef_ref, stateb_ref):
    row = lax.broadcasted_iota(jnp.int32, (CHUNK, CHUNK), 0)
    col = lax.broadcasted_iota(jnp.int32, (CHUNK, CHUNK), 1)
    lower = row >= col
    diag = row == col
    tri_f = jnp.where(lower, 1.0, 0.0).astype(BF16)
    tri_b = jnp.where(row <= col, 1.0, 0.0).astype(BF16)

    af_neg = -jnp.exp(alogf_ref[...])
    ab_neg = -jnp.exp(alogb_ref[...])
    dt_f = jax.nn.softplus(dtf_ref[...] + biasf_ref[...])
    dt_b = jax.nn.softplus(dtb_ref[...] + biasb_ref[...])
    dt_b2 = jax.nn.softplus(dtb2_ref[...] + biasb_ref[...])
    (cf,) = _masked_sums(tri_f, [dt_f * af_neg])
    cb, cb2 = _masked_sums(tri_b, [dt_b * ab_neg, dt_b2 * ab_neg])
    tot_f = cf[CHUNK - 1:CHUNK, :]
    tot_b2 = cb2[0:1, :]
    dtb_t = dt_b.T
    srcf_t = (cf - jnp.log(dt_f)).T
    srcb_t = cb.T - jnp.log(dtb_t)

    dec = jnp.concatenate(_split3(jnp.exp(tot_f)) + _split3(jnp.exp(tot_b2))
                          + [jnp.zeros((PIECE - 6, LANES), BF16)], axis=0)
    stack = jnp.concatenate([(dt_f * jnp.exp(tot_f - cf)).astype(BF16), jnp.exp(cf).astype(BF16),
                             (dt_b2 * jnp.exp(tot_b2 - cb2)).astype(BF16), jnp.exp(cb2).astype(BF16), dec], axis=0)
    lane = lax.broadcasted_iota(jnp.int32, (CHUNK, LANES), 1)
    lo = lane < HEADDIM
    gw = HEADS_PER_GROUP * HEADDIM
    tn = (((0,), (0,)), ((), ()))
    d0 = 4 * CHUNK
    for g in range(N_GROUPS):
        gs = slice(g * gw, (g + 1) * gw)
        ns = slice(g * D_STATE, (g + 1) * D_STATE)
        ex = jnp.dot(stack, exp_ref[:, gs], preferred_element_type=F32)
        decf_e = ex[d0:d0 + 1] + ex[d0 + 1:d0 + 2] + ex[d0 + 2:d0 + 3]
        decb_e = ex[d0 + 3:d0 + 4] + ex[d0 + 4:d0 + 5] + ex[d0 + 5:d0 + 6]
        xd_f = (xs_ref[:, gs].astype(F32) * ex[0:CHUNK]).astype(BF16)
        xd_b = (xs2_ref[:, gs].astype(F32) * ex[2 * CHUNK:3 * CHUNK]).astype(BF16)
        bg, cg = b_ref[:, ns], c_ref[:, ns]
        cbm = lax.dot_general(cg, bg, (((1,), (1,)), ((), ())), preferred_element_type=F32)
        prev_f = statef_ref[g]
        y_off = jnp.dot(cg, prev_f.astype(BF16), preferred_element_type=F32) * ex[CHUNK:2 * CHUNK]
        statef_ref[g] = prev_f * decf_e + lax.dot_general(bg, xd_f, tn, preferred_element_type=F32)
        prev_b = stateb_ref[g]
        y2_ref[:, gs] = (jnp.dot(c2_ref[:, ns], prev_b.astype(BF16), preferred_element_type=F32)
                         * ex[3 * CHUNK:4 * CHUNK]).astype(BF16)
        stateb_ref[g] = prev_b * decb_e + lax.dot_general(b2_ref[:, ns], xd_b, tn, preferred_element_type=F32)
        for q in range(HEADS_PER_GROUP // 2):
            ms = []
            for hh in range(2):
                h = g * HEADS_PER_GROUP + 2 * q + hh
                seg = jnp.where(lower, cf[:, h:h + 1] - srcf_t[h:h + 1, :], cb[:, h:h + 1] - srcb_t[h:h + 1, :])
                wgt = jnp.exp(seg) + jnp.where(diag, dtb_t[h:h + 1, :], 0.0)
                ms.append((wgt * cbm).astype(BF16))
            lhs = jnp.concatenate(ms, axis=1)
            c0 = g * gw + 2 * q * HEADDIM
            xp = xs_ref[:, c0:c0 + LANES]
            zero = jnp.zeros_like(xp)
            rhs = jnp.concatenate([jnp.where(lo, xp, zero), jnp.where(lo, zero, xp)], axis=0)
            y = (jnp.dot(lhs, rhs, preferred_element_type=F32) + y_off[:, 2 * q * HEADDIM:2 * q * HEADDIM + LANES]
                 + dskip_ref[:, c0:c0 + LANES] * xp.astype(F32))
            y1_ref[:, c0:c0 + LANES] = y.astype(BF16)


def _ssd_scan(xbc, dt, p, seq):
    t = xbc.shape[0]
    rows = SSD_CHUNKS_PER_STEP * CHUNK
    assert seq % rows == 0
    ns = seq // rows
    near = lambda col: (lambda b, c: (b * ns + c, col))
    far = lambda col: (lambda b, c: (b * ns + ns - 1 - c, col))
    const = lambda b, c: (0, 0)
    gn = N_GROUPS * D_STATE
    state = pltpu.VMEM((N_GROUPS, D_STATE, HEADS_PER_GROUP * HEADDIM), F32)
    return pl.pallas_call(
        _ssd_kernel,
        grid=(t // seq, ns),
        in_specs=[
            pl.BlockSpec((rows, D_INNER), near(0)),
            pl.BlockSpec((rows, gn), near(D_INNER // gn)),
            pl.BlockSpec((rows, gn), near(D_INNER // gn + 1)),
            pl.BlockSpec((rows, LANES), near(0)),
            pl.BlockSpec((rows, LANES), near(1)),
            pl.BlockSpec((rows, D_INNER), far(0)),
            pl.BlockSpec((rows, gn), far(D_INNER // gn)),
            pl.BlockSpec((rows, gn), far(D_INNER // gn + 1)),
            pl.BlockSpec((rows, LANES), far(1)),
            pl.BlockSpec((1, LANES), const), pl.BlockSpec((1, LANES), const),
            pl.BlockSpec((1, LANES), const), pl.BlockSpec((1, LANES), const),
            pl.BlockSpec((LANES, D_INNER), const),
            pl.BlockSpec((1, D_INNER), const),
        ],
        out_specs=[pl.BlockSpec((rows, D_INNER), near(0)), pl.BlockSpec((rows, D_INNER), far(0))],
        out_shape=[jax.ShapeDtypeStruct((t, D_INNER), BF16), jax.ShapeDtypeStruct((t, D_INNER), BF16)],
        scratch_shapes=[state, state],
        compiler_params=_cparams(("parallel", "arbitrary")),
        name="ssd_scan",
    )(xbc, xbc, xbc, dt, dt, xbc, xbc, xbc, dt,
      p["dtb_f"], p["dtb_b"], p["alog_f"], p["alog_b"], p["expand"], p["d_skip"])


def _merge_kernel(y1_ref, y2_ref, z_ref, gate_ref, fm_ref, x_ref, snw_ref, wf_ref, ws_ref, wo_ref, fnw_ref,
                  wr_ref, br_ref, x1_ref, hn_ref, gates_ref, blk_ref):
    acc = None
    ssq = None
    for c0 in range(0, D_INNER, MERGE_CHUNK):
        cs = slice(c0, c0 + MERGE_CHUNK)
        z = z_ref[:, cs].astype(F32)
        yg = (y1_ref[:, cs].astype(F32) + y2_ref[:, cs].astype(F32)) * (z * jax.nn.sigmoid(z))
        sq = jnp.sum(yg * yg, axis=-1, keepdims=True)
        part = jnp.dot((yg * snw_ref[:, cs]).astype(BF16), ws_ref[cs, :], preferred_element_type=F32)
        acc = part if acc is None else acc + part
        ssq = sq if ssq is None else ssq + sq
    u_s = acc * lax.rsqrt(ssq * (1.0 / D_INNER) + EPS)
    x1 = x_ref[...]
    fm = fm_ref[...]
    for c0 in range(0, D_MODEL, MERGE_CHUNK):
        cs = slice(c0, c0 + MERGE_CHUNK)
        u_f = jnp.dot(fm, wf_ref[:, cs], preferred_element_type=F32)
        g_f = jax.nn.sigmoid(gate_ref[:, cs].astype(F32))
        g_s = jax.nn.sigmoid(gate_ref[:, D_MODEL + c0:D_MODEL + c0 + MERGE_CHUNK].astype(F32))
        merged = (g_f * u_f + g_s * u_s[:, cs]).astype(BF16)
        x1 = x1 + jnp.dot(merged, wo_ref[cs, :], preferred_element_type=F32)
    x1_ref[...] = x1
    ms1 = jnp.mean(x1 * x1, axis=-1, keepdims=True)
    hn = (x1 * lax.rsqrt(ms1 + EPS) * fnw_ref[...]).astype(BF16)
    hn_ref[...] = hn
    logits = jnp.dot(hn, wr_ref[...], preferred_element_type=F32) + br_ref[...]
    lane = lax.broadcasted_iota(jnp.int32, logits.shape, 1).astype(F32)
    tm = logits.shape[0]
    work = logits
    top = None
    denom = jnp.zeros((tm, 1), F32)
    route = jnp.zeros(logits.shape, F32)
    probs, onehots = [], []
    for k in range(TOP_K):
        m = jnp.max(work, axis=-1, keepdims=True)
        if k == 0:
            top = m
        first = jnp.min(jnp.where(work == m, lane, float(LANES)), axis=-1, keepdims=True)
        sel = lane == first
        onehots.append(jnp.where(sel, 1.0, 0.0))
        work = jnp.where(sel, NEG_BIG * 2, work)
        pk = jnp.exp(m - top)
        denom = denom + pk
        probs.append(pk)
        route = jnp.where(lane == float(k), first, route)
    inv = 1.0 / denom
    for k in range(TOP_K):
        route = jnp.where(lane == float(TOP_K + k), probs[k] * inv, route)
    ti = lax.broadcasted_iota(jnp.int32, (tm, tm), 0)
    tj = lax.broadcasted_iota(jnp.int32, (tm, tm), 1)
    earlier = jnp.where(tj < ti, 1.0, 0.0).astype(BF16)
    prefix = jnp.dot(earlier, jnp.concatenate(onehots, axis=1).astype(BF16), preferred_element_type=F32)
    cnts = [prefix[tm - 1:tm, k * LANES:(k + 1) * LANES] + onehots[k][tm - 1:tm, :] for k in range(TOP_K)]
    pieces = jnp.floor((cnts[0] + cnts[1] + cnts[2] + cnts[3] + (PIECE - 1.0)) * (1.0 / PIECE))
    ei = lax.broadcasted_iota(jnp.int32, (LANES, LANES), 0)
    ej = lax.broadcasted_iota(jnp.int32, (LANES, LANES), 1)
    lower_experts = jnp.where(ei < ej, 1.0, 0.0).astype(BF16)
    seg_start = jnp.dot(jnp.broadcast_to(pieces, (8, LANES)).astype(BF16), lower_experts,
                        preferred_element_type=F32)[0:1, :]
    base = seg_start * float(PIECE)
    for k in range(TOP_K):
        row_of = base + prefix[:, k * LANES:(k + 1) * LANES]
        pos = jnp.sum(onehots[k] * row_of, axis=-1, keepdims=True)
        route = jnp.where(lane == float(2 * TOP_K + k), pos, route)
        base = base + cnts[k]
    gates_ref[...] = route
    blk_ref[0] = jnp.concatenate([pieces, seg_start, jnp.zeros((6, LANES), F32)], axis=0)


def _merge_route(y1, y2, proj, f_mix, x, ssm_norm, w_fourier, w_ssm_out, w_out, norm_ffn, w_router, b_router):
    t = x.shape[0]
    tm = TOK_BLOCK
    assert t % tm == 0
    full = lambda r, c: pl.BlockSpec((r, c), lambda i: (0, 0))
    return pl.pallas_call(
        _merge_kernel,
        grid=(t // tm,),
        in_specs=[
            pl.BlockSpec((tm, D_INNER), lambda i: (i, 0)),
            pl.BlockSpec((tm, D_INNER), lambda i: (i, 0)),
            pl.BlockSpec((tm, D_INNER), lambda i: (i, COL_Z // D_INNER)),
            pl.BlockSpec((tm, 2 * D_MODEL), lambda i: (i, COL_GATE // (2 * D_MODEL))),
            pl.BlockSpec((tm, D_F), lambda i: (i, 0)),
            pl.BlockSpec((tm, D_MODEL), lambda i: (i, 0)),
            full(1, D_INNER), full(D_F, D_MODEL), full(D_INNER, D_MODEL), full(D_MODEL, D_MODEL),
            full(1, D_MODEL), full(D_MODEL, LANES), full(1, LANES),
        ],
        out_specs=[
            pl.BlockSpec((tm, D_MODEL), lambda i: (i, 0)),
            pl.BlockSpec((tm, D_MODEL), lambda i: (i, 0)),
            pl.BlockSpec((tm, LANES), lambda i: (i, 0)),
            pl.BlockSpec((1, 8, LANES), lambda i: (i, 0, 0)),
        ],
        out_shape=[
            jax.ShapeDtypeStruct((t, D_MODEL), F32),
            jax.ShapeDtypeStruct((t, D_MODEL), BF16),
            jax.ShapeDtypeStruct((t, LANES), F32),
            jax.ShapeDtypeStruct((t // tm, 8, LANES), F32),
        ],
        compiler_params=_cparams(("parallel",)),
        name="merge_route",
    )(y1, y2, proj, proj, f_mix, x, ssm_norm, w_fourier, w_ssm_out, w_out, norm_ffn, w_router, b_router)


def _route_tables(pos, blk):
    i32 = jnp.int32
    nb = blk.shape[0]
    pcs = blk[:, 0, :N_EXPERTS].astype(i32)
    seg_start = blk[:, 1, :N_EXPERTS].astype(i32)
    seg_end = seg_start + pcs
    used = seg_end[:, -1]
    per_e = jnp.sum(pcs, axis=0)
    g_end = jnp.cumsum(per_e)
    g_start = g_end - per_e
    b_prefix = jnp.cumsum(pcs, axis=0) - pcs
    j = jnp.arange(PIECES_PER_BLOCK, dtype=i32)[None, :, None]
    in_seg = jnp.logical_and(j >= seg_start[:, None, :], j < seg_end[:, None, :])
    shift = (g_start[None, :] + b_prefix - seg_start)[:, None, :]
    j2 = j[:, :, 0]
    dst_used = jnp.sum(jnp.where(in_seg, shift, 0), axis=2) + j2
    free = PIECES_PER_BLOCK - used
    dst_unused = g_end[-1] + (jnp.cumsum(free) - free)[:, None] + j2 - used[:, None]
    is_used = j2 < used[:, None]
    dst = jnp.where(is_used, dst_used, dst_unused).reshape(-1)
    src = jnp.where(is_used, dst_used, 0).reshape(-1)
    n_tiles = nb * ROWS_PER_BLOCK // FFN_TM
    n_items_max = n_tiles + N_EXPERTS
    start_rows, end_rows = g_start * PIECE, g_end * PIECE
    t0 = start_rows // FFN_TM
    n_it = jnp.where(end_rows > start_rows, (end_rows - 1) // FFN_TM - t0 + 1, 0)
    it_end = jnp.cumsum(n_it)
    it_start = it_end - n_it
    n_exp = it_end[-1]
    tiles_used = (end_rows[-1] + FFN_TM - 1) // FFN_TM
    q = jnp.arange(n_items_max, dtype=i32)
    qc = jnp.minimum(q, n_exp - 1)[:, None]
    mine = jnp.logical_and(qc >= it_start[None, :], qc < it_end[None, :])
    pick = lambda v: jnp.sum(jnp.where(mine, v[None, :], 0), axis=1)
    e_q = pick(jnp.arange(N_EXPERTS, dtype=i32))
    tile_exp = pick(t0 - it_start) + qc[:, 0]
    tile_q = jnp.where(q < n_exp, tile_exp, jnp.minimum(tiles_used + q - n_exp, n_tiles - 1))
    lo = jnp.maximum(pick(start_rows), tile_q * FFN_TM) - tile_q * FFN_TM
    hi = jnp.minimum(pick(end_rows), (tile_q + 1) * FFN_TM) - tile_q * FFN_TM
    new_tile = jnp.concatenate([jnp.ones((1,), i32), (tile_q[1:] != tile_q[:-1]).astype(i32)])
    kind = jnp.where(q < n_exp, jnp.where(new_tile == 1, ITEM_WRITE, ITEM_MERGE),
                     jnp.where(new_tile == 1, ITEM_ZERO, ITEM_NONE))
    pos_t = pos.astype(i32).reshape(nb, TOK_BLOCK, TOP_K).transpose(0, 2, 1)
    return dict(pos_t=pos_t, dst=dst, src=src,
                item_e=e_q, item_tile=tile_q, item_lo=lo, item_hi=hi, item_kind=kind)


def _piece_copy_out(buf_ref, hbm_ref, sem, slot, j, dst_piece):
    return pltpu.make_async_copy(buf_ref.at[slot, j], hbm_ref.at[dst_piece], sem.at[slot])


def _piece_copy_in(hbm_ref, buf_ref, sem, slot, j, src_piece):
    return pltpu.make_async_copy(hbm_ref.at[src_piece], buf_ref.at[slot, j], sem.at[slot])


def _dispatch_kernel(dst_ref, hna_ref, hnb_ref, post_ref, xs_hbm, buf_ref, sem, *, nb, nb_a):
    b = pl.program_id(0)
    slot = b % 2
    hn = jnp.where(b < nb_a, hna_ref[...], hnb_ref[...])

    def wait_slot(s):
        pltpu.make_async_copy(buf_ref.at[s], xs_hbm.at[pl.ds(0, PIECES_PER_BLOCK)], sem.at[s]).wait()

    @pl.when(b >= 2)
    def _():
        wait_slot(slot)

    r = lax.broadcasted_iota(jnp.int32, (ROWS_PER_BLOCK, TOK_BLOCK), 0)
    onehot = jnp.zeros((ROWS_PER_BLOCK, TOK_BLOCK), F32)
    for k in range(TOP_K):
        onehot = jnp.where(r == post_ref[0, k:k + 1, :], 1.0, onehot)
    sorted_rows = jnp.dot(onehot.astype(BF16), hn, preferred_element_type=F32).astype(BF16)
    buf_ref[slot] = sorted_rows.reshape(PIECES_PER_BLOCK, PIECE, D_MODEL)

    def start(jj, c):
        for u in range(DMA_UNROLL):
            j = jj * DMA_UNROLL + u
            _piece_copy_out(buf_ref, xs_hbm, sem, slot, j, dst_ref[b * PIECES_PER_BLOCK + j]).start(priority=u % 2)
        return c
    lax.fori_loop(0, PIECES_PER_BLOCK // DMA_UNROLL, start, 0)

    @pl.when(b == nb - 1)
    def _():
        wait_slot(slot)
        if nb >= 2:
            wait_slot(1 - slot)


def _ffn_kernel(e_ref, tile_ref, lo_ref, hi_ref, kind_ref, x_ref, w1_ref, b1_ref, w2_ref, b2_ref, o_ref):
    q = pl.program_id(0)
    lo, hi, kind = lo_ref[q], hi_ref[q], kind_ref[q]

    def ffn():
        hu = jnp.dot(x_ref[...], w1_ref[0].astype(BF16), preferred_element_type=F32) + b1_ref[0]
        glu = jnp.minimum(hu[:, :D_FF], SWIGLU_LIMIT)
        lin = jnp.clip(hu[:, D_FF:], -SWIGLU_LIMIT, SWIGLU_LIMIT)
        act = (glu * jax.nn.sigmoid(SWIGLU_ALPHA * glu) * (lin + 1.0)).astype(BF16)
        return (jnp.dot(act, w2_ref[0].astype(BF16), preferred_element_type=F32) + b2_ref[0]).astype(BF16)

    @pl.when(kind == ITEM_WRITE)
    def _():
        o_ref[...] = ffn()

    @pl.when(kind == ITEM_MERGE)
    def _():
        row = lax.broadcasted_iota(jnp.int32, (FFN_TM, 1), 0)
        mine = jnp.logical_and(row >= lo, row < hi)
        o_ref[...] = jnp.where(mine, ffn(), o_ref[...])

    @pl.when(kind == ITEM_ZERO)
    def _():
        o_ref[...] = jnp.zeros(o_ref.shape, BF16)


def _combine_kernel(src_ref, route_ref, x1_ref, nf_ref, os_hbm, o_ref, buf_ref, sem, *, nb, blk0):
    b = pl.program_id(0)
    slot = b % 2

    def fetch(bb, s):
        def body(jj, c):
            for u in range(DMA_UNROLL):
                j = jj * DMA_UNROLL + u
                _piece_copy_in(os_hbm, buf_ref, sem, s, j,
                               src_ref[(blk0 + bb) * PIECES_PER_BLOCK + j]).start(priority=u % 2)
            return c
        lax.fori_loop(0, PIECES_PER_BLOCK // DMA_UNROLL, body, 0)

    @pl.when(b == 0)
    def _():
        fetch(0, 0)

    @pl.when(b + 1 < nb)
    def _():
        fetch(b + 1, 1 - slot)

    pltpu.make_async_copy(os_hbm.at[pl.ds(0, PIECES_PER_BLOCK)], buf_ref.at[slot], sem.at[slot]).wait()

    r = lax.broadcasted_iota(jnp.int32, (TOK_BLOCK, ROWS_PER_BLOCK), 1).astype(F32)
    route = route_ref[...]
    wmat = jnp.zeros((TOK_BLOCK, ROWS_PER_BLOCK), F32)
    for k in range(TOP_K):
        wmat = jnp.where(r == route[:, 2 * TOP_K + k:2 * TOP_K + k + 1], route[:, TOP_K + k:TOP_K + k + 1], wmat)
    y = jnp.dot(wmat.astype(BF16), buf_ref[slot].reshape(ROWS_PER_BLOCK, D_MODEL), preferred_element_type=F32)
    x2 = x1_ref[...] + y
    ms = jnp.mean(x2 * x2, axis=-1, keepdims=True)
    o_ref[...] = x2 * lax.rsqrt(ms + EPS) * nf_ref[...]


def _moe_routed(streams, w1, b1, w2, b2, norm_final):
    (hn_a, route_a, blk_a, _), (hn_b, route_b, blk_b, _) = streams
    assert ROWS_PER_BLOCK % FFN_TM == 0
    nb_a, nb_b = blk_a.shape[0], blk_b.shape[0]
    nb = nb_a + nb_b
    rows = nb * ROWS_PER_BLOCK
    pos = jnp.concatenate([route_a[:, 2 * TOP_K:3 * TOP_K], route_b[:, 2 * TOP_K:3 * TOP_K]], axis=0)
    tb = _route_tables(pos, jnp.concatenate([blk_a, blk_b], axis=0))
    sorted_x = pl.pallas_call(
        functools.partial(_dispatch_kernel, nb=nb, nb_a=nb_a),
        grid_spec=pltpu.PrefetchScalarGridSpec(
            num_scalar_prefetch=1,
            grid=(nb,),
            in_specs=[
                pl.BlockSpec((TOK_BLOCK, D_MODEL), lambda b, d: (jnp.minimum(b, nb_a - 1), 0)),
                pl.BlockSpec((TOK_BLOCK, D_MODEL), lambda b, d: (jnp.maximum(b - nb_a, 0), 0)),
                pl.BlockSpec((1, TOP_K, TOK_BLOCK), lambda b, d: (b, 0, 0)),
            ],
            out_specs=pl.BlockSpec(memory_space=pl.ANY),
            scratch_shapes=[pltpu.VMEM((2, PIECES_PER_BLOCK, PIECE, D_MODEL), BF16),
                            pltpu.SemaphoreType.DMA((2,))],
        ),
        out_shape=jax.ShapeDtypeStruct((rows // PIECE, PIECE, D_MODEL), BF16),
        compiler_params=_cparams(("arbitrary",)),
        name="moe_dispatch",
    )(tb["dst"], hn_a, hn_b, tb["pos_t"]).reshape(rows, D_MODEL)
    n_items = rows // FFN_TM + N_EXPERTS
    sorted_o = pl.pallas_call(
        _ffn_kernel,
        grid_spec=pltpu.PrefetchScalarGridSpec(
            num_scalar_prefetch=5,
            grid=(n_items,),
            in_specs=[
                pl.BlockSpec((FFN_TM, D_MODEL), lambda q, e, ti, lo, hi, fi: (ti[q], 0)),
                pl.BlockSpec((1, D_MODEL, 2 * D_FF), lambda q, e, ti, lo, hi, fi: (e[q], 0, 0)),
                pl.BlockSpec((1, 1, 2 * D_FF), lambda q, e, ti, lo, hi, fi: (e[q], 0, 0)),
                pl.BlockSpec((1, D_FF, D_MODEL), lambda q, e, ti, lo, hi, fi: (e[q], 0, 0)),
                pl.BlockSpec((1, 1, D_MODEL), lambda q, e, ti, lo, hi, fi: (e[q], 0, 0)),
            ],
            out_specs=pl.BlockSpec((FFN_TM, D_MODEL), lambda q, e, ti, lo, hi, fi: (ti[q], 0)),
        ),
        out_shape=jax.ShapeDtypeStruct((rows, D_MODEL), BF16),
        compiler_params=_cparams(("arbitrary",)),
        name="moe_ffn",
    )(tb["item_e"], tb["item_tile"], tb["item_lo"], tb["item_hi"], tb["item_kind"], sorted_x, w1, b1, w2, b2)
    sorted_o = sorted_o.reshape(rows // PIECE, PIECE, D_MODEL)
    outs = []
    blk0 = 0
    for _, route, blk, x1 in streams:
        nb_s = blk.shape[0]
        outs.append(pl.pallas_call(
            functools.partial(_combine_kernel, nb=nb_s, blk0=blk0),
            grid_spec=pltpu.PrefetchScalarGridSpec(
                num_scalar_prefetch=1,
                grid=(nb_s,),
                in_specs=[
                    pl.BlockSpec((TOK_BLOCK, LANES), lambda b, s: (b, 0)),
                    pl.BlockSpec((TOK_BLOCK, D_MODEL), lambda b, s: (b, 0)),
                    pl.BlockSpec((1, D_MODEL), lambda b, s: (0, 0)),
                    pl.BlockSpec(memory_space=pl.ANY),
                ],
                out_specs=pl.BlockSpec((TOK_BLOCK, D_MODEL), lambda b, s: (b, 0)),
                scratch_shapes=[pltpu.VMEM((2, PIECES_PER_BLOCK, PIECE, D_MODEL), BF16),
                                pltpu.SemaphoreType.DMA((2,))],
            ),
            out_shape=jax.ShapeDtypeStruct((nb_s * TOK_BLOCK, D_MODEL), F32),
            compiler_params=_cparams(("arbitrary",)),
            name="moe_combine",
        )(tb["src"], route, x1, norm_final, sorted_o))
        blk0 += nb_s
    return outs


def _pad_lanes(v, fill=0.0):
    v = v.reshape(1, -1).astype(F32)
    return jnp.pad(v, ((0, 0), (0, LANES - v.shape[1])), constant_values=fill)


def _mixer(x3, p):
    bsz, seq, _ = x3.shape
    x = x3.reshape(bsz * seq, D_MODEL)
    proj, f_in, dt = _inproj(x, p["norm_mix"], p["w_main"], p["w_dt"])
    f_mix = _fourier_mix(f_in.reshape(bsz, seq, D_F), _dft_tables(seq)).reshape(bsz * seq, D_F)
    xbc = _conv_silu(proj, p["conv_w"], p["conv_b"], seq)
    y1, y2 = _ssd_scan(xbc, dt, p, seq)
    x1, hn, route, blk = _merge_route(y1, y2, proj, f_mix, x, p["ssm_norm"], p["w_fourier"], p["w_ssm_out"],
                                 p["w_out"], p["norm_ffn"], p["w_router"], p["b_router"])
    return hn, route, blk, x1


def kernel(x_prompt, x_sample, norm_mix, w_in, conv_w, conv_b, dt_bias_fwd, dt_bias_bwd, a_log_fwd, a_log_bwd, d_skip, ssm_norm, w_fourier, w_ssm_out, w_out, norm_ffn, w_router, b_router, w_gate_up, b_gate_up, w_down, b_down, norm_final):
    assert norm_mix.shape[0] == 1, "single-layer block"
    w = w_in[0]
    o_z, o_xbc, o_dt, o_gate = D_F, D_F + D_INNER, D_F + D_INNER + CONV_DIM, D_F + D_INNER + CONV_DIM + 2 * N_HEADS
    w_main = jnp.concatenate([w[:, o_z:o_xbc], w[:, o_xbc:o_dt], w[:, o_gate:], w[:, :D_F]], axis=1).astype(BF16)
    pad_dt = lambda m: jnp.pad(m, ((0, 0), (0, LANES - N_HEADS))).astype(BF16)
    head_of_chan = jnp.arange(D_INNER, dtype=jnp.int32) // HEADDIM
    expand = (jnp.arange(LANES, dtype=jnp.int32)[:, None] == head_of_chan[None, :]).astype(BF16)
    p = dict(
        norm_mix=norm_mix[0].reshape(1, D_MODEL),
        w_main=w_main,
        w_dt=jnp.concatenate([pad_dt(w[:, o_dt:o_dt + N_HEADS]), pad_dt(w[:, o_dt + N_HEADS:o_gate])], axis=1),
        conv_w=conv_w[0], conv_b=conv_b[0].reshape(1, CONV_DIM),
        dtb_f=_pad_lanes(dt_bias_fwd[0]), dtb_b=_pad_lanes(dt_bias_bwd[0]),
        alog_f=_pad_lanes(a_log_fwd[0], NEG_BIG), alog_b=_pad_lanes(a_log_bwd[0], NEG_BIG),
        expand=expand,
        d_skip=jnp.repeat(d_skip[0].astype(F32), HEADDIM).reshape(1, D_INNER),
        ssm_norm=ssm_norm[0].reshape(1, D_INNER),
        w_fourier=w_fourier[0].astype(BF16), w_ssm_out=w_ssm_out[0].astype(BF16), w_out=w_out[0].astype(BF16),
        norm_ffn=norm_ffn[0].reshape(1, D_MODEL),
        w_router=jnp.pad(w_router[0], ((0, 0), (0, LANES - N_EXPERTS))).astype(BF16),
        b_router=_pad_lanes(b_router[0], NEG_BIG),
        w1=w_gate_up[0], b1=b_gate_up[0].reshape(N_EXPERTS, 1, 2 * D_FF),
        w2=w_down[0], b2=b_down[0].reshape(N_EXPERTS, 1, D_MODEL),
        norm_final=norm_final.reshape(1, D_MODEL),
    )
    streams = [_mixer(x_prompt, p), _mixer(x_sample, p)]
    y_prompt, y_sample = _moe_routed(streams, p["w1"], p["b1"], p["w2"], p["b2"], p["norm_final"])
    return (y_prompt.reshape(x_prompt.shape), y_sample.reshape(x_sample.shape))
```
